```python
import jax, jax.numpy as jnp
from jax import lax
import numpy as np

D_MODEL = 1024
BATCH = 1
SEQ = 16384
DEPTH = 2
DEC_BATCH = 4
DEC_SEQ = 4096
PAST_LEN = 128

GRID_W = 64
WIN_ROWS = 8
WIN_COLS = 16
ATTN_HEADS = 8
HEAD_DIM = 64
ATTN_W = ATTN_HEADS * HEAD_DIM
CONV_GROUPS = 8
CONV_W = D_MODEL - ATTN_W
CONV_K = 3
PROJ_W = 3 * ATTN_W + 3 * CONV_W
N_KEYS = 128
N_EXPERTS = N_KEYS * N_KEYS
PEER_HEADS = 8
D_KEY = 256
PEER_TOPK = 16
PEER_BLOCK = 128
PLE_DIM = 256
EPS = 1e-6

kernel_name = "hymba_natten_shortconv_peer_encoder"


def rmsnorm(x, g):
    xf = x.astype(jnp.float32)
    y = xf * lax.rsqrt(jnp.mean(xf * xf, axis=-1, keepdims=True) + EPS)
    return (y * g.astype(jnp.float32)).astype(x.dtype)


def group_rmsnorm(x, g, n_groups):
    shp = x.shape
    xf = x.astype(jnp.float32).reshape(shp[:-1] + (n_groups, shp[-1] // n_groups))
    y = xf * lax.rsqrt(jnp.mean(xf * xf, axis=-1, keepdims=True) + EPS)
    return (y.reshape(shp) * g.astype(jnp.float32)).astype(x.dtype)


def neighbourhood_attention(q, k, v, rpb):
    b, n, nh, dh = q.shape
    rows = n // GRID_W
    kr = min(WIN_ROWS, rows)
    qg = jnp.moveaxis((q * (dh ** -0.5)).reshape(b, rows, GRID_W, nh, dh), 1, 0)
    kg = k.reshape(b, rows, GRID_W, nh, dh)
    vg = v.reshape(b, rows, GRID_W, nh, dh)
    cols = np.arange(GRID_W)
    col_start = np.clip(cols - WIN_COLS // 2, 0, GRID_W - WIN_COLS)
    col_idx = col_start[:, None] + np.arange(WIN_COLS)[None, :]
    col_off = col_idx - cols[:, None] + (WIN_COLS - 1)
    rpb_c = rpb[:, :, col_off]

    def row_block(args):
        r, q_r = args
        rs = jnp.clip(r - kr // 2, 0, rows - kr)
        k_rows = lax.dynamic_slice_in_dim(kg, rs, kr, axis=1)
        v_rows = lax.dynamic_slice_in_dim(vg, rs, kr, axis=1)
        k_win = k_rows[:, :, col_idx]
        v_win = v_rows[:, :, col_idx]
        row_off = rs + jnp.arange(kr) - r + (WIN_ROWS - 1)
        bias = jnp.take(rpb_c, row_off, axis=1).transpose(0, 2, 1, 3)
        s = jnp.einsum('bqhd,brqjhd->bhqrj', q_r, k_win).astype(jnp.float32) + bias.astype(jnp.float32)[None]
        pr = jax.nn.softmax(s, axis=(-2, -1)).astype(v.dtype)
        return jnp.einsum('bhqrj,brqjhd->bqhd', pr, v_win)

    out = lax.map(row_block, (jnp.arange(rows), qg))
    return jnp.moveaxis(out, 0, 1).reshape(b, n, nh * dh)


def short_conv(u, w):
    up = jnp.pad(u, ((0, 0), (1, 1), (0, 0)))
    return up[:, :-2] * w[0] + up[:, 1:-1] * w[1] + up[:, 2:] * w[2]


def peer(h, w_query, sub_keys1, sub_keys2, expert_u, expert_v):
    b, n, d = h.shape
    xb = h.reshape(-1, PEER_BLOCK, d)
    half = D_KEY // 2

    def block(x_t):
        qh = (x_t @ w_query).reshape(PEER_BLOCK, PEER_HEADS, D_KEY)
        s1 = jnp.einsum('thc,kc->thk', qh[..., :half], sub_keys1)
        s2 = jnp.einsum('thc,kc->thk', qh[..., half:], sub_keys2)
        v1, i1 = lax.top_k(s1, PEER_TOPK)
        v2, i2 = lax.top_k(s2, PEER_TOPK)
        cand_s = (v1[..., :, None] + v2[..., None, :]).reshape(PEER_BLOCK, PEER_HEADS, PEER_TOPK * PEER_TOPK)
        cand_i = (i1[..., :, None] * N_KEYS + i2[..., None, :]).reshape(PEER_BLOCK, PEER_HEADS, PEER_TOPK * PEER_TOPK)
        top_s, pos = lax.top_k(cand_s, PEER_TOPK)
        e = jnp.take_along_axis(cand_i, pos, axis=-1)
        gate = jax.nn.softmax(top_s.astype(jnp.float32), axis=-1)
        act = jax.nn.gelu(jnp.einsum('td,thkd->thk', x_t, expert_u[e]), approximate=False)
        wgt = (gate * act.astype(jnp.float32)).astype(x_t.dtype)
        return jnp.einsum('thk,thkd->td', wgt, expert_v[e])

    return lax.map(block, xb).reshape(b, n, d)


def encoder_layer(x, p_i, g_mix, w_in, rpb, conv_w, g_attn_out, g_conv_out, w_out,
                  g_ffn, w_query, sub_keys1, sub_keys2, expert_u, expert_v,
                  g_ple, w_ple_gate, w_ple_proj):
    b, n, _ = x.shape
    h = rmsnorm(x, g_mix)
    z = h @ w_in
    q, k, v, gb, gc, gu = jnp.split(z, np.cumsum([ATTN_W, ATTN_W, ATTN_W, CONV_W, CONV_W])[:].tolist(), axis=-1)
    attn = neighbourhood_attention(q.reshape(b, n, ATTN_HEADS, HEAD_DIM),
                                   k.reshape(b, n, ATTN_HEADS, HEAD_DIM),
                                   v.reshape(b, n, ATTN_HEADS, HEAD_DIM), rpb)
    conv = gb * short_conv(gc * gu, conv_w)
    merged = jnp.concatenate([group_rmsnorm(attn, g_attn_out, ATTN_HEADS),
                              group_rmsnorm(conv, g_conv_out, CONV_GROUPS)], axis=-1)
    x = x + merged @ w_out
    x = x + peer(rmsnorm(x, g_ffn), w_query, sub_keys1, sub_keys2, expert_u, expert_v)
    hp = rmsnorm(x, g_ple)
    x = x + jax.nn.sigmoid(hp @ w_ple_gate) * (p_i @ w_ple_proj)
    return x


def trunk(x, p, g_mix, w_in, rpb, conv_w, g_attn_out, g_conv_out, w_out,
          g_ffn, w_query, sub_keys1, sub_keys2, expert_u, expert_v,
          g_ple, w_ple_gate, w_ple_proj, g_final):
    for i in range(DEPTH):
        x = encoder_layer(x, p[i], g_mix[i], w_in[i], rpb[i], conv_w[i], g_attn_out[i], g_conv_out[i],
                          w_out[i], g_ffn[i], w_query[i], sub_keys1[i], sub_keys2[i],
                          expert_u[i], expert_v[i], g_ple[i], w_ple_gate[i], w_ple_proj[i])
    return rmsnorm(x, g_final)


def setup_inputs(seed: int = 0) -> dict:
    key = jax.random.key(seed)
    ks = jax.random.split(key, 24)
    f32 = jnp.float32
    nrm = lambda k, shp, s: jax.random.normal(k, shp, f32) * s
    gain = lambda k, shp: 1.0 + 0.02 * jax.random.normal(k, shp, f32)
    return {
        "x_prompt": nrm(ks[0], (BATCH, SEQ, D_MODEL), 1.0),
        "x_sample": nrm(ks[1], (DEC_BATCH, DEC_SEQ, D_MODEL), 1.0),
        "p_prompt": nrm(ks[2], (DEPTH, BATCH, SEQ, PLE_DIM), 1.0),
        "p_sample": nrm(ks[3], (DEPTH, DEC_BATCH, DEC_SEQ, PLE_DIM), 1.0),
        "g_mix": gain(ks[4], (DEPTH, D_MODEL)),
        "w_in": nrm(ks[5], (DEPTH, D_MODEL, PROJ_W), D_MODEL ** -0.5),
        "rpb": nrm(ks[6], (DEPTH, ATTN_HEADS, 2 * WIN_ROWS - 1, 2 * WIN_COLS - 1), 0.1),
        "conv_w": nrm(ks[7], (DEPTH, CONV_K, CONV_W), CONV_K ** -0.5),
        "g_attn_out": gain(ks[8], (DEPTH, ATTN_W)),
        "g_conv_out": gain(ks[9], (DEPTH, CONV_W)),
        "w_out": nrm(ks[10], (DEPTH, ATTN_W + CONV_W, D_MODEL), (ATTN_W + CONV_W) ** -0.5),
        "g_ffn": gain(ks[11], (DEPTH, D_MODEL)),
        "w_query": nrm(ks[12], (DEPTH, D_MODEL, PEER_HEADS * D_KEY), D_MODEL ** -0.5),
        "sub_keys1": nrm(ks[13], (DEPTH, N_KEYS, D_KEY // 2), (D_KEY // 2) ** -0.5),
        "sub_keys2": nrm(ks[14], (DEPTH, N_KEYS, D_KEY // 2), (D_KEY // 2) ** -0.5),
        "expert_u": nrm(ks[15], (DEPTH, N_EXPERTS, D_MODEL), D_MODEL ** -0.5),
        "expert_v": nrm(ks[16], (DEPTH, N_EXPERTS, D_MODEL), PEER_HEADS ** -0.5),
        "g_ple": gain(ks[17], (DEPTH, D_MODEL)),
        "w_ple_gate": nrm(ks[18], (DEPTH, D_MODEL, D_MODEL), D_MODEL ** -0.5),
        "w_ple_proj": nrm(ks[19], (DEPTH, PLE_DIM, D_MODEL), PLE_DIM ** -0.5),
        "g_final": gain(ks[20], (D_MODEL,)),
    }


def reference(x_prompt, x_sample, p_prompt, p_sample, g_mix, w_in, rpb, conv_w, g_attn_out, g_conv_out,
              w_out, g_ffn, w_query, sub_keys1, sub_keys2, expert_u, expert_v, g_ple, w_ple_gate,
              w_ple_proj, g_final):
    y_prompt = trunk(x_prompt, p_prompt, g_mix, w_in, rpb, conv_w, g_attn_out, g_conv_out, w_out,
                     g_ffn, w_query, sub_keys1, sub_keys2, expert_u, expert_v,
                     g_ple, w_ple_gate, w_ple_proj, g_final)
    y_sample = trunk(x_sample, p_sample, g_mix, w_in, rpb, conv_w, g_attn_out, g_conv_out, w_out,
                     g_ffn, w_query, sub_keys1, sub_keys2, expert_u, expert_v,
                     g_ple, w_ple_gate, w_ple_proj, g_final)
    return (y_prompt, y_sample)
```

```python
import functools

import numpy as np
import jax
import jax.numpy as jnp
from jax import lax
from jax.experimental import pallas as pl
from jax.experimental.pallas import tpu as pltpu

F32 = jnp.float32
BF16 = jnp.bfloat16

D_MODEL = 1024
DEPTH = 2
SEQ = 16384
DEC_BATCH = 4
DEC_SEQ = 4096
N_TOK = SEQ + DEC_BATCH * DEC_SEQ

GRID_W = 64
WIN_ROWS = 8
WIN_COLS = 16
ATTN_HEADS = 8
HEAD_DIM = 64
ATTN_W = ATTN_HEADS * HEAD_DIM
CONV_W = D_MODEL - ATTN_W
N_KEYS = 128
N_EXPERTS = N_KEYS * N_KEYS
PEER_HEADS = 8
D_KEY = 256
PEER_TOPK = 16
PLE_DIM = 256
EPS = 1e-6
NEG_BIG = -1e30

LANES = 128
ROW_BLOCK = WIN_ROWS
TOK_BLOCK = ROW_BLOCK * GRID_W
N_BLOCKS = N_TOK // TOK_BLOCK
PROMPT_ROWS = SEQ // GRID_W
SAMPLE_ROWS = DEC_SEQ // GRID_W
PROMPT_BLOCKS = PROMPT_ROWS // ROW_BLOCK
SAMPLE_BLOCKS = SAMPLE_ROWS // ROW_BLOCK
HALO = 8

ROUTE_TOK = 256
PEER_TOK = 256
PEER_CHUNK_KEYS = 8
PEER_CHUNK = PEER_CHUNK_KEYS * N_KEYS
PEER_N_CHUNKS = N_EXPERTS // PEER_CHUNK

VMEM_LIMIT = 48 * 1024 * 1024


def _rms(x, g):
    return x * lax.rsqrt(jnp.mean(x * x, axis=-1, keepdims=True) + EPS) * g


def _dot_nt(a, b):
    return lax.dot_general(a, b, (((1,), (1,)), ((), ())), preferred_element_type=F32)


def _proj_kernel(x_ref, g_ref, w_ref, q_ref, k_ref, v_ref, gb_ref, gc_ref, gu_ref):
    hb = _rms(x_ref[...], g_ref[...]).astype(BF16)
    outs = (q_ref, k_ref, v_ref, gb_ref, gc_ref, gu_ref)
    for j, o_ref in enumerate(outs):
        z = jnp.dot(hb, w_ref[:, j * ATTN_W:(j + 1) * ATTN_W], preferred_element_type=F32)
        if j == 0:
            z = z * (HEAD_DIM ** -0.5)
        o_ref[...] = z.astype(o_ref.dtype)


def _proj(x, g, w_in):
    tok = pl.BlockSpec((TOK_BLOCK, ATTN_W), lambda i: (i, 0))
    return pl.pallas_call(
        _proj_kernel,
        grid=(N_BLOCKS,),
        in_specs=[pl.BlockSpec((TOK_BLOCK, D_MODEL), lambda i: (i, 0)),
                  pl.BlockSpec((1, D_MODEL), lambda i: (0, 0)),
                  pl.BlockSpec((D_MODEL, 6 * ATTN_W), lambda i: (0, 0))],
        out_specs=[tok] * 6,
        out_shape=[jax.ShapeDtypeStruct((N_TOK, ATTN_W), BF16)] * 3
                  + [jax.ShapeDtypeStruct((N_TOK, ATTN_W), F32)] * 3,
        compiler_params=pltpu.CompilerParams(dimension_semantics=("arbitrary",),
                                             vmem_limit_bytes=VMEM_LIMIT),
        name="proj",
    )(x, g, w_in)


def _group_norm(xv, gmat, g):
    sq = xv * xv
    hi = sq.astype(BF16)
    lo = (sq - hi.astype(F32)).astype(BF16)
    ms = (jnp.dot(hi, gmat, preferred_element_type=F32)
          + jnp.dot(lo, gmat, preferred_element_type=F32)) * (1.0 / HEAD_DIM)
    return xv * lax.rsqrt(ms + EPS) * g


def _mixer_kernel(x_ref, q_ref, kp_ref, kc_ref, kn_ref, vp_ref, vc_ref, vn_ref,
                  gb_ref, gc_ref, gu_ref, gcp_ref, gup_ref, gcn_ref, gun_ref,
                  bias_ref, cw_ref, ga_ref, gcv_ref, gmat_ref, wo_ref,
                  o_ref, kbuf, vbuf, abuf):
    i = pl.program_id(0)
    is_prompt = i < PROMPT_BLOCKS
    sample = (i - PROMPT_BLOCKS) // SAMPLE_BLOCKS
    seq_r0 = jnp.where(is_prompt, 0, PROMPT_ROWS + SAMPLE_ROWS * sample)
    seq_r1 = jnp.where(is_prompt, PROMPT_ROWS, seq_r0 + SAMPLE_ROWS)
    blk_r0 = ROW_BLOCK * i

    kbuf[0:TOK_BLOCK, :] = kp_ref[...]
    kbuf[TOK_BLOCK:2 * TOK_BLOCK, :] = kc_ref[...]
    kbuf[2 * TOK_BLOCK:3 * TOK_BLOCK, :] = kn_ref[...]
    vbuf[0:TOK_BLOCK, :] = vp_ref[...]
    vbuf[TOK_BLOCK:2 * TOK_BLOCK, :] = vc_ref[...]
    vbuf[2 * TOK_BLOCK:3 * TOK_BLOCK, :] = vn_ref[...]

    lane = lax.broadcasted_iota(jnp.int32, (GRID_W, LANES), 1)
    low_half = lane < HEAD_DIM
    n_win = WIN_ROWS * GRID_W
    for jr in range(ROW_BLOCK):
        r = blk_r0 + jr
        rs = jnp.clip(r - WIN_ROWS // 2, seq_r0, seq_r1 - WIN_ROWS)
        delta = r - rs
        start = pl.multiple_of((rs - blk_r0 + ROW_BLOCK) * GRID_W, GRID_W)
        for p in range(ATTN_HEADS // 2):
            cols = slice(p * LANES, (p + 1) * LANES)
            kw = kbuf[pl.ds(start, n_win), cols]
            vw = vbuf[pl.ds(start, n_win), cols]
            qp = q_ref[jr * GRID_W:(jr + 1) * GRID_W, cols]
            outs = []
            for hh in range(2):
                qm = jnp.where(low_half if hh == 0 else jnp.logical_not(low_half), qp, jnp.zeros_like(qp))
                s = _dot_nt(qm, kw) + bias_ref[delta, 2 * p + hh]
                m = jnp.max(s, axis=-1, keepdims=True)
                e = jnp.exp(s - m)
                l = jnp.sum(e, axis=-1, keepdims=True)
                o = jnp.dot(e.astype(BF16), vw, preferred_element_type=F32)
                outs.append(o / l)
            abuf[jr * GRID_W:(jr + 1) * GRID_W, cols] = jnp.where(low_half, outs[0], outs[1])

    first = blk_r0 == seq_r0
    last = blk_r0 + ROW_BLOCK == seq_r1
    cu = gc_ref[...] * gu_ref[...]
    prev_row = jnp.where(first, 0.0, gcp_ref[HALO - 1:HALO, :] * gup_ref[HALO - 1:HALO, :])
    next_row = jnp.where(last, 0.0, gcn_ref[0:1, :] * gun_ref[0:1, :])
    row = lax.broadcasted_iota(jnp.int32, (TOK_BLOCK, CONV_W), 0)
    up_prev = jnp.where(row == 0, prev_row, pltpu.roll(cu, 1, axis=0))
    up_next = jnp.where(row == TOK_BLOCK - 1, next_row, pltpu.roll(cu, TOK_BLOCK - 1, axis=0))
    conv = gb_ref[...] * (up_prev * cw_ref[0:1, :] + cu * cw_ref[1:2, :] + up_next * cw_ref[2:3, :])

    gmat = gmat_ref[...]
    attn_n = _group_norm(abuf[...], gmat, ga_ref[...]).astype(BF16)
    conv_n = _group_norm(conv, gmat, gcv_ref[...]).astype(BF16)
    y = (jnp.dot(attn_n, wo_ref[0:ATTN_W, :], preferred_element_type=F32)
         + jnp.dot(conv_n, wo_ref[ATTN_W:D_MODEL, :], preferred_element_type=F32))
    o_ref[...] = x_ref[...] + y


def _mixer(x, q, k, v, gb, gc, gu, bias, conv_w, g_attn, g_conv, gmat, w_out):
    cur = lambda i: (i, 0)
    prev = lambda i: (jnp.maximum(i - 1, 0), 0)
    nxt = lambda i: (jnp.minimum(i + 1, N_BLOCKS - 1), 0)
    halo_per_block = TOK_BLOCK // HALO
    hprev = lambda i: (jnp.maximum(i * halo_per_block - 1, 0), 0)
    hnext = lambda i: (jnp.minimum((i + 1) * halo_per_block, N_TOK // HALO - 1), 0)
    const2 = lambda i: (0, 0)
    blk = lambda m: pl.BlockSpec((TOK_BLOCK, ATTN_W), m)
    halo = lambda m: pl.BlockSpec((HALO, CONV_W), m)
    return pl.pallas_call(
        _mixer_kernel,
        grid=(N_BLOCKS,),
        in_specs=[pl.BlockSpec((TOK_BLOCK, D_MODEL), cur),
                  blk(cur), blk(prev), blk(cur), blk(nxt), blk(prev), blk(cur), blk(nxt),
                  blk(cur), blk(cur), blk(cur), halo(hprev), halo(hprev), halo(hnext), halo(hnext),
                  pl.BlockSpec((WIN_ROWS, ATTN_HEADS, GRID_W, WIN_ROWS * GRID_W), lambda i: (0, 0, 0, 0)),
                  pl.BlockSpec((3, CONV_W), const2),
                  pl.BlockSpec((1, ATTN_W), const2),
                  pl.BlockSpec((1, CONV_W), const2),
                  pl.BlockSpec((ATTN_W, ATTN_W), const2),
                  pl.BlockSpec((D_MODEL, D_MODEL), const2)],
        out_specs=pl.BlockSpec((TOK_BLOCK, D_MODEL), cur),
        out_shape=jax.ShapeDtypeStruct((N_TOK, D_MODEL), F32),
        scratch_shapes=[pltpu.VMEM((3 * TOK_BLOCK, ATTN_W), BF16),
                        pltpu.VMEM((3 * TOK_BLOCK, ATTN_W), BF16),
                        pltpu.VMEM((TOK_BLOCK, ATTN_W), F32)],
        compiler_params=pltpu.CompilerParams(dimension_semantics=("arbitrary",),
                                             vmem_limit_bytes=VMEM_LIMIT),
        name="mixer",
    )(x, q, k, k, k, v, v, v, gb, gc, gu, gc, gu, gc, gu, bias, conv_w, g_attn, g_conv, gmat, w_out)


def _topk_rows(s, payload):
    rows, t = s.shape
    ridx = lax.broadcasted_iota(jnp.int32, (rows, t), 0).astype(F32)
    kidx = lax.broadcasted_iota(jnp.int32, (PEER_TOPK, t), 0)

    def body(k, carry):
        s, vals, picks = carry
        m = jnp.max(s, axis=0, keepdims=True)
        pos = jnp.min(jnp.where(s == m, ridx, float(rows)), axis=0, keepdims=True)
        onehot = ridx == pos
        if payload is None:
            pick = pos
        else:
            pick = jnp.max(jnp.where(onehot, payload, -1.0), axis=0, keepdims=True)
        s = jnp.where(onehot, -jnp.inf, s)
        sel = kidx == k
        return s, jnp.where(sel, m, vals), jnp.where(sel, pick, picks)

    zeros = jnp.zeros((PEER_TOPK, t), F32)
    _, vals, picks = lax.fori_loop(0, PEER_TOPK, body, (s, zeros, zeros))
    return vals, picks


def _route_kernel(x_ref, g_ref, wq_ref, k1_ref, k2_ref, h_ref, a_ref, b_ref, gate_ref, gt_buf, et_buf):
    hb = _rms(x_ref[...], g_ref[...]).astype(BF16)
    h_ref[...] = hb
    half = D_KEY // 2
    for h in range(PEER_HEADS):
        qh = jnp.dot(hb, wq_ref[:, h * D_KEY:(h + 1) * D_KEY], preferred_element_type=F32).astype(BF16)
        s1 = _dot_nt(k1_ref[...], qh[:, :half])
        s2 = _dot_nt(k2_ref[...], qh[:, half:])
        v1, i1 = _topk_rows(s1, None)
        v2, i2 = _topk_rows(s2, None)
        cand_s = jnp.concatenate([v1[ka:ka + 1, :] + v2 for ka in range(PEER_TOPK)], axis=0)
        cand_e = jnp.concatenate([i1[ka:ka + 1, :] * float(N_KEYS) + i2 for ka in range(PEER_TOPK)], axis=0)
        top_s, top_e = _topk_rows(cand_s, cand_e)
        ex = jnp.exp(top_s - jnp.max(top_s, axis=0, keepdims=True))
        gt_buf[h * PEER_TOPK:(h + 1) * PEER_TOPK, :] = ex / jnp.sum(ex, axis=0, keepdims=True)
        et_buf[h * PEER_TOPK:(h + 1) * PEER_TOPK, :] = top_e
    e = et_buf[...].T
    a = jnp.floor(e * (1.0 / N_KEYS))
    a_ref[...] = a
    b_ref[...] = e - a * float(N_KEYS)
    gate_ref[...] = gt_buf[...].T


def _route(x, g, w_query, keys1, keys2):
    n_blk = N_TOK // ROUTE_TOK
    hk = PEER_HEADS * PEER_TOPK
    tok = lambda w: pl.BlockSpec((ROUTE_TOK, w), lambda i: (i, 0))
    const2 = lambda i: (0, 0)
    return pl.pallas_call(
        _route_kernel,
        grid=(n_blk,),
        in_specs=[tok(D_MODEL),
                  pl.BlockSpec((1, D_MODEL), const2),
                  pl.BlockSpec((D_MODEL, PEER_HEADS * D_KEY), const2),
                  pl.BlockSpec((N_KEYS, D_KEY // 2), const2),
                  pl.BlockSpec((N_KEYS, D_KEY // 2), const2)],
        out_specs=[tok(D_MODEL), tok(hk), tok(hk), tok(hk)],
        out_shape=[jax.ShapeDtypeStruct((N_TOK, D_MODEL), BF16)]
                  + [jax.ShapeDtypeStruct((N_TOK, hk), F32)] * 3,
        scratch_shapes=[pltpu.VMEM((hk, ROUTE_TOK), F32), pltpu.VMEM((hk, ROUTE_TOK), F32)],
        compiler_params=pltpu.CompilerParams(dimension_semantics=("arbitrary",),
                                             vmem_limit_bytes=VMEM_LIMIT),
        name="route",
    )(x, g, w_query, keys1, keys2)


def _gelu_exact(x):
    return 0.5 * x * (1.0 + lax.erf(x * (2.0 ** -0.5)))


def _peer_kernel(h_ref, x_ref, a_ref, b_ref, gate_ref, u_ref, v_ref, o_ref, wtok, acc):
    c = pl.program_id(1)

    @pl.when(c == 0)
    def _():
        acc[...] = jnp.zeros_like(acc)
        sub = lax.broadcasted_iota(jnp.int32, (N_KEYS, LANES), 0).astype(F32)

        def tok(t, carry):
            ar = a_ref[pl.ds(t, 1), :]
            br = b_ref[pl.ds(t, 1), :]
            gr = gate_ref[pl.ds(t, 1), :]
            pg = jnp.where(sub == ar, gr, 0.0).astype(BF16)
            qb = jnp.where(sub == br, 1.0, 0.0).astype(BF16)
            wtok[pl.ds(pl.multiple_of(t * N_KEYS, N_KEYS), N_KEYS), :] = _dot_nt(pg, qb)
            return carry

        lax.fori_loop(0, PEER_TOK, tok, 0)

    a_act = _dot_nt(h_ref[...], u_ref[...])
    w = jnp.concatenate(
        [wtok[pl.ds(c * PEER_CHUNK_KEYS + j, PEER_TOK, stride=N_KEYS), :] for j in range(PEER_CHUNK_KEYS)],
        axis=1)
    wp = (w * _gelu_exact(a_act)).astype(BF16)
    acc[...] += jnp.dot(wp, v_ref[...], preferred_element_type=F32)

    @pl.when(c == PEER_N_CHUNKS - 1)
    def _():
        o_ref[...] = x_ref[...] + acc[...]


def _peer(h, x, a, b, gate, eu, ev):
    n_blk = N_TOK // PEER_TOK
    hk = PEER_HEADS * PEER_TOPK
    tok = lambda w: pl.BlockSpec((PEER_TOK, w), lambda i, c: (i, 0))
    chunk = pl.BlockSpec((PEER_CHUNK, D_MODEL), lambda i, c: (c, 0))
    return pl.pallas_call(
        _peer_kernel,
        grid=(n_blk, PEER_N_CHUNKS),
        in_specs=[tok(D_MODEL), tok(D_MODEL), tok(hk), tok(hk), tok(hk), chunk, chunk],
        out_specs=tok(D_MODEL),
        out_shape=jax.ShapeDtypeStruct((N_TOK, D_MODEL), F32),
        scratch_shapes=[pltpu.VMEM((PEER_TOK * N_KEYS, N_KEYS), F32),
                        pltpu.VMEM((PEER_TOK, D_MODEL), F32)],
        compiler_params=pltpu.CompilerParams(dimension_semantics=("arbitrary", "arbitrary"),
                                             vmem_limit_bytes=VMEM_LIMIT),
        name="peer",
    )(h, x, a, b, gate, eu, ev)


def _ple_kernel(x_ref, p_ref, g_ref, wg_ref, wp_ref, gf_ref, o_ref, *, final):
    x = x_ref[...]
    hp = _rms(x, g_ref[...]).astype(BF16)
    gate = jax.nn.sigmoid(jnp.dot(hp, wg_ref[...], preferred_element_type=F32))
    proj = jnp.dot(p_ref[...].astype(BF16), wp_ref[...], preferred_element_type=F32)
    y = x + gate * proj
    if final:
        y = _rms(y, gf_ref[...])
    o_ref[...] = y


def _ple(x, p, g, w_gate, w_proj, g_final, final):
    const2 = lambda i: (0, 0)
    return pl.pallas_call(
        functools.partial(_ple_kernel, final=final),
        grid=(N_BLOCKS,),
        in_specs=[pl.BlockSpec((TOK_BLOCK, D_MODEL), lambda i: (i, 0)),
                  pl.BlockSpec((TOK_BLOCK, PLE_DIM), lambda i: (i, 0)),
                  pl.BlockSpec((1, D_MODEL), const2),
                  pl.BlockSpec((D_MODEL, D_MODEL), const2),
                  pl.BlockSpec((PLE_DIM, D_MODEL), const2),
                  pl.BlockSpec((1, D_MODEL), const2)],
        out_specs=pl.BlockSpec((TOK_BLOCK, D_MODEL), lambda i: (i, 0)),
        out_shape=jax.ShapeDtypeStruct((N_TOK, D_MODEL), F32),
        compiler_params=pltpu.CompilerParams(dimension_semantics=("arbitrary",),
                                             vmem_limit_bytes=VMEM_LIMIT),
        name="ple",
    )(x, p, g, w_gate, w_proj, g_final)


def _bias_table(rpb):
    cols = np.arange(GRID_W)
    col_start = np.clip(cols - WIN_COLS // 2, 0, GRID_W - WIN_COLS)
    kc = np.arange(GRID_W)
    in_win = (kc[None, :] >= col_start[:, None]) & (kc[None, :] < col_start[:, None] + WIN_COLS)
    col_off = np.clip(kc[None, :] - cols[:, None] + (WIN_COLS - 1), 0, 2 * WIN_COLS - 2)
    delta = np.arange(WIN_ROWS)
    j = np.arange(WIN_ROWS)
    row_off = j[None, :] - delta[:, None] + (WIN_ROWS - 1)
    t = rpb[:, row_off][:, :, :, col_off]
    t = jnp.where(in_win[None, None, None], t, NEG_BIG)
    t = t.transpose(1, 0, 3, 2, 4)
    return t.reshape(WIN_ROWS, ATTN_HEADS, GRID_W, WIN_ROWS * GRID_W).astype(F32)


def kernel(x_prompt, x_sample, p_prompt, p_sample, g_mix, w_in, rpb, conv_w, g_attn_out, g_conv_out, w_out,
           g_ffn, w_query, sub_keys1, sub_keys2, expert_u, expert_v, g_ple, w_ple_gate, w_ple_proj, g_final):
    x = jnp.concatenate([x_prompt.reshape(SEQ, D_MODEL), x_sample.reshape(DEC_BATCH * DEC_SEQ, D_MODEL)], axis=0)
    p = jnp.concatenate([p_prompt.reshape(DEPTH, SEQ, PLE_DIM),
                         p_sample.reshape(DEPTH, DEC_BATCH * DEC_SEQ, PLE_DIM)], axis=1)
    group = np.arange(ATTN_W) // HEAD_DIM
    gmat = jnp.asarray(group[:, None] == group[None, :], dtype=BF16)
    row = lambda g: g.reshape(1, -1)
    for i in range(DEPTH):
        q, k, v, gb, gc, gu = _proj(x, row(g_mix[i]), w_in[i].astype(BF16))
        x = _mixer(x, q, k, v, gb, gc, gu, _bias_table(rpb[i]), conv_w[i], row(g_attn_out[i]),
                   row(g_conv_out[i]), gmat, w_out[i].astype(BF16))
        h, a, b, gate = _route(x, row(g_ffn[i]), w_query[i].astype(BF16),
                               sub_keys1[i].astype(BF16), sub_keys2[i].astype(BF16))
        x = _peer(h, x, a, b, gate, expert_u[i].astype(BF16), expert_v[i].astype(BF16))
        x = _ple(x, p[i], row(g_ple[i]), w_ple_gate[i].astype(BF16), w_ple_proj[i].astype(BF16),
                 row(g_final), final=(i == DEPTH - 1))
    y_prompt = x[:SEQ].reshape(1, SEQ, D_MODEL)
    y_sample = x[SEQ:].reshape(DEC_BATCH, DEC_SEQ, D_MODEL)
    return (y_prompt, y_sample)
```

```python
import functools

import numpy as np
import jax
import jax.numpy as jnp
from jax import lax
from jax.experimental import pallas as pl
from jax.experimental.pallas import tpu as pltpu

F32 = jnp.float32
BF16 = jnp.bfloat16

D_MODEL = 1024
DEPTH = 2
SEQ = 16384
DEC_BATCH = 4
DEC_SEQ = 4096
N_TOK = SEQ + DEC_BATCH * DEC_SEQ

GRID_W = 64
WIN_ROWS = 8
WIN_COLS = 16
ATTN_HEADS = 8
HEAD_DIM = 64
ATTN_W = ATTN_HEADS * HEAD_DIM
CONV_W = D_MODEL - ATTN_W
N_KEYS = 128
N_EXPERTS = N_KEYS * N_KEYS
PEER_HEADS = 8
D_KEY = 256
PEER_TOPK = 16
PLE_DIM = 256
EPS = 1e-6
NEG_BIG = -1e30

LANES = 128
ROW_BLOCK = WIN_ROWS
TOK_BLOCK = ROW_BLOCK * GRID_W
N_BLOCKS = N_TOK // TOK_BLOCK
PROMPT_ROWS = SEQ // GRID_W
SAMPLE_ROWS = DEC_SEQ // GRID_W
PROMPT_BLOCKS = PROMPT_ROWS // ROW_BLOCK
SAMPLE_BLOCKS = SAMPLE_ROWS // ROW_BLOCK
HALO = 8

ROUTE_TOK = 256
PEER_TOK = 256
PEER_CHUNK_KEYS = 16
PEER_CHUNK = PEER_CHUNK_KEYS * N_KEYS
PEER_KEY_GROUPS = 1
PEER_GROUP_KEYS = N_KEYS // PEER_KEY_GROUPS
PEER_GROUP_CHUNKS = PEER_GROUP_KEYS // PEER_CHUNK_KEYS

VMEM_LIMIT = 48 * 1024 * 1024


def _rms(x, g):
    return x * lax.rsqrt(jnp.mean(x * x, axis=-1, keepdims=True) + EPS) * g


def _dot_nt(a, b):
    return lax.dot_general(a, b, (((1,), (1,)), ((), ())), preferred_element_type=F32)


def _proj_kernel(x_ref, g_ref, w_ref, q_ref, k_ref, v_ref, gb_ref, gc_ref, gu_ref):
    hb = _rms(x_ref[...], g_ref[...]).astype(BF16)
    outs = (q_ref, k_ref, v_ref, gb_ref, gc_ref, gu_ref)
    for j, o_ref in enumerate(outs):
        z = jnp.dot(hb, w_ref[:, j * ATTN_W:(j + 1) * ATTN_W], preferred_element_type=F32)
        if j == 0:
            z = z * (HEAD_DIM ** -0.5)
        o_ref[...] = z.astype(o_ref.dtype)


def _proj(x, g, w_in):
    tok = pl.BlockSpec((TOK_BLOCK, ATTN_W), lambda i: (i, 0))
    return pl.pallas_call(
        _proj_kernel,
        grid=(N_BLOCKS,),
        in_specs=[pl.BlockSpec((TOK_BLOCK, D_MODEL), lambda i: (i, 0)),
                  pl.BlockSpec((1, D_MODEL), lambda i: (0, 0)),
                  pl.BlockSpec((D_MODEL, 6 * ATTN_W), lambda i: (0, 0))],
        out_specs=[tok] * 6,
        out_shape=[jax.ShapeDtypeStruct((N_TOK, ATTN_W), BF16)] * 3
                  + [jax.ShapeDtypeStruct((N_TOK, ATTN_W), F32)] * 3,
        compiler_params=pltpu.CompilerParams(dimension_semantics=("arbitrary",),
                                             vmem_limit_bytes=VMEM_LIMIT),
        name="proj",
    )(x, g, w_in)


def _group_norm(xv, gmat, g):
    sq = xv * xv
    hi = sq.astype(BF16)
    lo = (sq - hi.astype(F32)).astype(BF16)
    ms = (jnp.dot(hi, gmat, preferred_element_type=F32)
          + jnp.dot(lo, gmat, preferred_element_type=F32)) * (1.0 / HEAD_DIM)
    return xv * lax.rsqrt(ms + EPS) * g


def _mixer_kernel(x_ref, q_ref, kp_ref, kc_ref, kn_ref, vp_ref, vc_ref, vn_ref,
                  gb_ref, gc_ref, gu_ref, gcp_ref, gup_ref, gcn_ref, gun_ref,
                  bias_ref, cw_ref, ga_ref, gcv_ref, gmat_ref, wo_ref,
                  o_ref, kbuf, vbuf, abuf):
    i = pl.program_id(0)
    is_prompt = i < PROMPT_BLOCKS
    sample = (i - PROMPT_BLOCKS) // SAMPLE_BLOCKS
    seq_r0 = jnp.where(is_prompt, 0, PROMPT_ROWS + SAMPLE_ROWS * sample)
    seq_r1 = jnp.where(is_prompt, PROMPT_ROWS, seq_r0 + SAMPLE_ROWS)
    blk_r0 = ROW_BLOCK * i

    kbuf[0:TOK_BLOCK, :] = kp_ref[...]
    kbuf[TOK_BLOCK:2 * TOK_BLOCK, :] = kc_ref[...]
    kbuf[2 * TOK_BLOCK:3 * TOK_BLOCK, :] = kn_ref[...]
    vbuf[0:TOK_BLOCK, :] = vp_ref[...]
    vbuf[TOK_BLOCK:2 * TOK_BLOCK, :] = vc_ref[...]
    vbuf[2 * TOK_BLOCK:3 * TOK_BLOCK, :] = vn_ref[...]

    lane = lax.broadcasted_iota(jnp.int32, (GRID_W, LANES), 1)
    low_half = lane < HEAD_DIM
    n_win = WIN_ROWS * GRID_W
    for jr in range(ROW_BLOCK):
        r = blk_r0 + jr
        rs = jnp.clip(r - WIN_ROWS // 2, seq_r0, seq_r1 - WIN_ROWS)
        delta = r - rs
        start = pl.multiple_of((rs - blk_r0 + ROW_BLOCK) * GRID_W, GRID_W)
        for p in range(ATTN_HEADS // 2):
            cols = slice(p * LANES, (p + 1) * LANES)
            kw = kbuf[pl.ds(start, n_win), cols]
            vw = vbuf[pl.ds(start, n_win), cols]
            qp = q_ref[jr * GRID_W:(jr + 1) * GRID_W, cols]
            outs = []
            for hh in range(2):
                qm = jnp.where(low_half if hh == 0 else jnp.logical_not(low_half), qp, jnp.zeros_like(qp))
                s = _dot_nt(qm, kw) + bias_ref[delta, 2 * p + hh]
                m = jnp.max(s, axis=-1, keepdims=True)
                e = jnp.exp(s - m)
                l = jnp.sum(e, axis=-1, keepdims=True)
                o = jnp.dot(e.astype(BF16), vw, preferred_element_type=F32)
                outs.append(o / l)
            abuf[jr * GRID_W:(jr + 1) * GRID_W, cols] = jnp.where(low_half, outs[0], outs[1])

    first = blk_r0 == seq_r0
    last = blk_r0 + ROW_BLOCK == seq_r1
    cu = gc_ref[...] * gu_ref[...]
    prev_row = jnp.where(first, 0.0, gcp_ref[HALO - 1:HALO, :] * gup_ref[HALO - 1:HALO, :])
    next_row = jnp.where(last, 0.0, gcn_ref[0:1, :] * gun_ref[0:1, :])
    row = lax.broadcasted_iota(jnp.int32, (TOK_BLOCK, CONV_W), 0)
    up_prev = jnp.where(row == 0, prev_row, pltpu.roll(cu, 1, axis=0))
    up_next = jnp.where(row == TOK_BLOCK - 1, next_row, pltpu.roll(cu, TOK_BLOCK - 1, axis=0))
    conv = gb_ref[...] * (up_prev * cw_ref[0:1, :] + cu * cw_ref[1:2, :] + up_next * cw_ref[2:3, :])

    gmat = gmat_ref[...]
    attn_n = _group_norm(abuf[...], gmat, ga_ref[...]).astype(BF16)
    conv_n = _group_norm(conv, gmat, gcv_ref[...]).astype(BF16)
    y = (jnp.dot(attn_n, wo_ref[0:ATTN_W, :], preferred_element_type=F32)
         + jnp.dot(conv_n, wo_ref[ATTN_W:D_MODEL, :], preferred_element_type=F32))
    o_ref[...] = x_ref[...] + y


def _mixer(x, q, k, v, gb, gc, gu, bias, conv_w, g_attn, g_conv, gmat, w_out):
    cur = lambda i: (i, 0)
    prev = lambda i: (jnp.maximum(i - 1, 0), 0)
    nxt = lambda i: (jnp.minimum(i + 1, N_BLOCKS - 1), 0)
    halo_per_block = TOK_BLOCK // HALO
    hprev = lambda i: (jnp.maximum(i * halo_per_block - 1, 0), 0)
    hnext = lambda i: (jnp.minimum((i + 1) * halo_per_block, N_TOK // HALO - 1), 0)
    const2 = lambda i: (0, 0)
    blk = lambda m: pl.BlockSpec((TOK_BLOCK, ATTN_W), m)
    halo = lambda m: pl.BlockSpec((HALO, CONV_W), m)
    return pl.pallas_call(
        _mixer_kernel,
        grid=(N_BLOCKS,),
        in_specs=[pl.BlockSpec((TOK_BLOCK, D_MODEL), cur),
                  blk(cur), blk(prev), blk(cur), blk(nxt), blk(prev), blk(cur), blk(nxt),
                  blk(cur), blk(cur), blk(cur), halo(hprev), halo(hprev), halo(hnext), halo(hnext),
                  pl.BlockSpec((WIN_ROWS, ATTN_HEADS, GRID_W, WIN_ROWS * GRID_W), lambda i: (0, 0, 0, 0)),
                  pl.BlockSpec((3, CONV_W), const2),
                  pl.BlockSpec((1, ATTN_W), const2),
                  pl.BlockSpec((1, CONV_W), const2),
                  pl.BlockSpec((ATTN_W, ATTN_W), const2),
                  pl.BlockSpec((D_MODEL, D_MODEL), const2)],
        out_specs=pl.BlockSpec((TOK_BLOCK, D_MODEL), cur),
        out_shape=jax.ShapeDtypeStruct((N_TOK, D_MODEL), F32),
        scratch_shapes=[pltpu.VMEM((3 * TOK_BLOCK, ATTN_W), BF16),
                        pltpu.VMEM((3 * TOK_BLOCK, ATTN_W), BF16),
                        pltpu.VMEM((TOK_BLOCK, ATTN_W), F32)],
        compiler_params=pltpu.CompilerParams(dimension_semantics=("arbitrary",),
                                             vmem_limit_bytes=VMEM_LIMIT),
        name="mixer",
    )(x, q, k, k, k, v, v, v, gb, gc, gu, gc, gu, gc, gu, bias, conv_w, g_attn, g_conv, gmat, w_out)


def _record(k, kidx, m, pick, vals, picks):
    sel = kidx == k
    return jnp.where(sel, m, vals), jnp.where(sel, pick, picks)


def _top_keys(s):
    n, t = s.shape
    half = n // 2
    r0 = lax.broadcasted_iota(jnp.int32, (half, t), 0).astype(F32)
    r1 = r0 + float(half)
    first = s[:half] >= s[half:]
    hi = jnp.where(first, s[:half], s[half:])
    lo = jnp.where(first, s[half:], s[:half])
    ihi = jnp.where(first, r0, r1)
    ilo = jnp.where(first, r1, r0)
    kidx = lax.broadcasted_iota(jnp.int32, (PEER_TOPK, t), 0)

    def body(k, carry):
        hi, lo, ihi, vals, picks = carry
        m = jnp.max(hi, axis=0, keepdims=True)
        pos = jnp.min(jnp.where(hi == m, ihi, float(n)), axis=0, keepdims=True)
        onehot = ihi == pos
        vals, picks = _record(k, kidx, m, pos, vals, picks)
        return (jnp.where(onehot, lo, hi), jnp.where(onehot, -jnp.inf, lo), jnp.where(onehot, ilo, ihi),
                vals, picks)

    zeros = jnp.zeros((PEER_TOPK, t), F32)
    out = lax.fori_loop(0, PEER_TOPK, body, (hi, lo, ihi, zeros, zeros))
    return out[3], out[4]


_CAND_PER_KA = [PEER_TOPK // (ka + 1) for ka in range(PEER_TOPK)]
_N_CAND = sum(_CAND_PER_KA)
_CAND_PAD = -_N_CAND % 8


def _top_candidates(v1, i1, v2, i2):
    t = v1.shape[1]
    s_rows, e_rows = [], []
    for ka, n_kb in enumerate(_CAND_PER_KA):
        s_rows.append(v1[ka:ka + 1, :] + v2[:n_kb, :])
        e_rows.append(i1[ka:ka + 1, :] * float(N_KEYS) + i2[:n_kb, :])
    s_rows.append(jnp.full((_CAND_PAD, t), -jnp.inf, F32))
    e_rows.append(jnp.zeros((_CAND_PAD, t), F32))
    s = jnp.concatenate(s_rows, axis=0)
    e = jnp.concatenate(e_rows, axis=0)
    rows = s.shape[0]
    ridx = lax.broadcasted_iota(jnp.int32, (rows, t), 0).astype(F32)
    kidx = lax.broadcasted_iota(jnp.int32, (PEER_TOPK, t), 0)

    def body(k, carry):
        s, vals, picks = carry
        m = jnp.max(s, axis=0, keepdims=True)
        pos = jnp.min(jnp.where(s == m, ridx, float(rows)), axis=0, keepdims=True)
        onehot = ridx == pos
        pick = jnp.max(jnp.where(onehot, e, -1.0), axis=0, keepdims=True)
        vals, picks = _record(k, kidx, m, pick, vals, picks)
        return jnp.where(onehot, -jnp.inf, s), vals, picks

    zeros = jnp.zeros((PEER_TOPK, t), F32)
    _, vals, picks = lax.fori_loop(0, PEER_TOPK, body, (s, zeros, zeros))
    return vals, picks


def _route_kernel(x_ref, g_ref, wq_ref, k1_ref, k2_ref, h_ref, a_ref, b_ref, gate_ref, gt_buf, et_buf):
    hb = _rms(x_ref[...], g_ref[...]).astype(BF16)
    h_ref[...] = hb
    half = D_KEY // 2
    for h in range(PEER_HEADS):
        qh = jnp.dot(hb, wq_ref[:, h * D_KEY:(h + 1) * D_KEY], preferred_element_type=F32).astype(BF16)
        s1 = _dot_nt(k1_ref[...], qh[:, :half])
        s2 = _dot_nt(k2_ref[...], qh[:, half:])
        v1, i1 = _top_keys(s1)
        v2, i2 = _top_keys(s2)
        top_s, top_e = _top_candidates(v1, i1, v2, i2)
        ex = jnp.exp(top_s - jnp.max(top_s, axis=0, keepdims=True))
        gt_buf[h * PEER_TOPK:(h + 1) * PEER_TOPK, :] = ex / jnp.sum(ex, axis=0, keepdims=True)
        et_buf[h * PEER_TOPK:(h + 1) * PEER_TOPK, :] = top_e
    e = et_buf[...].T
    a = jnp.floor(e * (1.0 / N_KEYS))
    a_ref[...] = a
    b_ref[...] = e - a * float(N_KEYS)
    gate_ref[...] = gt_buf[...].T


def _route(x, g, w_query, keys1, keys2):
    n_blk = N_TOK // ROUTE_TOK
    hk = PEER_HEADS * PEER_TOPK
    tok = lambda w: pl.BlockSpec((ROUTE_TOK, w), lambda i: (i, 0))
    const2 = lambda i: (0, 0)
    return pl.pallas_call(
        _route_kernel,
        grid=(n_blk,),
        in_specs=[tok(D_MODEL),
                  pl.BlockSpec((1, D_MODEL), const2),
                  pl.BlockSpec((D_MODEL, PEER_HEADS * D_KEY), const2),
                  pl.BlockSpec((N_KEYS, D_KEY // 2), const2),
                  pl.BlockSpec((N_KEYS, D_KEY // 2), const2)],
        out_specs=[tok(D_MODEL), tok(hk), tok(hk), tok(hk)],
        out_shape=[jax.ShapeDtypeStruct((N_TOK, D_MODEL), BF16)]
                  + [jax.ShapeDtypeStruct((N_TOK, hk), F32)] * 3,
        scratch_shapes=[pltpu.VMEM((hk, ROUTE_TOK), F32), pltpu.VMEM((hk, ROUTE_TOK), F32)],
        compiler_params=pltpu.CompilerParams(dimension_semantics=("arbitrary",),
                                             vmem_limit_bytes=VMEM_LIMIT),
        name="route",
    )(x, g, w_query, keys1, keys2)


def _gelu_exact(x):
    return 0.5 * x * (1.0 + lax.erf(x * (2.0 ** -0.5)))


def _peer_kernel(h_ref, x_ref, a_ref, b_ref, gate_ref, u_ref, v_ref, o_ref, wtok, acc):
    grp = pl.program_id(1)
    c = pl.program_id(2)

    @pl.when(jnp.logical_and(grp == 0, c == 0))
    def _():
        acc[...] = jnp.zeros_like(acc)

    @pl.when(c == 0)
    def _():
        key = (lax.broadcasted_iota(jnp.int32, (PEER_GROUP_KEYS, LANES), 0) + grp * PEER_GROUP_KEYS).astype(F32)
        sub = lax.broadcasted_iota(jnp.int32, (N_KEYS, LANES), 0).astype(F32)

        def tok(t, carry):
            ar = a_ref[pl.ds(t, 1), :]
            br = b_ref[pl.ds(t, 1), :]
            gr = gate_ref[pl.ds(t, 1), :]
            pg = jnp.where(key == ar, gr, 0.0).astype(BF16)
            qb = jnp.where(sub == br, 1.0, 0.0).astype(BF16)
            wtok[t] = _dot_nt(pg, qb)
            return carry

        lax.fori_loop(0, PEER_TOK, tok, 0, unroll=32)

    a_act = _dot_nt(h_ref[...], u_ref[...])
    wk = wtok[:, pl.ds(pl.multiple_of(c * PEER_CHUNK_KEYS, PEER_CHUNK_KEYS), PEER_CHUNK_KEYS), :]
    wk = jnp.swapaxes(wk, 0, 1)
    w = jnp.concatenate([wk[j] for j in range(PEER_CHUNK_KEYS)], axis=1)
    wp = (w * _gelu_exact(a_act)).astype(BF16)
    acc[...] += jnp.dot(wp, v_ref[...], preferred_element_type=F32)

    @pl.when(jnp.logical_and(grp == PEER_KEY_GROUPS - 1, c == PEER_GROUP_CHUNKS - 1))
    def _():
        o_ref[...] = x_ref[...] + acc[...]


def _peer(h, x, a, b, gate, eu, ev):
    n_blk = N_TOK // PEER_TOK
    hk = PEER_HEADS * PEER_TOPK
    tok = lambda w: pl.BlockSpec((PEER_TOK, w), lambda i, g, c: (i, 0))
    chunk = pl.BlockSpec((PEER_CHUNK, D_MODEL), lambda i, g, c: (g * PEER_GROUP_CHUNKS + c, 0))
    return pl.pallas_call(
        _peer_kernel,
        grid=(n_blk, PEER_KEY_GROUPS, PEER_GROUP_CHUNKS),
        in_specs=[tok(D_MODEL), tok(D_MODEL), tok(hk), tok(hk), tok(hk), chunk, chunk],
        out_specs=tok(D_MODEL),
        out_shape=jax.ShapeDtypeStruct((N_TOK, D_MODEL), F32),
        scratch_shapes=[pltpu.VMEM((PEER_TOK, PEER_GROUP_KEYS, N_KEYS), F32),
                        pltpu.VMEM((PEER_TOK, D_MODEL), F32)],
        compiler_params=pltpu.CompilerParams(dimension_semantics=("arbitrary", "arbitrary", "arbitrary"),
                                             vmem_limit_bytes=VMEM_LIMIT),
        name="peer",
    )(h, x, a, b, gate, eu, ev)


def _ple_kernel(x_ref, p_ref, g_ref, wg_ref, wp_ref, gf_ref, o_ref, *, final):
    x = x_ref[...]
    hp = _rms(x, g_ref[...]).astype(BF16)
    gate = jax.nn.sigmoid(jnp.dot(hp, wg_ref[...], preferred_element_type=F32))
    proj = jnp.dot(p_ref[...].astype(BF16), wp_ref[...], preferred_element_type=F32)
    y = x + gate * proj
    if final:
        y = _rms(y, gf_ref[...])
    o_ref[...] = y


def _ple(x, p, g, w_gate, w_proj, g_final, final):
    const2 = lambda i: (0, 0)
    return pl.pallas_call(
        functools.partial(_ple_kernel, final=final),
        grid=(N_BLOCKS,),
        in_specs=[pl.BlockSpec((TOK_BLOCK, D_MODEL), lambda i: (i, 0)),
                  pl.BlockSpec((TOK_BLOCK, PLE_DIM), lambda i: (i, 0)),
                  pl.BlockSpec((1, D_MODEL), const2),
                  pl.BlockSpec((D_MODEL, D_MODEL), const2),
                  pl.BlockSpec((PLE_DIM, D_MODEL), const2),
                  pl.BlockSpec((1, D_MODEL), const2)],
        out_specs=pl.BlockSpec((TOK_BLOCK, D_MODEL), lambda i: (i, 0)),
        out_shape=jax.ShapeDtypeStruct((N_TOK, D_MODEL), F32),
        compiler_params=pltpu.CompilerParams(dimension_semantics=("arbitrary",),
                                             vmem_limit_bytes=VMEM_LIMIT),
        name="ple",
    )(x, p, g, w_gate, w_proj, g_final)


def _bias_table(rpb):
    cols = np.arange(GRID_W)
    col_start = np.clip(cols - WIN_COLS // 2, 0, GRID_W - WIN_COLS)
    kc = np.arange(GRID_W)
    in_win = (kc[None, :] >= col_start[:, None]) & (kc[None, :] < col_start[:, None] + WIN_COLS)
    col_off = np.clip(kc[None, :] - cols[:, None] + (WIN_COLS - 1), 0, 2 * WIN_COLS - 2)
    delta = np.arange(WIN_ROWS)
    j = np.arange(WIN_ROWS)
    row_off = j[None, :] - delta[:, None] + (WIN_ROWS - 1)
    t = rpb[:, row_off][:, :, :, col_off]
    t = jnp.where(in_win[None, None, None], t, NEG_BIG)
    t = t.transpose(1, 0, 3, 2, 4)
    return t.reshape(WIN_ROWS, ATTN_HEADS, GRID_W, WIN_ROWS * GRID_W).astype(F32)


def kernel(x_prompt, x_sample, p_prompt, p_sample, g_mix, w_in, rpb, conv_w, g_attn_out, g_conv_out, w_out,
           g_ffn, w_query, sub_keys1, sub_keys2, expert_u, expert_v, g_ple, w_ple_gate, w_ple_proj, g_final):
    x = jnp.concatenate([x_prompt.reshape(SEQ, D_MODEL), x_sample.reshape(DEC_BATCH * DEC_SEQ, D_MODEL)], axis=0)
    p = jnp.concatenate([p_prompt.reshape(DEPTH, SEQ, PLE_DIM),
                         p_sample.reshape(DEPTH, DEC_BATCH * DEC_SEQ, PLE_DIM)], axis=1)
    group = np.arange(ATTN_W) // HEAD_DIM
    gmat = jnp.asarray(group[:, None] == group[None, :], dtype=BF16)
    row = lambda g: g.reshape(1, -1)
    for i in range(DEPTH):
        q, k, v, gb, gc, gu = _proj(x, row(g_mix[i]), w_in[i].astype(BF16))
        x = _mixer(x, q, k, v, gb, gc, gu, _bias_table(rpb[i]), conv_w[i], row(g_attn_out[i]),
                   row(g_conv_out[i]), gmat, w_out[i].astype(BF16))
        h, a, b, gate = _route(x, row(g_ffn[i]), w_query[i].astype(BF16),
                               sub_keys1[i].astype(BF16), sub_keys2[i].astype(BF16))
        x = _peer(h, x, a, b, gate, expert_u[i].astype(BF16), expert_v[i].astype(BF16))
        x = _ple(x, p[i], row(g_ple[i]), w_ple_gate[i].astype(BF16), w_ple_proj[i].astype(BF16),
                 row(g_final), final=(i == DEPTH - 1))
    y_prompt = x[:SEQ].reshape(1, SEQ, D_MODEL)
    y_sample = x[SEQ:].reshape(DEC_BATCH, DEC_SEQ, D_MODEL)
    return (y_prompt, y_sample)
```

```python
import functools

import numpy as np
import jax
import jax.numpy as jnp
from jax import lax
from jax.experimental import pallas as pl
from jax.experimental.pallas import tpu as pltpu

F32 = jnp.float32
BF16 = jnp.bfloat16

D_MODEL = 1024
DEPTH = 2
SEQ = 16384
DEC_BATCH = 4
DEC_SEQ = 4096
N_TOK = SEQ + DEC_BATCH * DEC_SEQ

GRID_W = 64
WIN_ROWS = 8
WIN_COLS = 16
ATTN_HEADS = 8
HEAD_DIM = 64
ATTN_W = ATTN_HEADS * HEAD_DIM
CONV_W = D_MODEL - ATTN_W
N_KEYS = 128
N_EXPERTS = N_KEYS * N_KEYS
PEER_HEADS = 8
D_KEY = 256
PEER_TOPK = 16
PLE_DIM = 256
EPS = 1e-6
NEG_BIG = -1e30

LANES = 128
ROW_BLOCK = WIN_ROWS
TOK_BLOCK = ROW_BLOCK * GRID_W
N_BLOCKS = N_TOK // TOK_BLOCK
PROMPT_ROWS = SEQ // GRID_W
SAMPLE_ROWS = DEC_SEQ // GRID_W
PROMPT_BLOCKS = PROMPT_ROWS // ROW_BLOCK
SAMPLE_BLOCKS = SAMPLE_ROWS // ROW_BLOCK
HALO = 8

ROUTE_TOK = 256
PEER_TOK = 256
PEER_CHUNK_KEYS = 16
PEER_CHUNK = PEER_CHUNK_KEYS * N_KEYS
PEER_KEY_GROUPS = 1
PEER_GROUP_KEYS = N_KEYS // PEER_KEY_GROUPS
PEER_GROUP_CHUNKS = PEER_GROUP_KEYS // PEER_CHUNK_KEYS

VMEM_LIMIT = 48 * 1024 * 1024


def _rms(x, g):
    return x * lax.rsqrt(jnp.mean(x * x, axis=-1, keepdims=True) + EPS) * g


def _dot_nt(a, b):
    return lax.dot_general(a, b, (((1,), (1,)), ((), ())), preferred_element_type=F32)


def _proj_kernel(x_ref, g_ref, w_ref, q_ref, k_ref, v_ref, gb_ref, gc_ref, gu_ref):
    hb = _rms(x_ref[...], g_ref[...]).astype(BF16)
    outs = (q_ref, k_ref, v_ref, gb_ref, gc_ref, gu_ref)
    for j, o_ref in enumerate(outs):
        z = jnp.dot(hb, w_ref[:, j * ATTN_W:(j + 1) * ATTN_W], preferred_element_type=F32)
        if j == 0:
            z = z * (HEAD_DIM ** -0.5)
        o_ref[...] = z.astype(o_ref.dtype)


def _proj(x, g, w_in):
    tok = pl.BlockSpec((TOK_BLOCK, ATTN_W), lambda i: (i, 0))
    return pl.pallas_call(
        _proj_kernel,
        grid=(N_BLOCKS,),
        in_specs=[pl.BlockSpec((TOK_BLOCK, D_MODEL), lambda i: (i, 0)),
                  pl.BlockSpec((1, D_MODEL), lambda i: (0, 0)),
                  pl.BlockSpec((D_MODEL, 6 * ATTN_W), lambda i: (0, 0))],
        out_specs=[tok] * 6,
        out_shape=[jax.ShapeDtypeStruct((N_TOK, ATTN_W), BF16)] * 3
                  + [jax.ShapeDtypeStruct((N_TOK, ATTN_W), F32)] * 3,
        compiler_params=pltpu.CompilerParams(dimension_semantics=("arbitrary",),
                                             vmem_limit_bytes=VMEM_LIMIT),
        name="proj",
    )(x, g, w_in)


def _group_norm(xv, gmat, g):
    sq = xv * xv
    hi = sq.astype(BF16)
    lo = (sq - hi.astype(F32)).astype(BF16)
    ms = (jnp.dot(hi, gmat, preferred_element_type=F32)
          + jnp.dot(lo, gmat, preferred_element_type=F32)) * (1.0 / HEAD_DIM)
    return xv * lax.rsqrt(ms + EPS) * g


def _mixer_kernel(x_ref, q_ref, kp_ref, kc_ref, kn_ref, vp_ref, vc_ref, vn_ref,
                  gb_ref, gc_ref, gu_ref, gcp_ref, gup_ref, gcn_ref, gun_ref,
                  bias_ref, cw_ref, ga_ref, gcv_ref, gmat_ref, wo_ref,
                  o_ref, kbuf, vbuf, abuf):
    i = pl.program_id(0)
    is_prompt = i < PROMPT_BLOCKS
    sample = (i - PROMPT_BLOCKS) // SAMPLE_BLOCKS
    seq_r0 = jnp.where(is_prompt, 0, PROMPT_ROWS + SAMPLE_ROWS * sample)
    seq_r1 = jnp.where(is_prompt, PROMPT_ROWS, seq_r0 + SAMPLE_ROWS)
    blk_r0 = ROW_BLOCK * i

    kbuf[0:TOK_BLOCK, :] = kp_ref[...]
    kbuf[TOK_BLOCK:2 * TOK_BLOCK, :] = kc_ref[...]
    kbuf[2 * TOK_BLOCK:3 * TOK_BLOCK, :] = kn_ref[...]
    vbuf[0:TOK_BLOCK, :] = vp_ref[...]
    vbuf[TOK_BLOCK:2 * TOK_BLOCK, :] = vc_ref[...]
    vbuf[2 * TOK_BLOCK:3 * TOK_BLOCK, :] = vn_ref[...]

    lane = lax.broadcasted_iota(jnp.int32, (GRID_W, LANES), 1)
    low_half = lane < HEAD_DIM
    n_win = WIN_ROWS * GRID_W
    for jr in range(ROW_BLOCK):
        r = blk_r0 + jr
        rs = jnp.clip(r - WIN_ROWS // 2, seq_r0, seq_r1 - WIN_ROWS)
        delta = r - rs
        start = pl.multiple_of((rs - blk_r0 + ROW_BLOCK) * GRID_W, GRID_W)
        pairs = range(ATTN_HEADS // 2)
        col = [slice(p * LANES, (p + 1) * LANES) for p in pairs]
        scores = []
        for p in pairs:
            qp = q_ref[jr * GRID_W:(jr + 1) * GRID_W, col[p]]
            zero = jnp.zeros_like(qp)
            q2 = jnp.concatenate([jnp.where(low_half, qp, zero), jnp.where(low_half, zero, qp)], axis=0)
            scores.append(_dot_nt(q2, kbuf[pl.ds(start, n_win), col[p]]) + bias_ref[delta, p])
        probs, norms = [], []
        for s in scores:
            e = jnp.exp(s - jnp.max(s, axis=-1, keepdims=True))
            probs.append(e.astype(BF16))
            norms.append(jnp.sum(e, axis=-1, keepdims=True))
        for p in pairs:
            o2 = jnp.dot(probs[p], vbuf[pl.ds(start, n_win), col[p]], preferred_element_type=F32) / norms[p]
            abuf[jr * GRID_W:(jr + 1) * GRID_W, col[p]] = jnp.where(low_half, o2[:GRID_W], o2[GRID_W:])

    first = blk_r0 == seq_r0
    last = blk_r0 + ROW_BLOCK == seq_r1
    cu = gc_ref[...] * gu_ref[...]
    prev_row = jnp.where(first, 0.0, gcp_ref[HALO - 1:HALO, :] * gup_ref[HALO - 1:HALO, :])
    next_row = jnp.where(last, 0.0, gcn_ref[0:1, :] * gun_ref[0:1, :])
    row = lax.broadcasted_iota(jnp.int32, (TOK_BLOCK, CONV_W), 0)
    up_prev = jnp.where(row == 0, prev_row, pltpu.roll(cu, 1, axis=0))
    up_next = jnp.where(row == TOK_BLOCK - 1, next_row, pltpu.roll(cu, TOK_BLOCK - 1, axis=0))
    conv = gb_ref[...] * (up_prev * cw_ref[0:1, :] + cu * cw_ref[1:2, :] + up_next * cw_ref[2:3, :])

    gmat = gmat_ref[...]
    attn_n = _group_norm(abuf[...], gmat, ga_ref[...]).astype(BF16)
    conv_n = _group_norm(conv, gmat, gcv_ref[...]).astype(BF16)
    y = (jnp.dot(attn_n, wo_ref[0:ATTN_W, :], preferred_element_type=F32)
         + jnp.dot(conv_n, wo_ref[ATTN_W:D_MODEL, :], preferred_element_type=F32))
    o_ref[...] = x_ref[...] + y


def _mixer(x, q, k, v, gb, gc, gu, bias, conv_w, g_attn, g_conv, gmat, w_out):
    cur = lambda i: (i, 0)
    prev = lambda i: (jnp.maximum(i - 1, 0), 0)
    nxt = lambda i: (jnp.minimum(i + 1, N_BLOCKS - 1), 0)
    halo_per_block = TOK_BLOCK // HALO
    hprev = lambda i: (jnp.maximum(i * halo_per_block - 1, 0), 0)
    hnext = lambda i: (jnp.minimum((i + 1) * halo_per_block, N_TOK // HALO - 1), 0)
    const2 = lambda i: (0, 0)
    blk = lambda m: pl.BlockSpec((TOK_BLOCK, ATTN_W), m)
    halo = lambda m: pl.BlockSpec((HALO, CONV_W), m)
    return pl.pallas_call(
        _mixer_kernel,
        grid=(N_BLOCKS,),
        in_specs=[pl.BlockSpec((TOK_BLOCK, D_MODEL), cur),
                  blk(cur), blk(prev), blk(cur), blk(nxt), blk(prev), blk(cur), blk(nxt),
                  blk(cur), blk(cur), blk(cur), halo(hprev), halo(hprev), halo(hnext), halo(hnext),
                  pl.BlockSpec((WIN_ROWS, ATTN_HEADS // 2, 2 * GRID_W, WIN_ROWS * GRID_W), lambda i: (0, 0, 0, 0)),
                  pl.BlockSpec((3, CONV_W), const2),
                  pl.BlockSpec((1, ATTN_W), const2),
                  pl.BlockSpec((1, CONV_W), const2),
                  pl.BlockSpec((ATTN_W, ATTN_W), const2),
                  pl.BlockSpec((D_MODEL, D_MODEL), const2)],
        out_specs=pl.BlockSpec((TOK_BLOCK, D_MODEL), cur),
        out_shape=jax.ShapeDtypeStruct((N_TOK, D_MODEL), F32),
        scratch_shapes=[pltpu.VMEM((3 * TOK_BLOCK, ATTN_W), BF16),
                        pltpu.VMEM((3 * TOK_BLOCK, ATTN_W), BF16),
                        pltpu.VMEM((TOK_BLOCK, ATTN_W), F32)],
        compiler_params=pltpu.CompilerParams(dimension_semantics=("arbitrary",),
                                             vmem_limit_bytes=VMEM_LIMIT),
        name="mixer",
    )(x, q, k, k, k, v, v, v, gb, gc, gu, gc, gu, gc, gu, bias, conv_w, g_attn, g_conv, gmat, w_out)


def _record(k, kidx, m, pick, vals, picks):
    sel = kidx == k
    return jnp.where(sel, m, vals), jnp.where(sel, pick, picks)


def _top_keys(s):
    n, t = s.shape
    half = n // 2
    r0 = lax.broadcasted_iota(jnp.int32, (half, t), 0).astype(F32)
    r1 = r0 + float(half)
    first = s[:half] >= s[half:]
    hi = jnp.where(first, s[:half], s[half:])
    lo = jnp.where(first, s[half:], s[:half])
    ihi = jnp.where(first, r0, r1)
    ilo = jnp.where(first, r1, r0)
    kidx = lax.broadcasted_iota(jnp.int32, (PEER_TOPK, t), 0)

    def body(k, carry):
        hi, lo, ihi, vals, picks = carry
        m = jnp.max(hi, axis=0, keepdims=True)
        pos = jnp.min(jnp.where(hi == m, ihi, float(n)), axis=0, keepdims=True)
        onehot = ihi == pos
        vals, picks = _record(k, kidx, m, pos, vals, picks)
        return (jnp.where(onehot, lo, hi), jnp.where(onehot, -jnp.inf, lo), jnp.where(onehot, ilo, ihi),
                vals, picks)

    zeros = jnp.zeros((PEER_TOPK, t), F32)
    out = lax.fori_loop(0, PEER_TOPK, body, (hi, lo, ihi, zeros, zeros), unroll=True)
    return out[3], out[4]


_CAND_PER_KA = [PEER_TOPK // (ka + 1) for ka in range(PEER_TOPK)]
_N_CAND = sum(_CAND_PER_KA)
_CAND_PAD = -_N_CAND % 8


def _top_candidates(v1, i1, v2, i2):
    t = v1.shape[1]
    s_rows, e_rows = [], []
    for ka, n_kb in enumerate(_CAND_PER_KA):
        s_rows.append(v1[ka:ka + 1, :] + v2[:n_kb, :])
        e_rows.append(i1[ka:ka + 1, :] * float(N_KEYS) + i2[:n_kb, :])
    s_rows.append(jnp.full((_CAND_PAD, t), -jnp.inf, F32))
    e_rows.append(jnp.zeros((_CAND_PAD, t), F32))
    s = jnp.concatenate(s_rows, axis=0)
    e = jnp.concatenate(e_rows, axis=0)
    rows = s.shape[0]
    ridx = lax.broadcasted_iota(jnp.int32, (rows, t), 0).astype(F32)
    kidx = lax.broadcasted_iota(jnp.int32, (PEER_TOPK, t), 0)

    def body(k, carry):
        s, vals, picks = carry
        m = jnp.max(s, axis=0, keepdims=True)
        pos = jnp.min(jnp.where(s == m, ridx, float(rows)), axis=0, keepdims=True)
        onehot = ridx == pos
        pick = jnp.max(jnp.where(onehot, e, -1.0), axis=0, keepdims=True)
        vals, picks = _record(k, kidx, m, pick, vals, picks)
        return jnp.where(onehot, -jnp.inf, s), vals, picks

    zeros = jnp.zeros((PEER_TOPK, t), F32)
    _, vals, picks = lax.fori_loop(0, PEER_TOPK, body, (s, zeros, zeros), unroll=True)
    return vals, picks


def _route_kernel(x_ref, g_ref, wq_ref, k1_ref, k2_ref, h_ref, a_ref, b_ref, gate_ref, gt_buf, et_buf):
    hb = _rms(x_ref[...], g_ref[...]).astype(BF16)
    h_ref[...] = hb
    half = D_KEY // 2
    for h in range(PEER_HEADS):
        qh = jnp.dot(hb, wq_ref[:, h * D_KEY:(h + 1) * D_KEY], preferred_element_type=F32).astype(BF16)
        s1 = _dot_nt(k1_ref[...], qh[:, :half])
        s2 = _dot_nt(k2_ref[...], qh[:, half:])
        v1, i1 = _top_keys(s1)
        v2, i2 = _top_keys(s2)
        top_s, top_e = _top_candidates(v1, i1, v2, i2)
        ex = jnp.exp(top_s - jnp.max(top_s, axis=0, keepdims=True))
        gt_buf[h * PEER_TOPK:(h + 1) * PEER_TOPK, :] = ex / jnp.sum(ex, axis=0, keepdims=True)
        et_buf[h * PEER_TOPK:(h + 1) * PEER_TOPK, :] = top_e
    e = et_buf[...].T
    a = jnp.floor(e * (1.0 / N_KEYS))
    a_ref[...] = a
    b_ref[...] = e - a * float(N_KEYS)
    gate_ref[...] = gt_buf[...].T


def _route(x, g, w_query, keys1, keys2):
    n_blk = N_TOK // ROUTE_TOK
    hk = PEER_HEADS * PEER_TOPK
    tok = lambda w: pl.BlockSpec((ROUTE_TOK, w), lambda i: (i, 0))
    const2 = lambda i: (0, 0)
    return pl.pallas_call(
        _route_kernel,
        grid=(n_blk,),
        in_specs=[tok(D_MODEL),
                  pl.BlockSpec((1, D_MODEL), const2),
                  pl.BlockSpec((D_MODEL, PEER_HEADS * D_KEY), const2),
                  pl.BlockSpec((N_KEYS, D_KEY // 2), const2),
                  pl.BlockSpec((N_KEYS, D_KEY // 2), const2)],
        out_specs=[tok(D_MODEL), tok(hk), tok(hk), tok(hk)],
        out_shape=[jax.ShapeDtypeStruct((N_TOK, D_MODEL), BF16)]
                  + [jax.ShapeDtypeStruct((N_TOK, hk), F32)] * 3,
        scratch_shapes=[pltpu.VMEM((hk, ROUTE_TOK), F32), pltpu.VMEM((hk, ROUTE_TOK), F32)],
        compiler_params=pltpu.CompilerParams(dimension_semantics=("arbitrary",),
                                             vmem_limit_bytes=VMEM_LIMIT),
        name="route",
    )(x, g, w_query, keys1, keys2)


def _gelu_exact(x):
    return 0.5 * x * (1.0 + lax.erf(x * (2.0 ** -0.5)))


def _peer_kernel(h_ref, x_ref, a_ref, b_ref, gate_ref, u_ref, v_ref, o_ref, wtok, acc):
    grp = pl.program_id(1)
    c = pl.program_id(2)

    @pl.when(jnp.logical_and(grp == 0, c == 0))
    def _():
        acc[...] = jnp.zeros_like(acc)

    @pl.when(c == 0)
    def _():
        key = (lax.broadcasted_iota(jnp.int32, (PEER_GROUP_KEYS, LANES), 0) + grp * PEER_GROUP_KEYS).astype(F32)
        sub = lax.broadcasted_iota(jnp.int32, (N_KEYS, LANES), 0).astype(F32)

        def tok(t, carry):
            ar = a_ref[pl.ds(t, 1), :]
            br = b_ref[pl.ds(t, 1), :]
            gr = gate_ref[pl.ds(t, 1), :]
            pg = jnp.where(key == ar, gr, 0.0).astype(BF16)
            qb = jnp.where(sub == br, 1.0, 0.0).astype(BF16)
            wtok[t] = _dot_nt(pg, qb)
            return carry

        lax.fori_loop(0, PEER_TOK, tok, 0, unroll=32)

    a_act = _dot_nt(h_ref[...], u_ref[...])
    wk = wtok[:, pl.ds(pl.multiple_of(c * PEER_CHUNK_KEYS, PEER_CHUNK_KEYS), PEER_CHUNK_KEYS), :]
    wk = jnp.swapaxes(wk, 0, 1)
    w = jnp.concatenate([wk[j] for j in range(PEER_CHUNK_KEYS)], axis=1)
    wp = (w * _gelu_exact(a_act)).astype(BF16)
    acc[...] += jnp.dot(wp, v_ref[...], preferred_element_type=F32)

    @pl.when(jnp.logical_and(grp == PEER_KEY_GROUPS - 1, c == PEER_GROUP_CHUNKS - 1))
    def _():
        o_ref[...] = x_ref[...] + acc[...]


def _peer(h, x, a, b, gate, eu, ev):
    n_blk = N_TOK // PEER_TOK
    hk = PEER_HEADS * PEER_TOPK
    tok = lambda w: pl.BlockSpec((PEER_TOK, w), lambda i, g, c: (i, 0))
    chunk = pl.BlockSpec((PEER_CHUNK, D_MODEL), lambda i, g, c: (g * PEER_GROUP_CHUNKS + c, 0))
    return pl.pallas_call(
        _peer_kernel,
        grid=(n_blk, PEER_KEY_GROUPS, PEER_GROUP_CHUNKS),
        in_specs=[tok(D_MODEL), tok(D_MODEL), tok(hk), tok(hk), tok(hk), chunk, chunk],
        out_specs=tok(D_MODEL),
        out_shape=jax.ShapeDtypeStruct((N_TOK, D_MODEL), F32),
        scratch_shapes=[pltpu.VMEM((PEER_TOK, PEER_GROUP_KEYS, N_KEYS), F32),
                        pltpu.VMEM((PEER_TOK, D_MODEL), F32)],
        compiler_params=pltpu.CompilerParams(dimension_semantics=("arbitrary", "arbitrary", "arbitrary"),
                                             vmem_limit_bytes=VMEM_LIMIT),
        name="peer",
    )(h, x, a, b, gate, eu, ev)


def _ple_kernel(x_ref, p_ref, g_ref, wg_ref, wp_ref, gf_ref, o_ref, *, final):
    x = x_ref[...]
    hp = _rms(x, g_ref[...]).astype(BF16)
    gate = jax.nn.sigmoid(jnp.dot(hp, wg_ref[...], preferred_element_type=F32))
    proj = jnp.dot(p_ref[...].astype(BF16), wp_ref[...], preferred_element_type=F32)
    y = x + gate * proj
    if final:
        y = _rms(y, gf_ref[...])
    o_ref[...] = y


def _ple(x, p, g, w_gate, w_proj, g_final, final):
    const2 = lambda i: (0, 0)
    return pl.pallas_call(
        functools.partial(_ple_kernel, final=final),
        grid=(N_BLOCKS,),
        in_specs=[pl.BlockSpec((TOK_BLOCK, D_MODEL), lambda i: (i, 0)),
                  pl.BlockSpec((TOK_BLOCK, PLE_DIM), lambda i: (i, 0)),
                  pl.BlockSpec((1, D_MODEL), const2),
                  pl.BlockSpec((D_MODEL, D_MODEL), const2),
                  pl.BlockSpec((PLE_DIM, D_MODEL), const2),
                  pl.BlockSpec((1, D_MODEL), const2)],
        out_specs=pl.BlockSpec((TOK_BLOCK, D_MODEL), lambda i: (i, 0)),
        out_shape=jax.ShapeDtypeStruct((N_TOK, D_MODEL), F32),
        compiler_params=pltpu.CompilerParams(dimension_semantics=("arbitrary",),
                                             vmem_limit_bytes=VMEM_LIMIT),
        name="ple",
    )(x, p, g, w_gate, w_proj, g_final)


def _bias_table(rpb):
    cols = np.arange(GRID_W)
    col_start = np.clip(cols - WIN_COLS // 2, 0, GRID_W - WIN_COLS)
    kc = np.arange(GRID_W)
    in_win = (kc[None, :] >= col_start[:, None]) & (kc[None, :] < col_start[:, None] + WIN_COLS)
    col_off = np.clip(kc[None, :] - cols[:, None] + (WIN_COLS - 1), 0, 2 * WIN_COLS - 2)
    delta = np.arange(WIN_ROWS)
    j = np.arange(WIN_ROWS)
    row_off = j[None, :] - delta[:, None] + (WIN_ROWS - 1)
    t = rpb[:, row_off][:, :, :, col_off]
    t = jnp.where(in_win[None, None, None], t, NEG_BIG)
    t = t.transpose(1, 0, 3, 2, 4)
    return t.reshape(WIN_ROWS, ATTN_HEADS // 2, 2 * GRID_W, WIN_ROWS * GRID_W).astype(F32)


def kernel(x_prompt, x_sample, p_prompt, p_sample, g_mix, w_in, rpb, conv_w, g_attn_out, g_conv_out, w_out,
           g_ffn, w_query, sub_keys1, sub_keys2, expert_u, expert_v, g_ple, w_ple_gate, w_ple_proj, g_final):
    x = jnp.concatenate([x_prompt.reshape(SEQ, D_MODEL), x_sample.reshape(DEC_BATCH * DEC_SEQ, D_MODEL)], axis=0)
    p = jnp.concatenate([p_prompt.reshape(DEPTH, SEQ, PLE_DIM),
                         p_sample.reshape(DEPTH, DEC_BATCH * DEC_SEQ, PLE_DIM)], axis=1)
    group = np.arange(ATTN_W) // HEAD_DIM
    gmat = jnp.asarray(group[:, None] == group[None, :], dtype=BF16)
    row = lambda g: g.reshape(1, -1)
    for i in range(DEPTH):
        q, k, v, gb, gc, gu = _proj(x, row(g_mix[i]), w_in[i].astype(BF16))
        x = _mixer(x, q, k, v, gb, gc, gu, _bias_table(rpb[i]), conv_w[i], row(g_attn_out[i]),
                   row(g_conv_out[i]), gmat, w_out[i].astype(BF16))
        h, a, b, gate = _route(x, row(g_ffn[i]), w_query[i].astype(BF16),
                               sub_keys1[i].astype(BF16), sub_keys2[i].astype(BF16))
        x = _peer(h, x, a, b, gate, expert_u[i].astype(BF16), expert_v[i].astype(BF16))
        x = _ple(x, p[i], row(g_ple[i]), w_ple_gate[i].astype(BF16), w_ple_proj[i].astype(BF16),
                 row(g_final), final=(i == DEPTH - 1))
    y_prompt = x[:SEQ].reshape(1, SEQ, D_MODEL)
    y_sample = x[SEQ:].reshape(DEC_BATCH, DEC_SEQ, D_MODEL)
    return (y_prompt, y_sample)
```

```python
import functools

import numpy as np
import jax
import jax.numpy as jnp
from jax import lax
from jax.experimental import pallas as pl
from jax.experimental.pallas import tpu as pltpu

F32 = jnp.float32
BF16 = jnp.bfloat16

D_MODEL = 1024
DEPTH = 2
SEQ = 16384
DEC_BATCH = 4
DEC_SEQ = 4096
N_TOK = SEQ + DEC_BATCH * DEC_SEQ

GRID_W = 64
WIN_ROWS = 8
WIN_COLS = 16
ATTN_HEADS = 8
HEAD_DIM = 64
ATTN_W = ATTN_HEADS * HEAD_DIM
CONV_W = D_MODEL - ATTN_W
N_KEYS = 128
N_EXPERTS = N_KEYS * N_KEYS
PEER_HEADS = 8
D_KEY = 256
PEER_TOPK = 16
PLE_DIM = 256
EPS = 1e-6
NEG_BIG = -1e30

LANES = 128
ROW_BLOCK = WIN_ROWS
TOK_BLOCK = ROW_BLOCK * GRID_W
N_BLOCKS = N_TOK // TOK_BLOCK
PROMPT_ROWS = SEQ // GRID_W
SAMPLE_ROWS = DEC_SEQ // GRID_W
PROMPT_BLOCKS = PROMPT_ROWS // ROW_BLOCK
SAMPLE_BLOCKS = SAMPLE_ROWS // ROW_BLOCK
HALO = 8

PEER_TOK = 256
PEER_CHUNK_KEYS = 16
PEER_CHUNK = PEER_CHUNK_KEYS * N_KEYS
PEER_N_CHUNKS = N_EXPERTS // PEER_CHUNK
assert PEER_N_CHUNKS == PEER_HEADS

VMEM_LIMIT = 48 * 1024 * 1024
PEER_VMEM_LIMIT = 56 * 1024 * 1024


def _rms(x, g):
    return x * lax.rsqrt(jnp.mean(x * x, axis=-1, keepdims=True) + EPS) * g


def _dot_nt(a, b):
    return lax.dot_general(a, b, (((1,), (1,)), ((), ())), preferred_element_type=F32)


def _proj_kernel(x_ref, g_ref, w_ref, q_ref, k_ref, v_ref, gb_ref, gc_ref, gu_ref):
    hb = _rms(x_ref[...], g_ref[...]).astype(BF16)
    outs = (q_ref, k_ref, v_ref, gb_ref, gc_ref, gu_ref)
    for j, o_ref in enumerate(outs):
        z = jnp.dot(hb, w_ref[:, j * ATTN_W:(j + 1) * ATTN_W], preferred_element_type=F32)
        if j == 0:
            z = z * (HEAD_DIM ** -0.5)
        o_ref[...] = z.astype(o_ref.dtype)


def _proj(x, g, w_in):
    tok = pl.BlockSpec((TOK_BLOCK, ATTN_W), lambda i: (i, 0))
    return pl.pallas_call(
        _proj_kernel,
        grid=(N_BLOCKS,),
        in_specs=[pl.BlockSpec((TOK_BLOCK, D_MODEL), lambda i: (i, 0)),
                  pl.BlockSpec((1, D_MODEL), lambda i: (0, 0)),
                  pl.BlockSpec((D_MODEL, 6 * ATTN_W), lambda i: (0, 0))],
        out_specs=[tok] * 6,
        out_shape=[jax.ShapeDtypeStruct((N_TOK, ATTN_W), BF16)] * 3
                  + [jax.ShapeDtypeStruct((N_TOK, ATTN_W), F32)] * 3,
        compiler_params=pltpu.CompilerParams(dimension_semantics=("arbitrary",),
                                             vmem_limit_bytes=VMEM_LIMIT),
        name="proj",
    )(x, g, w_in)


def _group_norm(xv, gmat, g):
    sq = xv * xv
    hi = sq.astype(BF16)
    lo = (sq - hi.astype(F32)).astype(BF16)
    ms = (jnp.dot(hi, gmat, preferred_element_type=F32)
          + jnp.dot(lo, gmat, preferred_element_type=F32)) * (1.0 / HEAD_DIM)
    return xv * lax.rsqrt(ms + EPS) * g


def _mixer_kernel(x_ref, q_ref, kp_ref, kc_ref, kn_ref, vp_ref, vc_ref, vn_ref,
                  gb_ref, gc_ref, gu_ref, gcp_ref, gup_ref, gcn_ref, gun_ref,
                  bias_ref, cw_ref, ga_ref, gcv_ref, gmat_ref, wo_ref,
                  o_ref, kbuf, vbuf, abuf):
    i = pl.program_id(0)
    is_prompt = i < PROMPT_BLOCKS
    sample = (i - PROMPT_BLOCKS) // SAMPLE_BLOCKS
    seq_r0 = jnp.where(is_prompt, 0, PROMPT_ROWS + SAMPLE_ROWS * sample)
    seq_r1 = jnp.where(is_prompt, PROMPT_ROWS, seq_r0 + SAMPLE_ROWS)
    blk_r0 = ROW_BLOCK * i

    kbuf[0:TOK_BLOCK, :] = kp_ref[...]
    kbuf[TOK_BLOCK:2 * TOK_BLOCK, :] = kc_ref[...]
    kbuf[2 * TOK_BLOCK:3 * TOK_BLOCK, :] = kn_ref[...]
    vbuf[0:TOK_BLOCK, :] = vp_ref[...]
    vbuf[TOK_BLOCK:2 * TOK_BLOCK, :] = vc_ref[...]
    vbuf[2 * TOK_BLOCK:3 * TOK_BLOCK, :] = vn_ref[...]

    lane = lax.broadcasted_iota(jnp.int32, (GRID_W, LANES), 1)
    low_half = lane < HEAD_DIM
    n_win = WIN_ROWS * GRID_W
    for jr in range(ROW_BLOCK):
        r = blk_r0 + jr
        rs = jnp.clip(r - WIN_ROWS // 2, seq_r0, seq_r1 - WIN_ROWS)
        delta = r - rs
        start = pl.multiple_of((rs - blk_r0 + ROW_BLOCK) * GRID_W, GRID_W)
        pairs = range(ATTN_HEADS // 2)
        col = [slice(p * LANES, (p + 1) * LANES) for p in pairs]
        scores = []
        for p in pairs:
            qp = q_ref[jr * GRID_W:(jr + 1) * GRID_W, col[p]]
            zero = jnp.zeros_like(qp)
            q2 = jnp.concatenate([jnp.where(low_half, qp, zero), jnp.where(low_half, zero, qp)], axis=0)
            scores.append(_dot_nt(q2, kbuf[pl.ds(start, n_win), col[p]]) + bias_ref[delta, p])
        probs, norms = [], []
        for s in scores:
            e = jnp.exp(s - jnp.max(s, axis=-1, keepdims=True))
            probs.append(e.astype(BF16))
            norms.append(jnp.sum(e, axis=-1, keepdims=True))
        for p in pairs:
            o2 = jnp.dot(probs[p], vbuf[pl.ds(start, n_win), col[p]], preferred_element_type=F32) / norms[p]
            abuf[jr * GRID_W:(jr + 1) * GRID_W, col[p]] = jnp.where(low_half, o2[:GRID_W], o2[GRID_W:])

    first = blk_r0 == seq_r0
    last = blk_r0 + ROW_BLOCK == seq_r1
    cu = gc_ref[...] * gu_ref[...]
    prev_row = jnp.where(first, 0.0, gcp_ref[HALO - 1:HALO, :] * gup_ref[HALO - 1:HALO, :])
    next_row = jnp.where(last, 0.0, gcn_ref[0:1, :] * gun_ref[0:1, :])
    row = lax.broadcasted_iota(jnp.int32, (TOK_BLOCK, CONV_W), 0)
    up_prev = jnp.where(row == 0, prev_row, pltpu.roll(cu, 1, axis=0))
    up_next = jnp.where(row == TOK_BLOCK - 1, next_row, pltpu.roll(cu, TOK_BLOCK - 1, axis=0))
    conv = gb_ref[...] * (up_prev * cw_ref[0:1, :] + cu * cw_ref[1:2, :] + up_next * cw_ref[2:3, :])

    gmat = gmat_ref[...]
    attn_n = _group_norm(abuf[...], gmat, ga_ref[...]).astype(BF16)
    conv_n = _group_norm(conv, gmat, gcv_ref[...]).astype(BF16)
    y = (jnp.dot(attn_n, wo_ref[0:ATTN_W, :], preferred_element_type=F32)
         + jnp.dot(conv_n, wo_ref[ATTN_W:D_MODEL, :], preferred_element_type=F32))
    o_ref[...] = x_ref[...] + y


def _mixer(x, q, k, v, gb, gc, gu, bias, conv_w, g_attn, g_conv, gmat, w_out):
    cur = lambda i: (i, 0)
    prev = lambda i: (jnp.maximum(i - 1, 0), 0)
    nxt = lambda i: (jnp.minimum(i + 1, N_BLOCKS - 1), 0)
    halo_per_block = TOK_BLOCK // HALO
    hprev = lambda i: (jnp.maximum(i * halo_per_block - 1, 0), 0)
    hnext = lambda i: (jnp.minimum((i + 1) * halo_per_block, N_TOK // HALO - 1), 0)
    const2 = lambda i: (0, 0)
    blk = lambda m: pl.BlockSpec((TOK_BLOCK, ATTN_W), m)
    halo = lambda m: pl.BlockSpec((HALO, CONV_W), m)
    return pl.pallas_call(
        _mixer_kernel,
        grid=(N_BLOCKS,),
        in_specs=[pl.BlockSpec((TOK_BLOCK, D_MODEL), cur),
                  blk(cur), blk(prev), blk(cur), blk(nxt), blk(prev), blk(cur), blk(nxt),
                  blk(cur), blk(cur), blk(cur), halo(hprev), halo(hprev), halo(hnext), halo(hnext),
                  pl.BlockSpec((WIN_ROWS, ATTN_HEADS // 2, 2 * GRID_W, WIN_ROWS * GRID_W), lambda i: (0, 0, 0, 0)),
                  pl.BlockSpec((3, CONV_W), const2),
                  pl.BlockSpec((1, ATTN_W), const2),
                  pl.BlockSpec((1, CONV_W), const2),
                  pl.BlockSpec((ATTN_W, ATTN_W), const2),
                  pl.BlockSpec((D_MODEL, D_MODEL), const2)],
        out_specs=pl.BlockSpec((TOK_BLOCK, D_MODEL), cur),
        out_shape=jax.ShapeDtypeStruct((N_TOK, D_MODEL), F32),
        scratch_shapes=[pltpu.VMEM((3 * TOK_BLOCK, ATTN_W), BF16),
                        pltpu.VMEM((3 * TOK_BLOCK, ATTN_W), BF16),
                        pltpu.VMEM((TOK_BLOCK, ATTN_W), F32)],
        compiler_params=pltpu.CompilerParams(dimension_semantics=("arbitrary",),
                                             vmem_limit_bytes=VMEM_LIMIT),
        name="mixer",
    )(x, q, k, k, k, v, v, v, gb, gc, gu, gc, gu, gc, gu, bias, conv_w, g_attn, g_conv, gmat, w_out)


def _record(k, kidx, m, pick, vals, picks):
    sel = kidx == k
    return jnp.where(sel, m, vals), jnp.where(sel, pick, picks)


def _top_keys(s):
    n, t = s.shape
    half = n // 2
    r0 = lax.broadcasted_iota(jnp.int32, (half, t), 0).astype(F32)
    r1 = r0 + float(half)
    first = s[:half] >= s[half:]
    hi = jnp.where(first, s[:half], s[half:])
    lo = jnp.where(first, s[half:], s[:half])
    ihi = jnp.where(first, r0, r1)
    ilo = jnp.where(first, r1, r0)
    kidx = lax.broadcasted_iota(jnp.int32, (PEER_TOPK, t), 0)

    def body(k, carry):
        hi, lo, ihi, vals, picks = carry
        m = jnp.max(hi, axis=0, keepdims=True)
        pos = jnp.min(jnp.where(hi == m, ihi, float(n)), axis=0, keepdims=True)
        onehot = ihi == pos
        vals, picks = _record(k, kidx, m, pos, vals, picks)
        return (jnp.where(onehot, lo, hi), jnp.where(onehot, -jnp.inf, lo), jnp.where(onehot, ilo, ihi),
                vals, picks)

    zeros = jnp.zeros((PEER_TOPK, t), F32)
    out = lax.fori_loop(0, PEER_TOPK, body, (hi, lo, ihi, zeros, zeros), unroll=True)
    return out[3], out[4]


_CAND_PER_KA = [PEER_TOPK // (ka + 1) for ka in range(PEER_TOPK)]
_N_CAND = sum(_CAND_PER_KA)
_CAND_PAD = -_N_CAND % 8


def _top_candidates(v1, i1, v2, i2):
    t = v1.shape[1]
    s_rows, e_rows = [], []
    for ka, n_kb in enumerate(_CAND_PER_KA):
        s_rows.append(v1[ka:ka + 1, :] + v2[:n_kb, :])
        e_rows.append(i1[ka:ka + 1, :] * float(N_KEYS) + i2[:n_kb, :])
    s_rows.append(jnp.full((_CAND_PAD, t), -jnp.inf, F32))
    e_rows.append(jnp.zeros((_CAND_PAD, t), F32))
    s = jnp.concatenate(s_rows, axis=0)
    e = jnp.concatenate(e_rows, axis=0)
    rows = s.shape[0]
    ridx = lax.broadcasted_iota(jnp.int32, (rows, t), 0).astype(F32)
    kidx = lax.broadcasted_iota(jnp.int32, (PEER_TOPK, t), 0)

    def body(k, carry):
        s, vals, picks = carry
        m = jnp.max(s, axis=0, keepdims=True)
        pos = jnp.min(jnp.where(s == m, ridx, float(rows)), axis=0, keepdims=True)
        onehot = ridx == pos
        pick = jnp.max(jnp.where(onehot, e, -1.0), axis=0, keepdims=True)
        vals, picks = _record(k, kidx, m, pick, vals, picks)
        return jnp.where(onehot, -jnp.inf, s), vals, picks

    zeros = jnp.zeros((PEER_TOPK, t), F32)
    _, vals, picks = lax.fori_loop(0, PEER_TOPK, body, (s, zeros, zeros), unroll=True)
    return vals, picks


def _route_head(hb, wq, k1, k2):
    half = D_KEY // 2
    qh = jnp.dot(hb, wq, preferred_element_type=F32).astype(BF16)
    s1 = _dot_nt(k1, qh[:, :half])
    s2 = _dot_nt(k2, qh[:, half:])
    v1, i1 = _top_keys(s1)
    v2, i2 = _top_keys(s2)
    top_s, top_e = _top_candidates(v1, i1, v2, i2)
    ex = jnp.exp(top_s - jnp.max(top_s, axis=0, keepdims=True))
    return ex / jnp.sum(ex, axis=0, keepdims=True), top_e


def _gelu_exact(x):
    return 0.5 * x * (1.0 + lax.erf(x * (2.0 ** -0.5)))


def _peer_kernel(xr_ref, xp_ref, g_ref, wq_ref, k1_ref, k2_ref, u_ref, v_ref, o_ref,
                 h_buf, gt_buf, et_buf, a_buf, b_buf, gate_buf, wtok, acc):
    i = pl.program_id(0)
    c = pl.program_id(1)
    route_slot = i % 2
    peer_slot = 1 - route_slot

    @pl.when(jnp.logical_and(i == 0, c == 0))
    def _():
        h_buf[1] = jnp.zeros((PEER_TOK, D_MODEL), BF16)
        gt_buf[1] = jnp.zeros((PEER_HEADS * PEER_TOPK, PEER_TOK), F32)
        et_buf[1] = jnp.zeros((PEER_HEADS * PEER_TOPK, PEER_TOK), F32)

    @pl.when(c == 0)
    def _():
        h_buf[route_slot] = _rms(xr_ref[...], g_ref[...]).astype(BF16)
        acc[...] = jnp.zeros_like(acc)
        e = et_buf[peer_slot].T
        a = jnp.floor(e * (1.0 / N_KEYS))
        a_buf[...] = a
        b_buf[...] = e - a * float(N_KEYS)
        gate_buf[...] = gt_buf[peer_slot].T
        sub = lax.broadcasted_iota(jnp.int32, (N_KEYS, LANES), 0).astype(F32)

        def tok(t, carry):
            ar = a_buf[pl.ds(t, 1), :]
            br = b_buf[pl.ds(t, 1), :]
            gr = gate_buf[pl.ds(t, 1), :]
            pg = jnp.where(sub == ar, gr, 0.0).astype(BF16)
            qb = jnp.where(sub == br, 1.0, 0.0).astype(BF16)
            wtok[t] = _dot_nt(pg, qb)
            return carry

        lax.fori_loop(0, PEER_TOK, tok, 0, unroll=32)

    half = D_KEY // 2
    qh = jnp.dot(h_buf[route_slot], wq_ref[...], preferred_element_type=F32).astype(BF16)
    s1 = _dot_nt(k1_ref[...], qh[:, :half])
    s2 = _dot_nt(k2_ref[...], qh[:, half:])

    a_act = _dot_nt(h_buf[peer_slot], u_ref[...])
    v1, i1 = _top_keys(s1)
    v2, i2 = _top_keys(s2)

    wk = wtok[:, pl.ds(pl.multiple_of(c * PEER_CHUNK_KEYS, PEER_CHUNK_KEYS), PEER_CHUNK_KEYS), :]
    wk = jnp.swapaxes(wk, 0, 1)
    w = jnp.concatenate([wk[j] for j in range(PEER_CHUNK_KEYS)], axis=1)
    wp = (w * _gelu_exact(a_act)).astype(BF16)
    part = jnp.dot(wp, v_ref[...], preferred_element_type=F32)
    top_s, top_e = _top_candidates(v1, i1, v2, i2)
    acc[...] += part

    ex = jnp.exp(top_s - jnp.max(top_s, axis=0, keepdims=True))
    rows = pl.ds(pl.multiple_of(c * PEER_TOPK, PEER_TOPK), PEER_TOPK)
    gt_buf[route_slot, rows, :] = ex / jnp.sum(ex, axis=0, keepdims=True)
    et_buf[route_slot, rows, :] = top_e

    @pl.when(c == PEER_N_CHUNKS - 1)
    def _():
        o_ref[...] = xp_ref[...] + acc[...]


def _peer(x, g, w_query, keys1, keys2, eu, ev):
    n_blk = N_TOK // PEER_TOK
    hk = PEER_HEADS * PEER_TOPK
    route_blk = lambda i, c: (jnp.minimum(i, n_blk - 1), 0)
    peer_blk = lambda i, c: (jnp.maximum(i - 1, 0), 0)
    const2 = lambda i, c: (0, 0)
    chunk = pl.BlockSpec((PEER_CHUNK, D_MODEL), lambda i, c: (c, 0))
    return pl.pallas_call(
        _peer_kernel,
        grid=(n_blk + 1, PEER_N_CHUNKS),
        in_specs=[pl.BlockSpec((PEER_TOK, D_MODEL), route_blk),
                  pl.BlockSpec((PEER_TOK, D_MODEL), peer_blk),
                  pl.BlockSpec((1, D_MODEL), const2),
                  pl.BlockSpec((D_MODEL, D_KEY), lambda i, c: (0, c)),
                  pl.BlockSpec((N_KEYS, D_KEY // 2), const2),
                  pl.BlockSpec((N_KEYS, D_KEY // 2), const2),
                  chunk, chunk],
        out_specs=pl.BlockSpec((PEER_TOK, D_MODEL), peer_blk),
        out_shape=jax.ShapeDtypeStruct((N_TOK, D_MODEL), F32),
        scratch_shapes=[pltpu.VMEM((2, PEER_TOK, D_MODEL), BF16),
                        pltpu.VMEM((2, hk, PEER_TOK), F32),
                        pltpu.VMEM((2, hk, PEER_TOK), F32),
                        pltpu.VMEM((PEER_TOK, hk), F32),
                        pltpu.VMEM((PEER_TOK, hk), F32),
                        pltpu.VMEM((PEER_TOK, hk), F32),
                        pltpu.VMEM((PEER_TOK, N_KEYS, N_KEYS), F32),
                        pltpu.VMEM((PEER_TOK, D_MODEL), F32)],
        compiler_params=pltpu.CompilerParams(dimension_semantics=("arbitrary", "arbitrary"),
                                             vmem_limit_bytes=PEER_VMEM_LIMIT),
        name="peer",
    )(x, x, g, w_query, keys1, keys2, eu, ev)


def _ple_kernel(x_ref, p_ref, g_ref, wg_ref, wp_ref, gf_ref, o_ref, *, final):
    x = x_ref[...]
    hp = _rms(x, g_ref[...]).astype(BF16)
    gate = jax.nn.sigmoid(jnp.dot(hp, wg_ref[...], preferred_element_type=F32))
    proj = jnp.dot(p_ref[...].astype(BF16), wp_ref[...], preferred_element_type=F32)
    y = x + gate * proj
    if final:
        y = _rms(y, gf_ref[...])
    o_ref[...] = y


def _ple(x, p, g, w_gate, w_proj, g_final, final):
    const2 = lambda i: (0, 0)
    return pl.pallas_call(
        functools.partial(_ple_kernel, final=final),
        grid=(N_BLOCKS,),
        in_specs=[pl.BlockSpec((TOK_BLOCK, D_MODEL), lambda i: (i, 0)),
                  pl.BlockSpec((TOK_BLOCK, PLE_DIM), lambda i: (i, 0)),
                  pl.BlockSpec((1, D_MODEL), const2),
                  pl.BlockSpec((D_MODEL, D_MODEL), const2),
                  pl.BlockSpec((PLE_DIM, D_MODEL), const2),
                  pl.BlockSpec((1, D_MODEL), const2)],
        out_specs=pl.BlockSpec((TOK_BLOCK, D_MODEL), lambda i: (i, 0)),
        out_shape=jax.ShapeDtypeStruct((N_TOK, D_MODEL), F32),
        compiler_params=pltpu.CompilerParams(dimension_semantics=("arbitrary",),
                                             vmem_limit_bytes=VMEM_LIMIT),
        name="ple",
    )(x, p, g, w_gate, w_proj, g_final)


def _bias_table(rpb):
    cols = np.arange(GRID_W)
    col_start = np.clip(cols - WIN_COLS // 2, 0, GRID_W - WIN_COLS)
    kc = np.arange(GRID_W)
    in_win = (kc[None, :] >= col_start[:, None]) & (kc[None, :] < col_start[:, None] + WIN_COLS)
    col_off = np.clip(kc[None, :] - cols[:, None] + (WIN_COLS - 1), 0, 2 * WIN_COLS - 2)
    delta = np.arange(WIN_ROWS)
    j = np.arange(WIN_ROWS)
    row_off = j[None, :] - delta[:, None] + (WIN_ROWS - 1)
    t = rpb[:, row_off][:, :, :, col_off]
    t = jnp.where(in_win[None, None, None], t, NEG_BIG)
    t = t.transpose(1, 0, 3, 2, 4)
    return t.reshape(WIN_ROWS, ATTN_HEADS // 2, 2 * GRID_W, WIN_ROWS * GRID_W).astype(F32)


def kernel(x_prompt, x_sample, p_prompt, p_sample, g_mix, w_in, rpb, conv_w, g_attn_out, g_conv_out, w_out,
           g_ffn, w_query, sub_keys1, sub_keys2, expert_u, expert_v, g_ple, w_ple_gate, w_ple_proj, g_final):
    x = jnp.concatenate([x_prompt.reshape(SEQ, D_MODEL), x_sample.reshape(DEC_BATCH * DEC_SEQ, D_MODEL)], axis=0)
    p = jnp.concatenate([p_prompt.reshape(DEPTH, SEQ, PLE_DIM),
                         p_sample.reshape(DEPTH, DEC_BATCH * DEC_SEQ, PLE_DIM)], axis=1)
    group = np.arange(ATTN_W) // HEAD_DIM
    gmat = jnp.asarray(group[:, None] == group[None, :], dtype=BF16)
    row = lambda g: g.reshape(1, -1)
    for i in range(DEPTH):
        q, k, v, gb, gc, gu = _proj(x, row(g_mix[i]), w_in[i].astype(BF16))
        x = _mixer(x, q, k, v, gb, gc, gu, _bias_table(rpb[i]), conv_w[i], row(g_attn_out[i]),
                   row(g_conv_out[i]), gmat, w_out[i].astype(BF16))
        x = _peer(x, row(g_ffn[i]), w_query[i].astype(BF16), sub_keys1[i].astype(BF16),
                  sub_keys2[i].astype(BF16), expert_u[i].astype(BF16), expert_v[i].astype(BF16))
        x = _ple(x, p[i], row(g_ple[i]), w_ple_gate[i].astype(BF16), w_ple_proj[i].astype(BF16),
                 row(g_final), final=(i == DEPTH - 1))
    y_prompt = x[:SEQ].reshape(1, SEQ, D_MODEL)
    y_sample = x[SEQ:].reshape(DEC_BATCH, DEC_SEQ, D_MODEL)
    return (y_prompt, y_sample)
```

```python
import functools

import numpy as np
import jax
import jax.numpy as jnp
from jax import lax
from jax.experimental import pallas as pl
from jax.experimental.pallas import tpu as pltpu

F32 = jnp.float32
BF16 = jnp.bfloat16

D_MODEL = 1024
DEPTH = 2
SEQ = 16384
DEC_BATCH = 4
DEC_SEQ = 4096
N_TOK = SEQ + DEC_BATCH * DEC_SEQ

GRID_W = 64
WIN_ROWS = 8
WIN_COLS = 16
ATTN_HEADS = 8
HEAD_DIM = 64
ATTN_W = ATTN_HEADS * HEAD_DIM
CONV_W = D_MODEL - ATTN_W
N_KEYS = 128
N_EXPERTS = N_KEYS * N_KEYS
PEER_HEADS = 8
D_KEY = 256
PEER_TOPK = 16
PLE_DIM = 256
EPS = 1e-6
NEG_BIG = -1e30

LANES = 128
ROW_BLOCK = WIN_ROWS
TOK_BLOCK = ROW_BLOCK * GRID_W
N_BLOCKS = N_TOK // TOK_BLOCK
PROMPT_ROWS = SEQ // GRID_W
SAMPLE_ROWS = DEC_SEQ // GRID_W
PROMPT_BLOCKS = PROMPT_ROWS // ROW_BLOCK
SAMPLE_BLOCKS = SAMPLE_ROWS // ROW_BLOCK
HALO = 8

PEER_TOK = 256
PEER_CHUNK_KEYS = 16
PEER_CHUNK = PEER_CHUNK_KEYS * N_KEYS
PEER_N_CHUNKS = N_EXPERTS // PEER_CHUNK
assert PEER_N_CHUNKS == PEER_HEADS

VMEM_LIMIT = 48 * 1024 * 1024
PEER_VMEM_LIMIT = 56 * 1024 * 1024


def _rms(x, g):
    return x * lax.rsqrt(jnp.mean(x * x, axis=-1, keepdims=True) + EPS) * g


def _dot_nt(a, b):
    return lax.dot_general(a, b, (((1,), (1,)), ((), ())), preferred_element_type=F32)


def _proj_kernel(x_ref, g_ref, w_ref, q_ref, k_ref, v_ref, gb_ref, gc_ref, gu_ref):
    hb = _rms(x_ref[...], g_ref[...]).astype(BF16)
    outs = (q_ref, k_ref, v_ref, gb_ref, gc_ref, gu_ref)
    for j, o_ref in enumerate(outs):
        z = jnp.dot(hb, w_ref[:, j * ATTN_W:(j + 1) * ATTN_W], preferred_element_type=F32)
        if j == 0:
            z = z * (HEAD_DIM ** -0.5)
        o_ref[...] = z.astype(o_ref.dtype)


def _proj(x, g, w_in):
    tok = pl.BlockSpec((TOK_BLOCK, ATTN_W), lambda i: (i, 0))
    return pl.pallas_call(
        _proj_kernel,
        grid=(N_BLOCKS,),
        in_specs=[pl.BlockSpec((TOK_BLOCK, D_MODEL), lambda i: (i, 0)),
                  pl.BlockSpec((1, D_MODEL), lambda i: (0, 0)),
                  pl.BlockSpec((D_MODEL, 6 * ATTN_W), lambda i: (0, 0))],
        out_specs=[tok] * 6,
        out_shape=[jax.ShapeDtypeStruct((N_TOK, ATTN_W), BF16)] * 3
                  + [jax.ShapeDtypeStruct((N_TOK, ATTN_W), F32)] * 3,
        compiler_params=pltpu.CompilerParams(dimension_semantics=("arbitrary",),
                                             vmem_limit_bytes=VMEM_LIMIT),
        name="proj",
    )(x, g, w_in)


def _group_norm(xv, gmat, g):
    sq = xv * xv
    hi = sq.astype(BF16)
    lo = (sq - hi.astype(F32)).astype(BF16)
    ms = (jnp.dot(hi, gmat, preferred_element_type=F32)
          + jnp.dot(lo, gmat, preferred_element_type=F32)) * (1.0 / HEAD_DIM)
    return xv * lax.rsqrt(ms + EPS) * g


def _mixer_kernel(x_ref, q_ref, kp_ref, kc_ref, kn_ref, vp_ref, vc_ref, vn_ref,
                  gb_ref, gc_ref, gu_ref, gcp_ref, gup_ref, gcn_ref, gun_ref,
                  bias_ref, cw_ref, ga_ref, gcv_ref, gmat_ref, wo_ref,
                  o_ref, kbuf, vbuf, abuf):
    i = pl.program_id(0)
    is_prompt = i < PROMPT_BLOCKS
    sample = (i - PROMPT_BLOCKS) // SAMPLE_BLOCKS
    seq_r0 = jnp.where(is_prompt, 0, PROMPT_ROWS + SAMPLE_ROWS * sample)
    seq_r1 = jnp.where(is_prompt, PROMPT_ROWS, seq_r0 + SAMPLE_ROWS)
    blk_r0 = ROW_BLOCK * i

    kbuf[0:TOK_BLOCK, :] = kp_ref[...]
    kbuf[TOK_BLOCK:2 * TOK_BLOCK, :] = kc_ref[...]
    kbuf[2 * TOK_BLOCK:3 * TOK_BLOCK, :] = kn_ref[...]
    vbuf[0:TOK_BLOCK, :] = vp_ref[...]
    vbuf[TOK_BLOCK:2 * TOK_BLOCK, :] = vc_ref[...]
    vbuf[2 * TOK_BLOCK:3 * TOK_BLOCK, :] = vn_ref[...]

    lane = lax.broadcasted_iota(jnp.int32, (GRID_W, LANES), 1)
    low_half = lane < HEAD_DIM
    n_win = WIN_ROWS * GRID_W
    for jr in range(ROW_BLOCK):
        r = blk_r0 + jr
        rs = jnp.clip(r - WIN_ROWS // 2, seq_r0, seq_r1 - WIN_ROWS)
        delta = r - rs
        start = pl.multiple_of((rs - blk_r0 + ROW_BLOCK) * GRID_W, GRID_W)
        pairs = range(ATTN_HEADS // 2)
        col = [slice(p * LANES, (p + 1) * LANES) for p in pairs]
        scores = []
        for p in pairs:
            qp = q_ref[jr * GRID_W:(jr + 1) * GRID_W, col[p]]
            zero = jnp.zeros_like(qp)
            q2 = jnp.concatenate([jnp.where(low_half, qp, zero), jnp.where(low_half, zero, qp)], axis=0)
            scores.append(_dot_nt(q2, kbuf[pl.ds(start, n_win), col[p]]) + bias_ref[delta, p])
        probs, norms = [], []
        for s in scores:
            e = jnp.exp(s - jnp.max(s, axis=-1, keepdims=True))
            probs.append(e.astype(BF16))
            norms.append(jnp.sum(e, axis=-1, keepdims=True))
        for p in pairs:
            o2 = jnp.dot(probs[p], vbuf[pl.ds(start, n_win), col[p]], preferred_element_type=F32) / norms[p]
            abuf[jr * GRID_W:(jr + 1) * GRID_W, col[p]] = jnp.where(low_half, o2[:GRID_W], o2[GRID_W:])

    first = blk_r0 == seq_r0
    last = blk_r0 + ROW_BLOCK == seq_r1
    cu = gc_ref[...] * gu_ref[...]
    prev_row = jnp.where(first, 0.0, gcp_ref[HALO - 1:HALO, :] * gup_ref[HALO - 1:HALO, :])
    next_row = jnp.where(last, 0.0, gcn_ref[0:1, :] * gun_ref[0:1, :])
    row = lax.broadcasted_iota(jnp.int32, (TOK_BLOCK, CONV_W), 0)
    up_prev = jnp.where(row == 0, prev_row, pltpu.roll(cu, 1, axis=0))
    up_next = jnp.where(row == TOK_BLOCK - 1, next_row, pltpu.roll(cu, TOK_BLOCK - 1, axis=0))
    conv = gb_ref[...] * (up_prev * cw_ref[0:1, :] + cu * cw_ref[1:2, :] + up_next * cw_ref[2:3, :])

    gmat = gmat_ref[...]
    attn_n = _group_norm(abuf[...], gmat, ga_ref[...]).astype(BF16)
    conv_n = _group_norm(conv, gmat, gcv_ref[...]).astype(BF16)
    y = (jnp.dot(attn_n, wo_ref[0:ATTN_W, :], preferred_element_type=F32)
         + jnp.dot(conv_n, wo_ref[ATTN_W:D_MODEL, :], preferred_element_type=F32))
    o_ref[...] = x_ref[...] + y


def _mixer(x, q, k, v, gb, gc, gu, bias, conv_w, g_attn, g_conv, gmat, w_out):
    cur = lambda i: (i, 0)
    prev = lambda i: (jnp.maximum(i - 1, 0), 0)
    nxt = lambda i: (jnp.minimum(i + 1, N_BLOCKS - 1), 0)
    halo_per_block = TOK_BLOCK // HALO
    hprev = lambda i: (jnp.maximum(i * halo_per_block - 1, 0), 0)
    hnext = lambda i: (jnp.minimum((i + 1) * halo_per_block, N_TOK // HALO - 1), 0)
    const2 = lambda i: (0, 0)
    blk = lambda m: pl.BlockSpec((TOK_BLOCK, ATTN_W), m)
    halo = lambda m: pl.BlockSpec((HALO, CONV_W), m)
    return pl.pallas_call(
        _mixer_kernel,
        grid=(N_BLOCKS,),
        in_specs=[pl.BlockSpec((TOK_BLOCK, D_MODEL), cur),
                  blk(cur), blk(prev), blk(cur), blk(nxt), blk(prev), blk(cur), blk(nxt),
                  blk(cur), blk(cur), blk(cur), halo(hprev), halo(hprev), halo(hnext), halo(hnext),
                  pl.BlockSpec((WIN_ROWS, ATTN_HEADS // 2, 2 * GRID_W, WIN_ROWS * GRID_W), lambda i: (0, 0, 0, 0)),
                  pl.BlockSpec((3, CONV_W), const2),
                  pl.BlockSpec((1, ATTN_W), const2),
                  pl.BlockSpec((1, CONV_W), const2),
                  pl.BlockSpec((ATTN_W, ATTN_W), const2),
                  pl.BlockSpec((D_MODEL, D_MODEL), const2)],
        out_specs=pl.BlockSpec((TOK_BLOCK, D_MODEL), cur),
        out_shape=jax.ShapeDtypeStruct((N_TOK, D_MODEL), F32),
        scratch_shapes=[pltpu.VMEM((3 * TOK_BLOCK, ATTN_W), BF16),
                        pltpu.VMEM((3 * TOK_BLOCK, ATTN_W), BF16),
                        pltpu.VMEM((TOK_BLOCK, ATTN_W), F32)],
        compiler_params=pltpu.CompilerParams(dimension_semantics=("arbitrary",),
                                             vmem_limit_bytes=VMEM_LIMIT),
        name="mixer",
    )(x, q, k, k, k, v, v, v, gb, gc, gu, gc, gu, gc, gu, bias, conv_w, g_attn, g_conv, gmat, w_out)


def _record(k, kidx, m, pick, vals, picks):
    sel = kidx == k
    return jnp.where(sel, m, vals), jnp.where(sel, pick, picks)


def _top_keys(s):
    n, t = s.shape
    half = n // 2
    r0 = lax.broadcasted_iota(jnp.int32, (half, t), 0).astype(F32)
    r1 = r0 + float(half)
    first = s[:half] >= s[half:]
    hi = jnp.where(first, s[:half], s[half:])
    lo = jnp.where(first, s[half:], s[:half])
    ihi = jnp.where(first, r0, r1)
    ilo = jnp.where(first, r1, r0)
    kidx = lax.broadcasted_iota(jnp.int32, (PEER_TOPK, t), 0)

    def body(k, carry):
        hi, lo, ihi, vals, picks = carry
        m = jnp.max(hi, axis=0, keepdims=True)
        pos = jnp.min(jnp.where(hi == m, ihi, float(n)), axis=0, keepdims=True)
        onehot = ihi == pos
        vals, picks = _record(k, kidx, m, pos, vals, picks)
        return (jnp.where(onehot, lo, hi), jnp.where(onehot, -jnp.inf, lo), jnp.where(onehot, ilo, ihi),
                vals, picks)

    zeros = jnp.zeros((PEER_TOPK, t), F32)
    out = lax.fori_loop(0, PEER_TOPK, body, (hi, lo, ihi, zeros, zeros), unroll=True)
    return out[3], out[4]


_CAND_PER_KA = [PEER_TOPK // (ka + 1) for ka in range(PEER_TOPK)]
_N_CAND = sum(_CAND_PER_KA)
_CAND_PAD = -_N_CAND % 8


def _top_candidates(v1, i1, v2, i2):
    t = v1.shape[1]
    s_rows, e_rows = [], []
    for ka, n_kb in enumerate(_CAND_PER_KA):
        s_rows.append(v1[ka:ka + 1, :] + v2[:n_kb, :])
        e_rows.append(i1[ka:ka + 1, :] * float(N_KEYS) + i2[:n_kb, :])
    s_rows.append(jnp.full((_CAND_PAD, t), -jnp.inf, F32))
    e_rows.append(jnp.zeros((_CAND_PAD, t), F32))
    s = jnp.concatenate(s_rows, axis=0)
    e = jnp.concatenate(e_rows, axis=0)
    rows = s.shape[0]
    ridx = lax.broadcasted_iota(jnp.int32, (rows, t), 0).astype(F32)
    kidx = lax.broadcasted_iota(jnp.int32, (PEER_TOPK, t), 0)

    def body(k, carry):
        s, vals, picks = carry
        m = jnp.max(s, axis=0, keepdims=True)
        pos = jnp.min(jnp.where(s == m, ridx, float(rows)), axis=0, keepdims=True)
        onehot = ridx == pos
        pick = jnp.max(jnp.where(onehot, e, -1.0), axis=0, keepdims=True)
        vals, picks = _record(k, kidx, m, pick, vals, picks)
        return jnp.where(onehot, -jnp.inf, s), vals, picks

    zeros = jnp.zeros((PEER_TOPK, t), F32)
    _, vals, picks = lax.fori_loop(0, PEER_TOPK, body, (s, zeros, zeros), unroll=True)
    return vals, picks


def _gelu_exact(x, half=0.5):
    return half * x * (1.0 + lax.erf(x * (2.0 ** -0.5)))


def _key_scores(hb, wq, k1, k2):
    half = D_KEY // 2
    qh = jnp.dot(hb, wq, preferred_element_type=F32).astype(BF16)
    return _dot_nt(k1, qh[:, :half]), _dot_nt(k2, qh[:, half:])


def _peer_kernel(xr_ref, xp_ref, g_ref, wq0_ref, wq_ref, k1_ref, k2_ref, u_ref, v_ref, o_ref,
                 h_buf, sc_buf, gt_buf, et_buf, a_buf, b_buf, gate_buf, wtok, acc):
    i = pl.program_id(0)
    c = pl.program_id(1)
    last_c = c == PEER_N_CHUNKS - 1
    route_slot = i % 2
    peer_slot = 1 - route_slot
    score_slot = c % 2
    h_peer_slot = (i + 2) % 3

    @pl.when(jnp.logical_and(i == 0, c == 0))
    def _():
        hb = _rms(xr_ref[...], g_ref[...]).astype(BF16)
        h_buf[0] = hb
        s1, s2 = _key_scores(hb, wq0_ref[...], k1_ref[...], k2_ref[...])
        sc_buf[0, 0] = s1
        sc_buf[0, 1] = s2
        h_buf[2] = jnp.zeros((PEER_TOK, D_MODEL), BF16)
        gt_buf[1] = jnp.zeros((PEER_HEADS * PEER_TOPK, PEER_TOK), F32)
        et_buf[1] = jnp.zeros((PEER_HEADS * PEER_TOPK, PEER_TOK), F32)

    @pl.when(last_c)
    def _():
        h_buf[(i + 1) % 3] = _rms(xr_ref[...], g_ref[...]).astype(BF16)

    @pl.when(c == 0)
    def _():
        acc[...] = jnp.zeros_like(acc)
        e = et_buf[peer_slot].T
        a = jnp.floor(e * (1.0 / N_KEYS))
        a_buf[...] = a
        b_buf[...] = e - a * float(N_KEYS)
        gate_buf[...] = gt_buf[peer_slot].T
        sub = lax.broadcasted_iota(jnp.int32, (N_KEYS, LANES), 0).astype(F32)

        def tok(t, carry):
            ar = a_buf[pl.ds(t, 1), :]
            br = b_buf[pl.ds(t, 1), :]
            gr = gate_buf[pl.ds(t, 1), :]
            pg = jnp.where(sub == ar, gr, 0.0).astype(BF16)
            qb = jnp.where(sub == br, 1.0, 0.0).astype(BF16)
            wtok[t] = _dot_nt(pg, qb)
            return carry

        lax.fori_loop(0, PEER_TOK, tok, 0, unroll=32)

    v1, i1 = _top_keys(sc_buf[score_slot, 0])
    v2, i2 = _top_keys(sc_buf[score_slot, 1])

    h_next_slot = jnp.where(last_c, i + 1, i) % 3
    s1, s2 = _key_scores(h_buf[h_next_slot], wq_ref[...], k1_ref[...], k2_ref[...])
    sc_buf[1 - score_slot, 0] = s1
    sc_buf[1 - score_slot, 1] = s2

    a_act = _dot_nt(h_buf[h_peer_slot], u_ref[...])
    wk = wtok[:, pl.ds(pl.multiple_of(c * PEER_CHUNK_KEYS, PEER_CHUNK_KEYS), PEER_CHUNK_KEYS), :]
    wk = jnp.swapaxes(wk, 0, 1)
    w = jnp.concatenate([wk[j] for j in range(PEER_CHUNK_KEYS)], axis=1)
    half = 0.5 + jnp.where(i1[PEER_TOPK - 1:, 0:1] + i2[PEER_TOPK - 1:, 0:1] < -1.0, 1.0, 0.0)
    wg = w * _gelu_exact(a_act, half)
    acc[...] += jnp.dot(wg.astype(BF16), v_ref[...], preferred_element_type=F32)

    top_s, top_e = _top_candidates(v1, i1, v2, i2)
    ex = jnp.exp(top_s - jnp.max(top_s, axis=0, keepdims=True))
    rows = pl.ds(pl.multiple_of(c * PEER_TOPK, PEER_TOPK), PEER_TOPK)
    gt_buf[route_slot, rows, :] = ex / jnp.sum(ex, axis=0, keepdims=True)
    et_buf[route_slot, rows, :] = top_e

    @pl.when(last_c)
    def _():
        o_ref[...] = xp_ref[...] + acc[...]


def _peer(x, g, w_query, keys1, keys2, eu, ev):
    n_blk = N_TOK // PEER_TOK
    hk = PEER_HEADS * PEER_TOPK
    route_blk = lambda i, c: (jnp.minimum(i + c // (PEER_N_CHUNKS - 1), n_blk - 1), 0)
    peer_blk = lambda i, c: (jnp.maximum(i - 1, 0), 0)
    const2 = lambda i, c: (0, 0)
    chunk = pl.BlockSpec((PEER_CHUNK, D_MODEL), lambda i, c: (c, 0))
    return pl.pallas_call(
        _peer_kernel,
        grid=(n_blk + 1, PEER_N_CHUNKS),
        in_specs=[pl.BlockSpec((PEER_TOK, D_MODEL), route_blk),
                  pl.BlockSpec((PEER_TOK, D_MODEL), peer_blk),
                  pl.BlockSpec((1, D_MODEL), const2),
                  pl.BlockSpec((D_MODEL, D_KEY), const2),
                  pl.BlockSpec((D_MODEL, D_KEY), lambda i, c: (0, (c + 1) % PEER_HEADS)),
                  pl.BlockSpec((N_KEYS, D_KEY // 2), const2),
                  pl.BlockSpec((N_KEYS, D_KEY // 2), const2),
                  chunk, chunk],
        out_specs=pl.BlockSpec((PEER_TOK, D_MODEL), peer_blk),
        out_shape=jax.ShapeDtypeStruct((N_TOK, D_MODEL), F32),
        scratch_shapes=[pltpu.VMEM((3, PEER_TOK, D_MODEL), BF16),
                        pltpu.VMEM((2, 2, N_KEYS, PEER_TOK), F32),
                        pltpu.VMEM((2, hk, PEER_TOK), F32),
                        pltpu.VMEM((2, hk, PEER_TOK), F32),
                        pltpu.VMEM((PEER_TOK, hk), F32),
                        pltpu.VMEM((PEER_TOK, hk), F32),
                        pltpu.VMEM((PEER_TOK, hk), F32),
                        pltpu.VMEM((PEER_TOK, N_KEYS, N_KEYS), F32),
                        pltpu.VMEM((PEER_TOK, D_MODEL), F32)],
        compiler_params=pltpu.CompilerParams(dimension_semantics=("arbitrary", "arbitrary"),
                                             vmem_limit_bytes=PEER_VMEM_LIMIT),
        name="peer",
    )(x, x, g, w_query, w_query, keys1, keys2, eu, ev)


def _ple_kernel(x_ref, p_ref, g_ref, wg_ref, wp_ref, gf_ref, o_ref, *, final):
    x = x_ref[...]
    hp = _rms(x, g_ref[...]).astype(BF16)
    gate = jax.nn.sigmoid(jnp.dot(hp, wg_ref[...], preferred_element_type=F32))
    proj = jnp.dot(p_ref[...].astype(BF16), wp_ref[...], preferred_element_type=F32)
    y = x + gate * proj
    if final:
        y = _rms(y, gf_ref[...])
    o_ref[...] = y


def _ple(x, p, g, w_gate, w_proj, g_final, final):
    const2 = lambda i: (0, 0)
    return pl.pallas_call(
        functools.partial(_ple_kernel, final=final),
        grid=(N_BLOCKS,),
        in_specs=[pl.BlockSpec((TOK_BLOCK, D_MODEL), lambda i: (i, 0)),
                  pl.BlockSpec((TOK_BLOCK, PLE_DIM), lambda i: (i, 0)),
                  pl.BlockSpec((1, D_MODEL), const2),
                  pl.BlockSpec((D_MODEL, D_MODEL), const2),
                  pl.BlockSpec((PLE_DIM, D_MODEL), const2),
                  pl.BlockSpec((1, D_MODEL), const2)],
        out_specs=pl.BlockSpec((TOK_BLOCK, D_MODEL), lambda i: (i, 0)),
        out_shape=jax.ShapeDtypeStruct((N_TOK, D_MODEL), F32),
        compiler_params=pltpu.CompilerParams(dimension_semantics=("arbitrary",),
                                             vmem_limit_bytes=VMEM_LIMIT),
        name="ple",
    )(x, p, g, w_gate, w_proj, g_final)


def _bias_table(rpb):
    cols = np.arange(GRID_W)
    col_start = np.clip(cols - WIN_COLS // 2, 0, GRID_W - WIN_COLS)
    kc = np.arange(GRID_W)
    in_win = (kc[None, :] >= col_start[:, None]) & (kc[None, :] < col_start[:, None] + WIN_COLS)
    col_off = np.clip(kc[None, :] - cols[:, None] + (WIN_COLS - 1), 0, 2 * WIN_COLS - 2)
    delta = np.arange(WIN_ROWS)
    j = np.arange(WIN_ROWS)
    row_off = j[None, :] - delta[:, None] + (WIN_ROWS - 1)
    t = rpb[:, row_off][:, :, :, col_off]
    t = jnp.where(in_win[None, None, None], t, NEG_BIG)
    t = t.transpose(1, 0, 3, 2, 4)
    return t.reshape(WIN_ROWS, ATTN_HEADS // 2, 2 * GRID_W, WIN_ROWS * GRID_W).astype(F32)


def kernel(x_prompt, x_sample, p_prompt, p_sample, g_mix, w_in, rpb, conv_w, g_attn_out, g_conv_out, w_out,
           g_ffn, w_query, sub_keys1, sub_keys2, expert_u, expert_v, g_ple, w_ple_gate, w_ple_proj, g_final):
    x = jnp.concatenate([x_prompt.reshape(SEQ, D_MODEL), x_sample.reshape(DEC_BATCH * DEC_SEQ, D_MODEL)], axis=0)
    p = jnp.concatenate([p_prompt.reshape(DEPTH, SEQ, PLE_DIM),
                         p_sample.reshape(DEPTH, DEC_BATCH * DEC_SEQ, PLE_DIM)], axis=1)
    group = np.arange(ATTN_W) // HEAD_DIM
    gmat = jnp.asarray(group[:, None] == group[None, :], dtype=BF16)
    row = lambda g: g.reshape(1, -1)
    for i in range(DEPTH):
        q, k, v, gb, gc, gu = _proj(x, row(g_mix[i]), w_in[i].astype(BF16))
        x = _mixer(x, q, k, v, gb, gc, gu, _bias_table(rpb[i]), conv_w[i], row(g_attn_out[i]),
                   row(g_conv_out[i]), gmat, w_out[i].astype(BF16))
        x = _peer(x, row(g_ffn[i]), w_query[i].astype(BF16), sub_keys1[i].astype(BF16),
                  sub_keys2[i].astype(BF16), expert_u[i].astype(BF16), expert_v[i].astype(BF16))
        x = _ple(x, p[i], row(g_ple[i]), w_ple_gate[i].astype(BF16), w_ple_proj[i].astype(BF16),
                 row(g_final), final=(i == DEPTH - 1))
    y_prompt = x[:SEQ].reshape(1, SEQ, D_MODEL)
    y_sample = x[SEQ:].reshape(DEC_BATCH, DEC_SEQ, D_MODEL)
    return (y_prompt, y_sample)
```

```python
import functools

import numpy as np
import jax
import jax.numpy as jnp
from jax import lax
from jax.experimental import pallas as pl
from jax.experimental.pallas import tpu as pltpu

F32 = jnp.float32
BF16 = jnp.bfloat16

D_MODEL = 1024
DEPTH = 2
SEQ = 16384
DEC_BATCH = 4
DEC_SEQ = 4096
N_TOK = SEQ + DEC_BATCH * DEC_SEQ

GRID_W = 64
WIN_ROWS = 8
WIN_COLS = 16
ATTN_HEADS = 8
HEAD_DIM = 64
ATTN_W = ATTN_HEADS * HEAD_DIM
CONV_W = D_MODEL - ATTN_W
N_KEYS = 128
N_EXPERTS = N_KEYS * N_KEYS
PEER_HEADS = 8
D_KEY = 256
PEER_TOPK = 16
PLE_DIM = 256
EPS = 1e-6
NEG_BIG = -1e30

LANES = 128
ROW_BLOCK = WIN_ROWS
TOK_BLOCK = ROW_BLOCK * GRID_W
N_BLOCKS = N_TOK // TOK_BLOCK
PROMPT_ROWS = SEQ // GRID_W
SAMPLE_ROWS = DEC_SEQ // GRID_W
PROMPT_BLOCKS = PROMPT_ROWS // ROW_BLOCK
SAMPLE_BLOCKS = SAMPLE_ROWS // ROW_BLOCK
HALO = 8

PEER_TOK = 256
PEER_CHUNK_KEYS = 16
PEER_CHUNK = PEER_CHUNK_KEYS * N_KEYS
PEER_N_CHUNKS = N_EXPERTS // PEER_CHUNK
assert PEER_N_CHUNKS == PEER_HEADS

VMEM_LIMIT = 48 * 1024 * 1024
PEER_VMEM_LIMIT = 56 * 1024 * 1024


def _rms(x, g):
    return x * lax.rsqrt(jnp.mean(x * x, axis=-1, keepdims=True) + EPS) * g


def _dot_nt(a, b):
    return lax.dot_general(a, b, (((1,), (1,)), ((), ())), preferred_element_type=F32)


def _split_specs(width, joint):
    off = PROMPT_BLOCKS if joint else 0
    return (pl.BlockSpec((TOK_BLOCK, width), lambda i: (jnp.minimum(i, PROMPT_BLOCKS - 1), 0)),
            pl.BlockSpec((TOK_BLOCK, width), lambda i: (jnp.maximum(i - PROMPT_BLOCKS, 0) + off, 0)))


def _pick_part(prompt_ref, sample_ref):
    return jnp.where(pl.program_id(0) < PROMPT_BLOCKS, prompt_ref[...], sample_ref[...])


def _parts(x):
    return (x[0], x[1], False) if isinstance(x, tuple) else (x, x, True)


def _proj_kernel(xa_ref, xb_ref, g_ref, w_ref, q_ref, k_ref, v_ref, gb_ref, gc_ref, gu_ref):
    hb = _rms(_pick_part(xa_ref, xb_ref), g_ref[...]).astype(BF16)
    outs = (q_ref, k_ref, v_ref, gb_ref, gc_ref, gu_ref)
    for j, o_ref in enumerate(outs):
        z = jnp.dot(hb, w_ref[:, j * ATTN_W:(j + 1) * ATTN_W], preferred_element_type=F32)
        if j == 0:
            z = z * (HEAD_DIM ** -0.5)
        o_ref[...] = z.astype(o_ref.dtype)


def _proj(x, g, w_in):
    xa, xb, joint = _parts(x)
    tok = pl.BlockSpec((TOK_BLOCK, ATTN_W), lambda i: (i, 0))
    return pl.pallas_call(
        _proj_kernel,
        grid=(N_BLOCKS,),
        in_specs=[*_split_specs(D_MODEL, joint),
                  pl.BlockSpec((1, D_MODEL), lambda i: (0, 0)),
                  pl.BlockSpec((D_MODEL, 6 * ATTN_W), lambda i: (0, 0))],
        out_specs=[tok] * 6,
        out_shape=[jax.ShapeDtypeStruct((N_TOK, ATTN_W), BF16)] * 3
                  + [jax.ShapeDtypeStruct((N_TOK, ATTN_W), F32)] * 3,
        compiler_params=pltpu.CompilerParams(dimension_semantics=("arbitrary",),
                                             vmem_limit_bytes=VMEM_LIMIT),
        name="proj",
    )(xa, xb, g, w_in)


def _group_norm(xv, gmat, g):
    sq = xv * xv
    hi = sq.astype(BF16)
    lo = (sq - hi.astype(F32)).astype(BF16)
    ms = (jnp.dot(hi, gmat, preferred_element_type=F32)
          + jnp.dot(lo, gmat, preferred_element_type=F32)) * (1.0 / HEAD_DIM)
    return xv * lax.rsqrt(ms + EPS) * g


def _mixer_kernel(xa_ref, xb_ref, q_ref, kp_ref, kc_ref, kn_ref, vp_ref, vc_ref, vn_ref,
                  gb_ref, gc_ref, gu_ref, gcp_ref, gup_ref, gcn_ref, gun_ref,
                  bias_ref, cw_ref, ga_ref, gcv_ref, gmat_ref, wo_ref,
                  o_ref, kbuf, vbuf, abuf):
    i = pl.program_id(0)
    is_prompt = i < PROMPT_BLOCKS
    sample = (i - PROMPT_BLOCKS) // SAMPLE_BLOCKS
    seq_r0 = jnp.where(is_prompt, 0, PROMPT_ROWS + SAMPLE_ROWS * sample)
    seq_r1 = jnp.where(is_prompt, PROMPT_ROWS, seq_r0 + SAMPLE_ROWS)
    blk_r0 = ROW_BLOCK * i

    kbuf[0:TOK_BLOCK, :] = kp_ref[...]
    kbuf[TOK_BLOCK:2 * TOK_BLOCK, :] = kc_ref[...]
    kbuf[2 * TOK_BLOCK:3 * TOK_BLOCK, :] = kn_ref[...]
    vbuf[0:TOK_BLOCK, :] = vp_ref[...]
    vbuf[TOK_BLOCK:2 * TOK_BLOCK, :] = vc_ref[...]
    vbuf[2 * TOK_BLOCK:3 * TOK_BLOCK, :] = vn_ref[...]

    lane = lax.broadcasted_iota(jnp.int32, (GRID_W, LANES), 1)
    low_half = lane < HEAD_DIM
    n_win = WIN_ROWS * GRID_W
    for jr in range(ROW_BLOCK):
        r = blk_r0 + jr
        rs = jnp.clip(r - WIN_ROWS // 2, seq_r0, seq_r1 - WIN_ROWS)
        delta = r - rs
        start = pl.multiple_of((rs - blk_r0 + ROW_BLOCK) * GRID_W, GRID_W)
        pairs = range(ATTN_HEADS // 2)
        col = [slice(p * LANES, (p + 1) * LANES) for p in pairs]
        scores = []
        for p in pairs:
            qp = q_ref[jr * GRID_W:(jr + 1) * GRID_W, col[p]]
            zero = jnp.zeros_like(qp)
            q2 = jnp.concatenate([jnp.where(low_half, qp, zero), jnp.where(low_half, zero, qp)], axis=0)
            scores.append(_dot_nt(q2, kbuf[pl.ds(start, n_win), col[p]]) + bias_ref[delta, p])
        probs, norms = [], []
        for s in scores:
            e = jnp.exp(s - jnp.max(s, axis=-1, keepdims=True))
            probs.append(e.astype(BF16))
            norms.append(jnp.sum(e, axis=-1, keepdims=True))
        for p in pairs:
            o2 = jnp.dot(probs[p], vbuf[pl.ds(start, n_win), col[p]], preferred_element_type=F32) / norms[p]
            abuf[jr * GRID_W:(jr + 1) * GRID_W, col[p]] = jnp.where(low_half, o2[:GRID_W], o2[GRID_W:])

    first = blk_r0 == seq_r0
    last = blk_r0 + ROW_BLOCK == seq_r1
    cu = gc_ref[...] * gu_ref[...]
    prev_row = jnp.where(first, 0.0, gcp_ref[HALO - 1:HALO, :] * gup_ref[HALO - 1:HALO, :])
    next_row = jnp.where(last, 0.0, gcn_ref[0:1, :] * gun_ref[0:1, :])
    row = lax.broadcasted_iota(jnp.int32, (TOK_BLOCK, CONV_W), 0)
    up_prev = jnp.where(row == 0, prev_row, pltpu.roll(cu, 1, axis=0))
    up_next = jnp.where(row == TOK_BLOCK - 1, next_row, pltpu.roll(cu, TOK_BLOCK - 1, axis=0))
    conv = gb_ref[...] * (up_prev * cw_ref[0:1, :] + cu * cw_ref[1:2, :] + up_next * cw_ref[2:3, :])

    gmat = gmat_ref[...]
    attn_n = _group_norm(abuf[...], gmat, ga_ref[...]).astype(BF16)
    conv_n = _group_norm(conv, gmat, gcv_ref[...]).astype(BF16)
    y = (jnp.dot(attn_n, wo_ref[0:ATTN_W, :], preferred_element_type=F32)
         + jnp.dot(conv_n, wo_ref[ATTN_W:D_MODEL, :], preferred_element_type=F32))
    o_ref[...] = _pick_part(xa_ref, xb_ref) + y


def _mixer(x, q, k, v, gb, gc, gu, bias, conv_w, g_attn, g_conv, gmat, w_out):
    xa, xb, joint = _parts(x)
    cur = lambda i: (i, 0)
    prev = lambda i: (jnp.maximum(i - 1, 0), 0)
    nxt = lambda i: (jnp.minimum(i + 1, N_BLOCKS - 1), 0)
    halo_per_block = TOK_BLOCK // HALO
    hprev = lambda i: (jnp.maximum(i * halo_per_block - 1, 0), 0)
    hnext = lambda i: (jnp.minimum((i + 1) * halo_per_block, N_TOK // HALO - 1), 0)
    const2 = lambda i: (0, 0)
    blk = lambda m: pl.BlockSpec((TOK_BLOCK, ATTN_W), m)
    halo = lambda m: pl.BlockSpec((HALO, CONV_W), m)
    return pl.pallas_call(
        _mixer_kernel,
        grid=(N_BLOCKS,),
        in_specs=[*_split_specs(D_MODEL, joint),
                  blk(cur), blk(prev), blk(cur), blk(nxt), blk(prev), blk(cur), blk(nxt),
                  blk(cur), blk(cur), blk(cur), halo(hprev), halo(hprev), halo(hnext), halo(hnext),
                  pl.BlockSpec((WIN_ROWS, ATTN_HEADS // 2, 2 * GRID_W, WIN_ROWS * GRID_W), lambda i: (0, 0, 0, 0)),
                  pl.BlockSpec((3, CONV_W), const2),
                  pl.BlockSpec((1, ATTN_W), const2),
                  pl.BlockSpec((1, CONV_W), const2),
                  pl.BlockSpec((ATTN_W, ATTN_W), const2),
                  pl.BlockSpec((D_MODEL, D_MODEL), const2)],
        out_specs=pl.BlockSpec((TOK_BLOCK, D_MODEL), cur),
        out_shape=jax.ShapeDtypeStruct((N_TOK, D_MODEL), F32),
        scratch_shapes=[pltpu.VMEM((3 * TOK_BLOCK, ATTN_W), BF16),
                        pltpu.VMEM((3 * TOK_BLOCK, ATTN_W), BF16),
                        pltpu.VMEM((TOK_BLOCK, ATTN_W), F32)],
        compiler_params=pltpu.CompilerParams(dimension_semantics=("arbitrary",),
                                             vmem_limit_bytes=VMEM_LIMIT),
        name="mixer",
    )(xa, xb, q, k, k, k, v, v, v, gb, gc, gu, gc, gu, gc, gu, bias, conv_w, g_attn, g_conv, gmat, w_out)


def _record(k, kidx, m, pick, vals, picks):
    sel = kidx == k
    return jnp.where(sel, m, vals), jnp.where(sel, pick, picks)


def _top_keys(s):
    n, t = s.shape
    half = n // 2
    r0 = lax.broadcasted_iota(jnp.int32, (half, t), 0).astype(F32)
    r1 = r0 + float(half)
    first = s[:half] >= s[half:]
    hi = jnp.where(first, s[:half], s[half:])
    lo = jnp.where(first, s[half:], s[:half])
    ihi = jnp.where(first, r0, r1)
    ilo = jnp.where(first, r1, r0)
    kidx = lax.broadcasted_iota(jnp.int32, (PEER_TOPK, t), 0)

    def body(k, carry):
        hi, lo, ihi, vals, picks = carry
        m = jnp.max(hi, axis=0, keepdims=True)
        pos = jnp.min(jnp.where(hi == m, ihi, float(n)), axis=0, keepdims=True)
        onehot = ihi == pos
        vals, picks = _record(k, kidx, m, pos, vals, picks)
        return (jnp.where(onehot, lo, hi), jnp.where(onehot, -jnp.inf, lo), jnp.where(onehot, ilo, ihi),
                vals, picks)

    zeros = jnp.zeros((PEER_TOPK, t), F32)
    out = lax.fori_loop(0, PEER_TOPK, body, (hi, lo, ihi, zeros, zeros), unroll=True)
    return out[3], out[4]


_CAND_PER_KA = [PEER_TOPK // (ka + 1) for ka in range(PEER_TOPK)]
_N_CAND = sum(_CAND_PER_KA)
_CAND_PAD = -_N_CAND % 8


def _top_candidates(v1, i1, v2, i2):
    t = v1.shape[1]
    s_rows, e_rows = [], []
    for ka, n_kb in enumerate(_CAND_PER_KA):
        s_rows.append(v1[ka:ka + 1, :] + v2[:n_kb, :])
        e_rows.append(i1[ka:ka + 1, :] * float(N_KEYS) + i2[:n_kb, :])
    s_rows.append(jnp.full((_CAND_PAD, t), -jnp.inf, F32))
    e_rows.append(jnp.zeros((_CAND_PAD, t), F32))
    s = jnp.concatenate(s_rows, axis=0)
    e = jnp.concatenate(e_rows, axis=0)
    rows = s.shape[0]
    ridx = lax.broadcasted_iota(jnp.int32, (rows, t), 0).astype(F32)
    kidx = lax.broadcasted_iota(jnp.int32, (PEER_TOPK, t), 0)

    def body(k, carry):
        s, vals, picks = carry
        m = jnp.max(s, axis=0, keepdims=True)
        pos = jnp.min(jnp.where(s == m, ridx, float(rows)), axis=0, keepdims=True)
        onehot = ridx == pos
        pick = jnp.max(jnp.where(onehot, e, -1.0), axis=0, keepdims=True)
        vals, picks = _record(k, kidx, m, pick, vals, picks)
        return jnp.where(onehot, -jnp.inf, s), vals, picks

    zeros = jnp.zeros((PEER_TOPK, t), F32)
    _, vals, picks = lax.fori_loop(0, PEER_TOPK, body, (s, zeros, zeros), unroll=True)
    return vals, picks


def _gelu_exact(x, half=0.5):
    return half * x * (1.0 + lax.erf(x * (2.0 ** -0.5)))


def _key_scores(hb, wq, k1, k2):
    half = D_KEY // 2
    qh = jnp.dot(hb, wq, preferred_element_type=F32).astype(BF16)
    return _dot_nt(k1, qh[:, :half]), _dot_nt(k2, qh[:, half:])


def _peer_kernel(xr_ref, xp_ref, g_ref, wq0_ref, wq_ref, k1_ref, k2_ref, u_ref, v_ref, o_ref,
                 h_buf, sc_buf, gt_buf, et_buf, a_buf, b_buf, gate_buf, wtok, acc):
    i = pl.program_id(0)
    c = pl.program_id(1)
    last_c = c == PEER_N_CHUNKS - 1
    route_slot = i % 2
    peer_slot = 1 - route_slot
    score_slot = c % 2
    h_peer_slot = (i + 2) % 3

    @pl.when(jnp.logical_and(i == 0, c == 0))
    def _():
        hb = _rms(xr_ref[...], g_ref[...]).astype(BF16)
        h_buf[0] = hb
        s1, s2 = _key_scores(hb, wq0_ref[...], k1_ref[...], k2_ref[...])
        sc_buf[0, 0] = s1
        sc_buf[0, 1] = s2
        h_buf[2] = jnp.zeros((PEER_TOK, D_MODEL), BF16)
        gt_buf[1] = jnp.zeros((PEER_HEADS * PEER_TOPK, PEER_TOK), F32)
        et_buf[1] = jnp.zeros((PEER_HEADS * PEER_TOPK, PEER_TOK), F32)

    @pl.when(last_c)
    def _():
        h_buf[(i + 1) % 3] = _rms(xr_ref[...], g_ref[...]).astype(BF16)

    @pl.when(c == 0)
    def _():
        acc[...] = jnp.zeros_like(acc)
        e = et_buf[peer_slot].T
        a = jnp.floor(e * (1.0 / N_KEYS))
        a_buf[...] = a
        b_buf[...] = e - a * float(N_KEYS)
        gate_buf[...] = gt_buf[peer_slot].T
        sub = lax.broadcasted_iota(jnp.int32, (N_KEYS, LANES), 0).astype(F32)

        def tok(t, carry):
            ar = a_buf[pl.ds(t, 1), :]
            br = b_buf[pl.ds(t, 1), :]
            gr = gate_buf[pl.ds(t, 1), :]
            pg = jnp.where(sub == ar, gr, 0.0).astype(BF16)
            qb = jnp.where(sub == br, 1.0, 0.0).astype(BF16)
            wtok[t] = _dot_nt(pg, qb)
            return carry

        lax.fori_loop(0, PEER_TOK, tok, 0, unroll=32)

    v1, i1 = _top_keys(sc_buf[score_slot, 0])
    v2, i2 = _top_keys(sc_buf[score_slot, 1])

    h_next_slot = jnp.where(last_c, i + 1, i) % 3
    s1, s2 = _key_scores(h_buf[h_next_slot], wq_ref[...], k1_ref[...], k2_ref[...])
    sc_buf[1 - score_slot, 0] = s1
    sc_buf[1 - score_slot, 1] = s2

    a_act = _dot_nt(h_buf[h_peer_slot], u_ref[...])
    wk = wtok[:, pl.ds(pl.multiple_of(c * PEER_CHUNK_KEYS, PEER_CHUNK_KEYS), PEER_CHUNK_KEYS), :]
    wk = jnp.swapaxes(wk, 0, 1)
    w = jnp.concatenate([wk[j] for j in range(PEER_CHUNK_KEYS)], axis=1)
    half = 0.5 + jnp.where(i1[PEER_TOPK - 1:, 0:1] + i2[PEER_TOPK - 1:, 0:1] < -1.0, 1.0, 0.0)
    wg = w * _gelu_exact(a_act, half)
    acc[...] += jnp.dot(wg.astype(BF16), v_ref[...], preferred_element_type=F32)

    top_s, top_e = _top_candidates(v1, i1, v2, i2)
    ex = jnp.exp(top_s - jnp.max(top_s, axis=0, keepdims=True))
    rows = pl.ds(pl.multiple_of(c * PEER_TOPK, PEER_TOPK), PEER_TOPK)
    gt_buf[route_slot, rows, :] = ex / jnp.sum(ex, axis=0, keepdims=True)
    et_buf[route_slot, rows, :] = top_e

    @pl.when(last_c)
    def _():
        o_ref[...] = xp_ref[...] + acc[...]


def _peer(x, g, w_query, keys1, keys2, eu, ev):
    n_blk = N_TOK // PEER_TOK
    hk = PEER_HEADS * PEER_TOPK
    route_blk = lambda i, c: (jnp.minimum(i + c // (PEER_N_CHUNKS - 1), n_blk - 1), 0)
    peer_blk = lambda i, c: (jnp.maximum(i - 1, 0), 0)
    const2 = lambda i, c: (0, 0)
    chunk = pl.BlockSpec((PEER_CHUNK, D_MODEL), lambda i, c: (c, 0))
    return pl.pallas_call(
        _peer_kernel,
        grid=(n_blk + 1, PEER_N_CHUNKS),
        in_specs=[pl.BlockSpec((PEER_TOK, D_MODEL), route_blk),
                  pl.BlockSpec((PEER_TOK, D_MODEL), peer_blk),
                  pl.BlockSpec((1, D_MODEL), const2),
                  pl.BlockSpec((D_MODEL, D_KEY), const2),
                  pl.BlockSpec((D_MODEL, D_KEY), lambda i, c: (0, (c + 1) % PEER_HEADS)),
                  pl.BlockSpec((N_KEYS, D_KEY // 2), const2),
                  pl.BlockSpec((N_KEYS, D_KEY // 2), const2),
                  chunk, chunk],
        out_specs=pl.BlockSpec((PEER_TOK, D_MODEL), peer_blk),
        out_shape=jax.ShapeDtypeStruct((N_TOK, D_MODEL), F32),
        scratch_shapes=[pltpu.VMEM((3, PEER_TOK, D_MODEL), BF16),
                        pltpu.VMEM((2, 2, N_KEYS, PEER_TOK), F32),
                        pltpu.VMEM((2, hk, PEER_TOK), F32),
                        pltpu.VMEM((2, hk, PEER_TOK), F32),
                        pltpu.VMEM((PEER_TOK, hk), F32),
                        pltpu.VMEM((PEER_TOK, hk), F32),
                        pltpu.VMEM((PEER_TOK, hk), F32),
                        pltpu.VMEM((PEER_TOK, N_KEYS, N_KEYS), F32),
                        pltpu.VMEM((PEER_TOK, D_MODEL), F32)],
        compiler_params=pltpu.CompilerParams(dimension_semantics=("arbitrary", "arbitrary"),
                                             vmem_limit_bytes=PEER_VMEM_LIMIT),
        name="peer",
    )(x, x, g, w_query, w_query, keys1, keys2, eu, ev)


def _ple_kernel(x_ref, pa_ref, pb_ref, g_ref, wg_ref, wp_ref, gf_ref, *o_refs, final):
    x = x_ref[...]
    hp = _rms(x, g_ref[...]).astype(BF16)
    gate = jax.nn.sigmoid(jnp.dot(hp, wg_ref[...], preferred_element_type=F32))
    proj = jnp.dot(_pick_part(pa_ref, pb_ref).astype(BF16), wp_ref[...], preferred_element_type=F32)
    y = x + gate * proj
    if not final:
        o_refs[0][...] = y
        return
    y = _rms(y, gf_ref[...])
    is_prompt = pl.program_id(0) < PROMPT_BLOCKS

    @pl.when(is_prompt)
    def _():
        o_refs[0][...] = y

    @pl.when(jnp.logical_not(is_prompt))
    def _():
        o_refs[1][...] = y


def _ple(x, p_prompt, p_sample, layer, g, w_gate, w_proj, g_final, final):
    const2 = lambda i: (0, 0)
    tok = pl.BlockSpec((TOK_BLOCK, D_MODEL), lambda i: (i, 0))
    p_specs = (pl.BlockSpec((None, TOK_BLOCK, PLE_DIM), lambda i: (layer, jnp.minimum(i, PROMPT_BLOCKS - 1), 0)),
               pl.BlockSpec((None, TOK_BLOCK, PLE_DIM), lambda i: (layer, jnp.maximum(i - PROMPT_BLOCKS, 0), 0)))
    if final:
        out_specs = list(_split_specs(D_MODEL, joint=False))
        out_shape = [jax.ShapeDtypeStruct((SEQ, D_MODEL), F32), jax.ShapeDtypeStruct((N_TOK - SEQ, D_MODEL), F32)]
    else:
        out_specs, out_shape = tok, jax.ShapeDtypeStruct((N_TOK, D_MODEL), F32)
    return pl.pallas_call(
        functools.partial(_ple_kernel, final=final),
        grid=(N_BLOCKS,),
        in_specs=[tok, *p_specs,
                  pl.BlockSpec((1, D_MODEL), const2),
                  pl.BlockSpec((D_MODEL, D_MODEL), const2),
                  pl.BlockSpec((PLE_DIM, D_MODEL), const2),
                  pl.BlockSpec((1, D_MODEL), const2)],
        out_specs=out_specs,
        out_shape=out_shape,
        compiler_params=pltpu.CompilerParams(dimension_semantics=("arbitrary",),
                                             vmem_limit_bytes=VMEM_LIMIT),
        name="ple",
    )(x, p_prompt, p_sample, g, w_gate, w_proj, g_final)


def _bias_table(rpb):
    cols = np.arange(GRID_W)
    col_start = np.clip(cols - WIN_COLS // 2, 0, GRID_W - WIN_COLS)
    kc = np.arange(GRID_W)
    in_win = (kc[None, :] >= col_start[:, None]) & (kc[None, :] < col_start[:, None] + WIN_COLS)
    col_off = np.clip(kc[None, :] - cols[:, None] + (WIN_COLS - 1), 0, 2 * WIN_COLS - 2)
    delta = np.arange(WIN_ROWS)
    j = np.arange(WIN_ROWS)
    row_off = j[None, :] - delta[:, None] + (WIN_ROWS - 1)
    t = rpb[:, row_off][:, :, :, col_off]
    t = jnp.where(in_win[None, None, None], t, NEG_BIG)
    t = t.transpose(1, 0, 3, 2, 4)
    return t.reshape(WIN_ROWS, ATTN_HEADS // 2, 2 * GRID_W, WIN_ROWS * GRID_W).astype(F32)


def kernel(x_prompt, x_sample, p_prompt, p_sample, g_mix, w_in, rpb, conv_w, g_attn_out, g_conv_out, w_out,
           g_ffn, w_query, sub_keys1, sub_keys2, expert_u, expert_v, g_ple, w_ple_gate, w_ple_proj, g_final):
    x = (x_prompt.reshape(SEQ, D_MODEL), x_sample.reshape(DEC_BATCH * DEC_SEQ, D_MODEL))
    pp = p_prompt.reshape(DEPTH, SEQ, PLE_DIM)
    ps = p_sample.reshape(DEPTH, DEC_BATCH * DEC_SEQ, PLE_DIM)
    group = np.arange(ATTN_W) // HEAD_DIM
    gmat = jnp.asarray(group[:, None] == group[None, :], dtype=BF16)
    row = lambda g: g.reshape(1, -1)
    for i in range(DEPTH):
        q, k, v, gb, gc, gu = _proj(x, row(g_mix[i]), w_in[i].astype(BF16))
        x = _mixer(x, q, k, v, gb, gc, gu, _bias_table(rpb[i]), conv_w[i], row(g_attn_out[i]),
                   row(g_conv_out[i]), gmat, w_out[i].astype(BF16))
        x = _peer(x, row(g_ffn[i]), w_query[i].astype(BF16), sub_keys1[i].astype(BF16),
                  sub_keys2[i].astype(BF16), expert_u[i].astype(BF16), expert_v[i].astype(BF16))
        x = _ple(x, pp, ps, i, row(g_ple[i]), w_ple_gate[i].astype(BF16), w_ple_proj[i].astype(BF16),
                 row(g_final), final=(i == DEPTH - 1))
    y_prompt, y_sample = x
    return (y_prompt.reshape(1, SEQ, D_MODEL), y_sample.reshape(DEC_BATCH, DEC_SEQ, D_MODEL))
```

```python
import functools

import numpy as np
import jax
import jax.numpy as jnp
from jax import lax
from jax.experimental import pallas as pl
from jax.experimental.pallas import tpu as pltpu

F32 = jnp.float32
BF16 = jnp.bfloat16

D_MODEL = 1024
DEPTH = 2
SEQ = 16384
DEC_BATCH = 4
DEC_SEQ = 4096
N_TOK = SEQ + DEC_BATCH * DEC_SEQ

GRID_W = 64
WIN_ROWS = 8
WIN_COLS = 16
ATTN_HEADS = 8
HEAD_DIM = 64
ATTN_W = ATTN_HEADS * HEAD_DIM
CONV_W = D_MODEL - ATTN_W
N_KEYS = 128
N_EXPERTS = N_KEYS * N_KEYS
PEER_HEADS = 8
D_KEY = 256
PEER_TOPK = 16
PLE_DIM = 256
EPS = 1e-6
NEG_BIG = -1e30

LANES = 128
ROW_BLOCK = WIN_ROWS
TOK_BLOCK = ROW_BLOCK * GRID_W
N_BLOCKS = N_TOK // TOK_BLOCK
PROMPT_ROWS = SEQ // GRID_W
SAMPLE_ROWS = DEC_SEQ // GRID_W
PROMPT_BLOCKS = PROMPT_ROWS // ROW_BLOCK
SAMPLE_BLOCKS = SAMPLE_ROWS // ROW_BLOCK
HALO = 8

PEER_TOK = 256
PEER_CHUNK_KEYS = 16
PEER_CHUNK = PEER_CHUNK_KEYS * N_KEYS
PEER_N_CHUNKS = N_EXPERTS // PEER_CHUNK
assert PEER_N_CHUNKS == PEER_HEADS

VMEM_LIMIT = 48 * 1024 * 1024
PEER_VMEM_LIMIT = 56 * 1024 * 1024


def _rms(x, g):
    return x * lax.rsqrt(jnp.mean(x * x, axis=-1, keepdims=True) + EPS) * g


def _dot_nt(a, b):
    return lax.dot_general(a, b, (((1,), (1,)), ((), ())), preferred_element_type=F32)


def _split_specs(width, joint):
    off = PROMPT_BLOCKS if joint else 0
    return (pl.BlockSpec((TOK_BLOCK, width), lambda i: (jnp.minimum(i, PROMPT_BLOCKS - 1), 0)),
            pl.BlockSpec((TOK_BLOCK, width), lambda i: (jnp.maximum(i - PROMPT_BLOCKS, 0) + off, 0)))


def _pick_part(prompt_ref, sample_ref):
    return jnp.where(pl.program_id(0) < PROMPT_BLOCKS, prompt_ref[...], sample_ref[...])


def _parts(x):
    return (x[0], x[1], False) if isinstance(x, tuple) else (x, x, True)


def _proj_kernel(xa_ref, xb_ref, g_ref, w_ref, q_ref, k_ref, v_ref, gb_ref, gc_ref, gu_ref):
    hb = _rms(_pick_part(xa_ref, xb_ref), g_ref[...]).astype(BF16)
    outs = (q_ref, k_ref, v_ref, gb_ref, gc_ref, gu_ref)
    for j, o_ref in enumerate(outs):
        z = jnp.dot(hb, w_ref[:, j * ATTN_W:(j + 1) * ATTN_W], preferred_element_type=F32)
        if j == 0:
            z = z * (HEAD_DIM ** -0.5)
        o_ref[...] = z.astype(o_ref.dtype)


def _proj(x, g, w_in):
    xa, xb, joint = _parts(x)
    tok = pl.BlockSpec((TOK_BLOCK, ATTN_W), lambda i: (i, 0))
    return pl.pallas_call(
        _proj_kernel,
        grid=(N_BLOCKS,),
        in_specs=[*_split_specs(D_MODEL, joint),
                  pl.BlockSpec((1, D_MODEL), lambda i: (0, 0)),
                  pl.BlockSpec((D_MODEL, 6 * ATTN_W), lambda i: (0, 0))],
        out_specs=[tok] * 6,
        out_shape=[jax.ShapeDtypeStruct((N_TOK, ATTN_W), BF16)] * 3
                  + [jax.ShapeDtypeStruct((N_TOK, ATTN_W), F32)] * 3,
        compiler_params=pltpu.CompilerParams(dimension_semantics=("arbitrary",),
                                             vmem_limit_bytes=VMEM_LIMIT),
        name="proj",
    )(xa, xb, g, w_in)


def _group_norm(xv, gmat, g):
    sq = xv * xv
    hi = sq.astype(BF16)
    lo = (sq - hi.astype(F32)).astype(BF16)
    ms = (jnp.dot(hi, gmat, preferred_element_type=F32)
          + jnp.dot(lo, gmat, preferred_element_type=F32)) * (1.0 / HEAD_DIM)
    return xv * lax.rsqrt(ms + EPS) * g


def _mixer_kernel(xa_ref, xb_ref, q_ref, kp_ref, kc_ref, kn_ref, vp_ref, vc_ref, vn_ref,
                  gb_ref, gc_ref, gu_ref, gcp_ref, gup_ref, gcn_ref, gun_ref,
                  bias_ref, cw_ref, ga_ref, gcv_ref, gmat_ref, wo_ref,
                  o_ref, kbuf, vbuf, abuf):
    i = pl.program_id(0)
    is_prompt = i < PROMPT_BLOCKS
    sample = (i - PROMPT_BLOCKS) // SAMPLE_BLOCKS
    seq_r0 = jnp.where(is_prompt, 0, PROMPT_ROWS + SAMPLE_ROWS * sample)
    seq_r1 = jnp.where(is_prompt, PROMPT_ROWS, seq_r0 + SAMPLE_ROWS)
    blk_r0 = ROW_BLOCK * i

    kbuf[0:TOK_BLOCK, :] = kp_ref[...]
    kbuf[TOK_BLOCK:2 * TOK_BLOCK, :] = kc_ref[...]
    kbuf[2 * TOK_BLOCK:3 * TOK_BLOCK, :] = kn_ref[...]
    vbuf[0:TOK_BLOCK, :] = vp_ref[...]
    vbuf[TOK_BLOCK:2 * TOK_BLOCK, :] = vc_ref[...]
    vbuf[2 * TOK_BLOCK:3 * TOK_BLOCK, :] = vn_ref[...]

    lane = lax.broadcasted_iota(jnp.int32, (GRID_W, LANES), 1)
    low_half = lane < HEAD_DIM
    n_win = WIN_ROWS * GRID_W
    for jr in range(ROW_BLOCK):
        r = blk_r0 + jr
        rs = jnp.clip(r - WIN_ROWS // 2, seq_r0, seq_r1 - WIN_ROWS)
        delta = r - rs
        start = pl.multiple_of((rs - blk_r0 + ROW_BLOCK) * GRID_W, GRID_W)
        pairs = range(ATTN_HEADS // 2)
        col = [slice(p * LANES, (p + 1) * LANES) for p in pairs]
        scores = []
        for p in pairs:
            qp = q_ref[jr * GRID_W:(jr + 1) * GRID_W, col[p]]
            zero = jnp.zeros_like(qp)
            q2 = jnp.concatenate([jnp.where(low_half, qp, zero), jnp.where(low_half, zero, qp)], axis=0)
            scores.append(_dot_nt(q2, kbuf[pl.ds(start, n_win), col[p]]) + bias_ref[delta, p])
        probs, norms = [], []
        for s in scores:
            e = jnp.exp(s - jnp.max(s, axis=-1, keepdims=True))
            probs.append(e.astype(BF16))
            norms.append(jnp.sum(e, axis=-1, keepdims=True))
        for p in pairs:
            o2 = jnp.dot(probs[p], vbuf[pl.ds(start, n_win), col[p]], preferred_element_type=F32) / norms[p]
            abuf[jr * GRID_W:(jr + 1) * GRID_W, col[p]] = jnp.where(low_half, o2[:GRID_W], o2[GRID_W:])

    first = blk_r0 == seq_r0
    last = blk_r0 + ROW_BLOCK == seq_r1
    cu = gc_ref[...] * gu_ref[...]
    prev_row = jnp.where(first, 0.0, gcp_ref[HALO - 1:HALO, :] * gup_ref[HALO - 1:HALO, :])
    next_row = jnp.where(last, 0.0, gcn_ref[0:1, :] * gun_ref[0:1, :])
    row = lax.broadcasted_iota(jnp.int32, (TOK_BLOCK, CONV_W), 0)
    up_prev = jnp.where(row == 0, prev_row, pltpu.roll(cu, 1, axis=0))
    up_next = jnp.where(row == TOK_BLOCK - 1, next_row, pltpu.roll(cu, TOK_BLOCK - 1, axis=0))
    conv = gb_ref[...] * (up_prev * cw_ref[0:1, :] + cu * cw_ref[1:2, :] + up_next * cw_ref[2:3, :])

    gmat = gmat_ref[...]
    attn_n = _group_norm(abuf[...], gmat, ga_ref[...]).astype(BF16)
    conv_n = _group_norm(conv, gmat, gcv_ref[...]).astype(BF16)
    y = (jnp.dot(attn_n, wo_ref[0:ATTN_W, :], preferred_element_type=F32)
         + jnp.dot(conv_n, wo_ref[ATTN_W:D_MODEL, :], preferred_element_type=F32))
    o_ref[...] = _pick_part(xa_ref, xb_ref) + y


def _mixer(x, q, k, v, gb, gc, gu, bias, layer, conv_w, g_attn, g_conv, gmat, w_out):
    xa, xb, joint = _parts(x)
    cur = lambda i: (i, 0)
    prev = lambda i: (jnp.maximum(i - 1, 0), 0)
    nxt = lambda i: (jnp.minimum(i + 1, N_BLOCKS - 1), 0)
    halo_per_block = TOK_BLOCK // HALO
    hprev = lambda i: (jnp.maximum(i * halo_per_block - 1, 0), 0)
    hnext = lambda i: (jnp.minimum((i + 1) * halo_per_block, N_TOK // HALO - 1), 0)
    const2 = lambda i: (0, 0)
    blk = lambda m: pl.BlockSpec((TOK_BLOCK, ATTN_W), m)
    halo = lambda m: pl.BlockSpec((HALO, CONV_W), m)
    return pl.pallas_call(
        _mixer_kernel,
        grid=(N_BLOCKS,),
        in_specs=[*_split_specs(D_MODEL, joint),
                  blk(cur), blk(prev), blk(cur), blk(nxt), blk(prev), blk(cur), blk(nxt),
                  blk(cur), blk(cur), blk(cur), halo(hprev), halo(hprev), halo(hnext), halo(hnext),
                  pl.BlockSpec((None, WIN_ROWS, ATTN_HEADS // 2, 2 * GRID_W, WIN_ROWS * GRID_W),
                               lambda i: (layer, 0, 0, 0, 0)),
                  pl.BlockSpec((3, CONV_W), const2),
                  pl.BlockSpec((1, ATTN_W), const2),
                  pl.BlockSpec((1, CONV_W), const2),
                  pl.BlockSpec((ATTN_W, ATTN_W), const2),
                  pl.BlockSpec((D_MODEL, D_MODEL), const2)],
        out_specs=pl.BlockSpec((TOK_BLOCK, D_MODEL), cur),
        out_shape=jax.ShapeDtypeStruct((N_TOK, D_MODEL), F32),
        scratch_shapes=[pltpu.VMEM((3 * TOK_BLOCK, ATTN_W), BF16),
                        pltpu.VMEM((3 * TOK_BLOCK, ATTN_W), BF16),
                        pltpu.VMEM((TOK_BLOCK, ATTN_W), F32)],
        compiler_params=pltpu.CompilerParams(dimension_semantics=("arbitrary",),
                                             vmem_limit_bytes=VMEM_LIMIT),
        name="mixer",
    )(xa, xb, q, k, k, k, v, v, v, gb, gc, gu, gc, gu, gc, gu, bias, conv_w, g_attn, g_conv, gmat, w_out)


def _record(k, kidx, m, pick, vals, picks):
    sel = kidx == k
    return jnp.where(sel, m, vals), jnp.where(sel, pick, picks)


def _top_keys(s):
    n, t = s.shape
    half = n // 2
    r0 = lax.broadcasted_iota(jnp.int32, (half, t), 0).astype(F32)
    r1 = r0 + float(half)
    first = s[:half] >= s[half:]
    hi = jnp.where(first, s[:half], s[half:])
    lo = jnp.where(first, s[half:], s[:half])
    ihi = jnp.where(first, r0, r1)
    ilo = jnp.where(first, r1, r0)
    kidx = lax.broadcasted_iota(jnp.int32, (PEER_TOPK, t), 0)

    def body(k, carry):
        hi, lo, ihi, vals, picks = carry
        m = jnp.max(hi, axis=0, keepdims=True)
        pos = jnp.min(jnp.where(hi == m, ihi, float(n)), axis=0, keepdims=True)
        onehot = ihi == pos
        vals, picks = _record(k, kidx, m, pos, vals, picks)
        return (jnp.where(onehot, lo, hi), jnp.where(onehot, -jnp.inf, lo), jnp.where(onehot, ilo, ihi),
                vals, picks)

    zeros = jnp.zeros((PEER_TOPK, t), F32)
    out = lax.fori_loop(0, PEER_TOPK, body, (hi, lo, ihi, zeros, zeros), unroll=True)
    return out[3], out[4]


_CAND_PER_KA = [PEER_TOPK // (ka + 1) for ka in range(PEER_TOPK)]
_N_CAND = sum(_CAND_PER_KA)
_CAND_PAD = -_N_CAND % 8


def _top_candidates(v1, i1, v2, i2):
    t = v1.shape[1]
    s_rows, e_rows = [], []
    for ka, n_kb in enumerate(_CAND_PER_KA):
        s_rows.append(v1[ka:ka + 1, :] + v2[:n_kb, :])
        e_rows.append(i1[ka:ka + 1, :] * float(N_KEYS) + i2[:n_kb, :])
    s_rows.append(jnp.full((_CAND_PAD, t), -jnp.inf, F32))
    e_rows.append(jnp.zeros((_CAND_PAD, t), F32))
    s = jnp.concatenate(s_rows, axis=0)
    e = jnp.concatenate(e_rows, axis=0)
    rows = s.shape[0]
    ridx = lax.broadcasted_iota(jnp.int32, (rows, t), 0).astype(F32)
    kidx = lax.broadcasted_iota(jnp.int32, (PEER_TOPK, t), 0)

    def body(k, carry):
        s, vals, picks = carry
        m = jnp.max(s, axis=0, keepdims=True)
        pos = jnp.min(jnp.where(s == m, ridx, float(rows)), axis=0, keepdims=True)
        onehot = ridx == pos
        pick = jnp.max(jnp.where(onehot, e, -1.0), axis=0, keepdims=True)
        vals, picks = _record(k, kidx, m, pick, vals, picks)
        return jnp.where(onehot, -jnp.inf, s), vals, picks

    zeros = jnp.zeros((PEER_TOPK, t), F32)
    _, vals, picks = lax.fori_loop(0, PEER_TOPK, body, (s, zeros, zeros), unroll=True)
    return vals, picks


def _gelu_exact(x, half=0.5):
    return half * x * (1.0 + lax.erf(x * (2.0 ** -0.5)))


def _key_scores(hb, wq, k1, k2):
    half = D_KEY // 2
    qh = jnp.dot(hb, wq, preferred_element_type=F32).astype(BF16)
    return _dot_nt(k1, qh[:, :half]), _dot_nt(k2, qh[:, half:])


def _peer_kernel(xr_ref, xp_ref, g_ref, wq0_ref, wq_ref, k1_ref, k2_ref, u_ref, v_ref, o_ref,
                 h_buf, sc_buf, gt_buf, et_buf, a_buf, b_buf, gate_buf, wtok, acc):
    i = pl.program_id(0)
    c = pl.program_id(1)
    last_c = c == PEER_N_CHUNKS - 1
    route_slot = i % 2
    peer_slot = 1 - route_slot
    score_slot = c % 2
    h_peer_slot = (i + 2) % 3

    @pl.when(jnp.logical_and(i == 0, c == 0))
    def _():
        hb = _rms(xr_ref[...], g_ref[...]).astype(BF16)
        h_buf[0] = hb
        s1, s2 = _key_scores(hb, wq0_ref[...], k1_ref[...], k2_ref[...])
        sc_buf[0, 0] = s1
        sc_buf[0, 1] = s2
        h_buf[2] = jnp.zeros((PEER_TOK, D_MODEL), BF16)
        gt_buf[1] = jnp.zeros((PEER_HEADS * PEER_TOPK, PEER_TOK), F32)
        et_buf[1] = jnp.zeros((PEER_HEADS * PEER_TOPK, PEER_TOK), F32)

    @pl.when(last_c)
    def _():
        h_buf[(i + 1) % 3] = _rms(xr_ref[...], g_ref[...]).astype(BF16)

    @pl.when(c == 0)
    def _():
        acc[...] = jnp.zeros_like(acc)
        e = et_buf[peer_slot].T
        a = jnp.floor(e * (1.0 / N_KEYS))
        a_buf[...] = a
        b_buf[...] = e - a * float(N_KEYS)
        gate_buf[...] = gt_buf[peer_slot].T
        sub = lax.broadcasted_iota(jnp.int32, (N_KEYS, LANES), 0).astype(F32)

        def tok(t, carry):
            ar = a_buf[pl.ds(t, 1), :]
            br = b_buf[pl.ds(t, 1), :]
            gr = gate_buf[pl.ds(t, 1), :]
            pg = jnp.where(sub == ar, gr, 0.0).astype(BF16)
            qb = jnp.where(sub == br, 1.0, 0.0).astype(BF16)
            wtok[t] = _dot_nt(pg, qb)
            return carry

        lax.fori_loop(0, PEER_TOK, tok, 0, unroll=32)

    v1, i1 = _top_keys(sc_buf[score_slot, 0])
    v2, i2 = _top_keys(sc_buf[score_slot, 1])

    h_next_slot = jnp.where(last_c, i + 1, i) % 3
    s1, s2 = _key_scores(h_buf[h_next_slot], wq_ref[...], k1_ref[...], k2_ref[...])
    sc_buf[1 - score_slot, 0] = s1
    sc_buf[1 - score_slot, 1] = s2

    a_act = _dot_nt(h_buf[h_peer_slot], u_ref[...])
    wk = wtok[:, pl.ds(pl.multiple_of(c * PEER_CHUNK_KEYS, PEER_CHUNK_KEYS), PEER_CHUNK_KEYS), :]
    wk = jnp.swapaxes(wk, 0, 1)
    w = jnp.concatenate([wk[j] for j in range(PEER_CHUNK_KEYS)], axis=1)
    half = 0.5 + jnp.where(i1[PEER_TOPK - 1:, 0:1] + i2[PEER_TOPK - 1:, 0:1] < -1.0, 1.0, 0.0)
    wg = w * _gelu_exact(a_act, half)
    acc[...] += jnp.dot(wg.astype(BF16), v_ref[...], preferred_element_type=F32)

    top_s, top_e = _top_candidates(v1, i1, v2, i2)
    ex = jnp.exp(top_s - jnp.max(top_s, axis=0, keepdims=True))
    rows = pl.ds(pl.multiple_of(c * PEER_TOPK, PEER_TOPK), PEER_TOPK)
    gt_buf[route_slot, rows, :] = ex / jnp.sum(ex, axis=0, keepdims=True)
    et_buf[route_slot, rows, :] = top_e

    @pl.when(last_c)
    def _():
        o_ref[...] = xp_ref[...] + acc[...]


def _peer(x, g, w_query, keys1, keys2, eu, ev, layer):
    n_blk = N_TOK // PEER_TOK
    hk = PEER_HEADS * PEER_TOPK
    route_blk = lambda i, c: (jnp.minimum(i + c // (PEER_N_CHUNKS - 1), n_blk - 1), 0)
    peer_blk = lambda i, c: (jnp.maximum(i - 1, 0), 0)
    const2 = lambda i, c: (0, 0)
    chunk = pl.BlockSpec((None, PEER_CHUNK, D_MODEL), lambda i, c: (layer, c, 0))
    return pl.pallas_call(
        _peer_kernel,
        grid=(n_blk + 1, PEER_N_CHUNKS),
        in_specs=[pl.BlockSpec((PEER_TOK, D_MODEL), route_blk),
                  pl.BlockSpec((PEER_TOK, D_MODEL), peer_blk),
                  pl.BlockSpec((1, D_MODEL), const2),
                  pl.BlockSpec((D_MODEL, D_KEY), const2),
                  pl.BlockSpec((D_MODEL, D_KEY), lambda i, c: (0, (c + 1) % PEER_HEADS)),
                  pl.BlockSpec((N_KEYS, D_KEY // 2), const2),
                  pl.BlockSpec((N_KEYS, D_KEY // 2), const2),
                  chunk, chunk],
        out_specs=pl.BlockSpec((PEER_TOK, D_MODEL), peer_blk),
        out_shape=jax.ShapeDtypeStruct((N_TOK, D_MODEL), F32),
        scratch_shapes=[pltpu.VMEM((3, PEER_TOK, D_MODEL), BF16),
                        pltpu.VMEM((2, 2, N_KEYS, PEER_TOK), F32),
                        pltpu.VMEM((2, hk, PEER_TOK), F32),
                        pltpu.VMEM((2, hk, PEER_TOK), F32),
                        pltpu.VMEM((PEER_TOK, hk), F32),
                        pltpu.VMEM((PEER_TOK, hk), F32),
                        pltpu.VMEM((PEER_TOK, hk), F32),
                        pltpu.VMEM((PEER_TOK, N_KEYS, N_KEYS), F32),
                        pltpu.VMEM((PEER_TOK, D_MODEL), F32)],
        compiler_params=pltpu.CompilerParams(dimension_semantics=("arbitrary", "arbitrary"),
                                             vmem_limit_bytes=PEER_VMEM_LIMIT),
        name="peer",
    )(x, x, g, w_query, w_query, keys1, keys2, eu, ev)


def _ple_kernel(x_ref, pa_ref, pb_ref, g_ref, wg_ref, wp_ref, gf_ref, *o_refs, final):
    x = x_ref[...]
    hp = _rms(x, g_ref[...]).astype(BF16)
    gate = jax.nn.sigmoid(jnp.dot(hp, wg_ref[...], preferred_element_type=F32))
    proj = jnp.dot(_pick_part(pa_ref, pb_ref).astype(BF16), wp_ref[...], preferred_element_type=F32)
    y = x + gate * proj
    if not final:
        o_refs[0][...] = y
        return
    y = _rms(y, gf_ref[...])
    is_prompt = pl.program_id(0) < PROMPT_BLOCKS

    @pl.when(is_prompt)
    def _():
        o_refs[0][...] = y

    @pl.when(jnp.logical_not(is_prompt))
    def _():
        o_refs[1][...] = y


def _ple(x, p_prompt, p_sample, layer, g, w_gate, w_proj, g_final, final):
    const2 = lambda i: (0, 0)
    tok = pl.BlockSpec((TOK_BLOCK, D_MODEL), lambda i: (i, 0))
    p_specs = (pl.BlockSpec((None, TOK_BLOCK, PLE_DIM), lambda i: (layer, jnp.minimum(i, PROMPT_BLOCKS - 1), 0)),
               pl.BlockSpec((None, TOK_BLOCK, PLE_DIM), lambda i: (layer, jnp.maximum(i - PROMPT_BLOCKS, 0), 0)))
    if final:
        out_specs = list(_split_specs(D_MODEL, joint=False))
        out_shape = [jax.ShapeDtypeStruct((SEQ, D_MODEL), F32), jax.ShapeDtypeStruct((N_TOK - SEQ, D_MODEL), F32)]
    else:
        out_specs, out_shape = tok, jax.ShapeDtypeStruct((N_TOK, D_MODEL), F32)
    return pl.pallas_call(
        functools.partial(_ple_kernel, final=final),
        grid=(N_BLOCKS,),
        in_specs=[tok, *p_specs,
                  pl.BlockSpec((1, D_MODEL), const2),
                  pl.BlockSpec((D_MODEL, D_MODEL), const2),
                  pl.BlockSpec((PLE_DIM, D_MODEL), const2),
                  pl.BlockSpec((1, D_MODEL), const2)],
        out_specs=out_specs,
        out_shape=out_shape,
        compiler_params=pltpu.CompilerParams(dimension_semantics=("arbitrary",),
                                             vmem_limit_bytes=VMEM_LIMIT),
        name="ple",
    )(x, p_prompt, p_sample, g, w_gate, w_proj, g_final)


def _bias_tables(rpb):
    cols = np.arange(GRID_W)
    col_start = np.clip(cols - WIN_COLS // 2, 0, GRID_W - WIN_COLS)
    kc = np.arange(GRID_W)
    in_win = (kc[None, :] >= col_start[:, None]) & (kc[None, :] < col_start[:, None] + WIN_COLS)
    col_off = kc[None, :] - cols[:, None] + (WIN_COLS - 1)
    onehot = (col_off[None] == np.arange(2 * WIN_COLS - 1)[:, None, None]) & in_win[None]
    delta = np.arange(WIN_ROWS)
    j = np.arange(WIN_ROWS)
    row_off = j[None, :] - delta[:, None] + (WIN_ROWS - 1)
    rows = rpb[:, :, row_off]
    t = jnp.einsum('lhdjo,ock->ldhcjk', rows, jnp.asarray(onehot, F32), precision=lax.Precision.HIGHEST)
    t = t + jnp.asarray(np.where(in_win, 0.0, NEG_BIG), F32)[None, None, None, :, None, :]
    return t.reshape(DEPTH, WIN_ROWS, ATTN_HEADS // 2, 2 * GRID_W, WIN_ROWS * GRID_W)


def kernel(x_prompt, x_sample, p_prompt, p_sample, g_mix, w_in, rpb, conv_w, g_attn_out, g_conv_out, w_out,
           g_ffn, w_query, sub_keys1, sub_keys2, expert_u, expert_v, g_ple, w_ple_gate, w_ple_proj, g_final):
    x = (x_prompt.reshape(SEQ, D_MODEL), x_sample.reshape(DEC_BATCH * DEC_SEQ, D_MODEL))
    pp = p_prompt.reshape(DEPTH, SEQ, PLE_DIM)
    ps = p_sample.reshape(DEPTH, DEC_BATCH * DEC_SEQ, PLE_DIM)
    group = np.arange(ATTN_W) // HEAD_DIM
    gmat = jnp.asarray(group[:, None] == group[None, :], dtype=BF16)
    row = lambda g: g.reshape(1, -1)
    bias = _bias_tables(rpb)
    eu, ev = expert_u.astype(BF16), expert_v.astype(BF16)
    for i in range(DEPTH):
        q, k, v, gb, gc, gu = _proj(x, row(g_mix[i]), w_in[i].astype(BF16))
        x = _mixer(x, q, k, v, gb, gc, gu, bias, i, conv_w[i], row(g_attn_out[i]),
                   row(g_conv_out[i]), gmat, w_out[i].astype(BF16))
        x = _peer(x, row(g_ffn[i]), w_query[i].astype(BF16), sub_keys1[i].astype(BF16),
                  sub_keys2[i].astype(BF16), eu, ev, i)
        x = _ple(x, pp, ps, i, row(g_ple[i]), w_ple_gate[i].astype(BF16), w_ple_proj[i].astype(BF16),
                 row(g_final), final=(i == DEPTH - 1))
    y_prompt, y_sample = x
    return (y_prompt.reshape(1, SEQ, D_MODEL), y_sample.reshape(DEC_BATCH, DEC_SEQ, D_MODEL))
```

```python
import functools

import numpy as np
import jax
import jax.numpy as jnp
from jax import lax
from jax.experimental import pallas as pl
from jax.experimental.pallas import tpu as pltpu

F32 = jnp.float32
BF16 = jnp.bfloat16

D_MODEL = 1024
DEPTH = 2
SEQ = 16384
DEC_BATCH = 4
DEC_SEQ = 4096
N_TOK = SEQ + DEC_BATCH * DEC_SEQ

GRID_W = 64
WIN_ROWS = 8
WIN_COLS = 16
ATTN_HEADS = 8
HEAD_DIM = 64
ATTN_W = ATTN_HEADS * HEAD_DIM
CONV_W = D_MODEL - ATTN_W
N_KEYS = 128
N_EXPERTS = N_KEYS * N_KEYS
PEER_HEADS = 8
D_KEY = 256
PEER_TOPK = 16
PLE_DIM = 256
EPS = 1e-6
NEG_BIG = -1e30

LANES = 128
ROW_BLOCK = WIN_ROWS
TOK_BLOCK = ROW_BLOCK * GRID_W
N_BLOCKS = N_TOK // TOK_BLOCK
PROMPT_ROWS = SEQ // GRID_W
SAMPLE_ROWS = DEC_SEQ // GRID_W
PROMPT_BLOCKS = PROMPT_ROWS // ROW_BLOCK
SAMPLE_BLOCKS = SAMPLE_ROWS // ROW_BLOCK
HALO = 8

PEER_TOK = 256
TOK_GROUP = 8
PEER_CHUNK_KEYS = 16
PEER_CHUNK = PEER_CHUNK_KEYS * N_KEYS
PEER_N_CHUNKS = N_EXPERTS // PEER_CHUNK
assert PEER_N_CHUNKS == PEER_HEADS

VMEM_LIMIT = 48 * 1024 * 1024
PEER_VMEM_LIMIT = 56 * 1024 * 1024


def _rms(x, g):
    return x * lax.rsqrt(jnp.mean(x * x, axis=-1, keepdims=True) + EPS) * g


def _dot_nt(a, b):
    return lax.dot_general(a, b, (((1,), (1,)), ((), ())), preferred_element_type=F32)


def _split_specs(width, joint):
    off = PROMPT_BLOCKS if joint else 0
    return (pl.BlockSpec((TOK_BLOCK, width), lambda i: (jnp.minimum(i, PROMPT_BLOCKS - 1), 0)),
            pl.BlockSpec((TOK_BLOCK, width), lambda i: (jnp.maximum(i - PROMPT_BLOCKS, 0) + off, 0)))


def _pick_part(prompt_ref, sample_ref):
    return jnp.where(pl.program_id(0) < PROMPT_BLOCKS, prompt_ref[...], sample_ref[...])


def _parts(x):
    return (x[0], x[1], False) if isinstance(x, tuple) else (x, x, True)


def _proj_kernel(xa_ref, xb_ref, g_ref, w_ref, q_ref, k_ref, v_ref, gb_ref, gc_ref, gu_ref):
    hb = _rms(_pick_part(xa_ref, xb_ref), g_ref[...]).astype(BF16)
    outs = (q_ref, k_ref, v_ref, gb_ref, gc_ref, gu_ref)
    for j, o_ref in enumerate(outs):
        z = jnp.dot(hb, w_ref[:, j * ATTN_W:(j + 1) * ATTN_W], preferred_element_type=F32)
        if j == 0:
            z = z * (HEAD_DIM ** -0.5)
        o_ref[...] = z.astype(o_ref.dtype)


def _proj(x, g, w_in):
    xa, xb, joint = _parts(x)
    tok = pl.BlockSpec((TOK_BLOCK, ATTN_W), lambda i: (i, 0))
    return pl.pallas_call(
        _proj_kernel,
        grid=(N_BLOCKS,),
        in_specs=[*_split_specs(D_MODEL, joint),
                  pl.BlockSpec((1, D_MODEL), lambda i: (0, 0)),
                  pl.BlockSpec((D_MODEL, 6 * ATTN_W), lambda i: (0, 0))],
        out_specs=[tok] * 6,
        out_shape=[jax.ShapeDtypeStruct((N_TOK, ATTN_W), BF16)] * 3
                  + [jax.ShapeDtypeStruct((N_TOK, ATTN_W), F32)] * 3,
        compiler_params=pltpu.CompilerParams(dimension_semantics=("arbitrary",),
                                             vmem_limit_bytes=VMEM_LIMIT),
        name="proj",
    )(xa, xb, g, w_in)


def _group_norm(xv, gmat, g):
    sq = xv * xv
    hi = sq.astype(BF16)
    lo = (sq - hi.astype(F32)).astype(BF16)
    ms = (jnp.dot(hi, gmat, preferred_element_type=F32)
          + jnp.dot(lo, gmat, preferred_element_type=F32)) * (1.0 / HEAD_DIM)
    return xv * lax.rsqrt(ms + EPS) * g


def _mixer_kernel(xa_ref, xb_ref, q_ref, kp_ref, kc_ref, kn_ref, vp_ref, vc_ref, vn_ref,
                  gb_ref, gc_ref, gu_ref, gcp_ref, gup_ref, gcn_ref, gun_ref,
                  bias_ref, cw_ref, ga_ref, gcv_ref, gmat_ref, wo_ref,
                  o_ref, kbuf, vbuf, abuf):
    i = pl.program_id(0)
    is_prompt = i < PROMPT_BLOCKS
    sample = (i - PROMPT_BLOCKS) // SAMPLE_BLOCKS
    seq_r0 = jnp.where(is_prompt, 0, PROMPT_ROWS + SAMPLE_ROWS * sample)
    seq_r1 = jnp.where(is_prompt, PROMPT_ROWS, seq_r0 + SAMPLE_ROWS)
    blk_r0 = ROW_BLOCK * i

    kbuf[0:TOK_BLOCK, :] = kp_ref[...]
    kbuf[TOK_BLOCK:2 * TOK_BLOCK, :] = kc_ref[...]
    kbuf[2 * TOK_BLOCK:3 * TOK_BLOCK, :] = kn_ref[...]
    vbuf[0:TOK_BLOCK, :] = vp_ref[...]
    vbuf[TOK_BLOCK:2 * TOK_BLOCK, :] = vc_ref[...]
    vbuf[2 * TOK_BLOCK:3 * TOK_BLOCK, :] = vn_ref[...]

    lane = lax.broadcasted_iota(jnp.int32, (GRID_W, LANES), 1)
    low_half = lane < HEAD_DIM
    n_win = WIN_ROWS * GRID_W
    for jr in range(ROW_BLOCK):
        r = blk_r0 + jr
        rs = jnp.clip(r - WIN_ROWS // 2, seq_r0, seq_r1 - WIN_ROWS)
        delta = r - rs
        start = pl.multiple_of((rs - blk_r0 + ROW_BLOCK) * GRID_W, GRID_W)
        pairs = range(ATTN_HEADS // 2)
        col = [slice(p * LANES, (p + 1) * LANES) for p in pairs]
        scores = []
        for p in pairs:
            qp = q_ref[jr * GRID_W:(jr + 1) * GRID_W, col[p]]
            zero = jnp.zeros_like(qp)
            q2 = jnp.concatenate([jnp.where(low_half, qp, zero), jnp.where(low_half, zero, qp)], axis=0)
            scores.append(_dot_nt(q2, kbuf[pl.ds(start, n_win), col[p]]) + bias_ref[delta, p])
        probs, norms = [], []
        for s in scores:
            e = jnp.exp(s - jnp.max(s, axis=-1, keepdims=True))
            probs.append(e.astype(BF16))
            norms.append(jnp.sum(e, axis=-1, keepdims=True))
        for p in pairs:
            o2 = jnp.dot(probs[p], vbuf[pl.ds(start, n_win), col[p]], preferred_element_type=F32) / norms[p]
            abuf[jr * GRID_W:(jr + 1) * GRID_W, col[p]] = jnp.where(low_half, o2[:GRID_W], o2[GRID_W:])

    first = blk_r0 == seq_r0
    last = blk_r0 + ROW_BLOCK == seq_r1
    cu = gc_ref[...] * gu_ref[...]
    prev_row = jnp.where(first, 0.0, gcp_ref[HALO - 1:HALO, :] * gup_ref[HALO - 1:HALO, :])
    next_row = jnp.where(last, 0.0, gcn_ref[0:1, :] * gun_ref[0:1, :])
    row = lax.broadcasted_iota(jnp.int32, (TOK_BLOCK, CONV_W), 0)
    up_prev = jnp.where(row == 0, prev_row, pltpu.roll(cu, 1, axis=0))
    up_next = jnp.where(row == TOK_BLOCK - 1, next_row, pltpu.roll(cu, TOK_BLOCK - 1, axis=0))
    conv = gb_ref[...] * (up_prev * cw_ref[0:1, :] + cu * cw_ref[1:2, :] + up_next * cw_ref[2:3, :])

    gmat = gmat_ref[...]
    attn_n = _group_norm(abuf[...], gmat, ga_ref[...]).astype(BF16)
    conv_n = _group_norm(conv, gmat, gcv_ref[...]).astype(BF16)
    y = (jnp.dot(attn_n, wo_ref[0:ATTN_W, :], preferred_element_type=F32)
         + jnp.dot(conv_n, wo_ref[ATTN_W:D_MODEL, :], preferred_element_type=F32))
    o_ref[...] = _pick_part(xa_ref, xb_ref) + y


def _mixer(x, q, k, v, gb, gc, gu, bias, layer, conv_w, g_attn, g_conv, gmat, w_out):
    xa, xb, joint = _parts(x)
    cur = lambda i: (i, 0)
    prev = lambda i: (jnp.maximum(i - 1, 0), 0)
    nxt = lambda i: (jnp.minimum(i + 1, N_BLOCKS - 1), 0)
    halo_per_block = TOK_BLOCK // HALO
    hprev = lambda i: (jnp.maximum(i * halo_per_block - 1, 0), 0)
    hnext = lambda i: (jnp.minimum((i + 1) * halo_per_block, N_TOK // HALO - 1), 0)
    const2 = lambda i: (0, 0)
    blk = lambda m: pl.BlockSpec((TOK_BLOCK, ATTN_W), m)
    halo = lambda m: pl.BlockSpec((HALO, CONV_W), m)
    return pl.pallas_call(
        _mixer_kernel,
        grid=(N_BLOCKS,),
        in_specs=[*_split_specs(D_MODEL, joint),
                  blk(cur), blk(prev), blk(cur), blk(nxt), blk(prev), blk(cur), blk(nxt),
                  blk(cur), blk(cur), blk(cur), halo(hprev), halo(hprev), halo(hnext), halo(hnext),
                  pl.BlockSpec((None, WIN_ROWS, ATTN_HEADS // 2, 2 * GRID_W, WIN_ROWS * GRID_W),
                               lambda i: (layer, 0, 0, 0, 0)),
                  pl.BlockSpec((3, CONV_W), const2),
                  pl.BlockSpec((1, ATTN_W), const2),
                  pl.BlockSpec((1, CONV_W), const2),
                  pl.BlockSpec((ATTN_W, ATTN_W), const2),
                  pl.BlockSpec((D_MODEL, D_MODEL), const2)],
        out_specs=pl.BlockSpec((TOK_BLOCK, D_MODEL), cur),
        out_shape=jax.ShapeDtypeStruct((N_TOK, D_MODEL), F32),
        scratch_shapes=[pltpu.VMEM((3 * TOK_BLOCK, ATTN_W), BF16),
                        pltpu.VMEM((3 * TOK_BLOCK, ATTN_W), BF16),
                        pltpu.VMEM((TOK_BLOCK, ATTN_W), F32)],
        compiler_params=pltpu.CompilerParams(dimension_semantics=("arbitrary",),
                                             vmem_limit_bytes=VMEM_LIMIT),
        name="mixer",
    )(xa, xb, q, k, k, k, v, v, v, gb, gc, gu, gc, gu, gc, gu, bias, conv_w, g_attn, g_conv, gmat, w_out)


def _record(k, kidx, m, pick, vals, picks):
    sel = kidx == k
    return jnp.where(sel, m, vals), jnp.where(sel, pick, picks)


def _top_keys(s):
    n, t = s.shape
    half = n // 2
    r0 = lax.broadcasted_iota(jnp.int32, (half, t), 0).astype(F32)
    r1 = r0 + float(half)
    first = s[:half] >= s[half:]
    hi = jnp.where(first, s[:half], s[half:])
    lo = jnp.where(first, s[half:], s[:half])
    ihi = jnp.where(first, r0, r1)
    ilo = jnp.where(first, r1, r0)
    kidx = lax.broadcasted_iota(jnp.int32, (PEER_TOPK, t), 0)

    def body(k, carry):
        hi, lo, ihi, vals, picks = carry
        m = jnp.max(hi, axis=0, keepdims=True)
        pos = jnp.min(jnp.where(hi == m, ihi, float(n)), axis=0, keepdims=True)
        onehot = ihi == pos
        vals, picks = _record(k, kidx, m, pos, vals, picks)
        return (jnp.where(onehot, lo, hi), jnp.where(onehot, -jnp.inf, lo), jnp.where(onehot, ilo, ihi),
                vals, picks)

    zeros = jnp.zeros((PEER_TOPK, t), F32)
    out = lax.fori_loop(0, PEER_TOPK, body, (hi, lo, ihi, zeros, zeros), unroll=True)
    return out[3], out[4]


_CAND_PER_KA = [PEER_TOPK // (ka + 1) for ka in range(PEER_TOPK)]
_N_CAND = sum(_CAND_PER_KA)
_CAND_PAD = -_N_CAND % 8


def _top_candidates(v1, i1, v2, i2):
    t = v1.shape[1]
    s_rows, e_rows = [], []
    for ka, n_kb in enumerate(_CAND_PER_KA):
        s_rows.append(v1[ka:ka + 1, :] + v2[:n_kb, :])
        e_rows.append(i1[ka:ka + 1, :] * float(N_KEYS) + i2[:n_kb, :])
    s_rows.append(jnp.full((_CAND_PAD, t), -jnp.inf, F32))
    e_rows.append(jnp.zeros((_CAND_PAD, t), F32))
    s = jnp.concatenate(s_rows, axis=0)
    e = jnp.concatenate(e_rows, axis=0)
    rows = s.shape[0]
    ridx = lax.broadcasted_iota(jnp.int32, (rows, t), 0).astype(F32)
    kidx = lax.broadcasted_iota(jnp.int32, (PEER_TOPK, t), 0)

    def body(k, carry):
        s, vals, picks = carry
        m = jnp.max(s, axis=0, keepdims=True)
        pos = jnp.min(jnp.where(s == m, ridx, float(rows)), axis=0, keepdims=True)
        onehot = ridx == pos
        pick = jnp.max(jnp.where(onehot, e, -1.0), axis=0, keepdims=True)
        vals, picks = _record(k, kidx, m, pick, vals, picks)
        return jnp.where(onehot, -jnp.inf, s), vals, picks

    zeros = jnp.zeros((PEER_TOPK, t), F32)
    _, vals, picks = lax.fori_loop(0, PEER_TOPK, body, (s, zeros, zeros), unroll=True)
    return vals, picks


def _gelu_exact(x, half=0.5):
    return half * x * (1.0 + lax.erf(x * (2.0 ** -0.5)))


def _key_scores(hb, wq, k1, k2):
    half = D_KEY // 2
    qh = jnp.dot(hb, wq, preferred_element_type=F32).astype(BF16)
    return _dot_nt(k1, qh[:, :half]), _dot_nt(k2, qh[:, half:])


def _peer_kernel(xr_ref, xp_ref, g_ref, wq0_ref, wq_ref, k1_ref, k2_ref, u_ref, v_ref, o_ref,
                 h_buf, sc_buf, gt_buf, et_buf, a_buf, b_buf, gate_buf, wtok, acc):
    i = pl.program_id(0)
    c = pl.program_id(1)
    last_c = c == PEER_N_CHUNKS - 1
    route_slot = i % 2
    peer_slot = 1 - route_slot
    score_slot = c % 2
    h_peer_slot = (i + 2) % 3

    @pl.when(jnp.logical_and(i == 0, c == 0))
    def _():
        hb = _rms(xr_ref[...], g_ref[...]).astype(BF16)
        h_buf[0] = hb
        s1, s2 = _key_scores(hb, wq0_ref[...], k1_ref[...], k2_ref[...])
        sc_buf[0, 0] = s1
        sc_buf[0, 1] = s2
        h_buf[2] = jnp.zeros((PEER_TOK, D_MODEL), BF16)
        gt_buf[1] = jnp.zeros((PEER_HEADS * PEER_TOPK, PEER_TOK), F32)
        et_buf[1] = jnp.zeros((PEER_HEADS * PEER_TOPK, PEER_TOK), F32)

    @pl.when(last_c)
    def _():
        h_buf[(i + 1) % 3] = _rms(xr_ref[...], g_ref[...]).astype(BF16)

    @pl.when(c == 0)
    def _():
        acc[...] = jnp.zeros_like(acc)
        e = et_buf[peer_slot].T
        a = jnp.floor(e * (1.0 / N_KEYS))
        a_buf[...] = a
        b_buf[...] = e - a * float(N_KEYS)
        gate_buf[...] = gt_buf[peer_slot].T
        sub = lax.broadcasted_iota(jnp.int32, (N_KEYS, LANES), 0).astype(F32)

        def tok_group(grp, carry):
            t0 = pl.multiple_of(grp * TOK_GROUP, TOK_GROUP)
            slabs = []
            for j in range(TOK_GROUP):
                ar = a_buf[pl.ds(t0 + j, 1), :]
                br = b_buf[pl.ds(t0 + j, 1), :]
                gr = gate_buf[pl.ds(t0 + j, 1), :]
                pg = jnp.where(sub == ar, gr, 0.0).astype(BF16)
                qb = jnp.where(sub == br, 1.0, 0.0).astype(BF16)
                slabs.append(_dot_nt(pg, qb))
            wtok[:, pl.ds(t0, TOK_GROUP), :] = jnp.swapaxes(jnp.stack(slabs), 0, 1)
            return carry

        lax.fori_loop(0, PEER_TOK // TOK_GROUP, tok_group, 0, unroll=4)

    v1, i1 = _top_keys(sc_buf[score_slot, 0])
    v2, i2 = _top_keys(sc_buf[score_slot, 1])

    h_next_slot = jnp.where(last_c, i + 1, i) % 3
    s1, s2 = _key_scores(h_buf[h_next_slot], wq_ref[...], k1_ref[...], k2_ref[...])
    sc_buf[1 - score_slot, 0] = s1
    sc_buf[1 - score_slot, 1] = s2

    a_act = _dot_nt(h_buf[h_peer_slot], u_ref[...])
    key0 = c * PEER_CHUNK_KEYS
    w = jnp.concatenate([wtok[key0 + j] for j in range(PEER_CHUNK_KEYS)], axis=1)
    half = 0.5 + jnp.where(i1[PEER_TOPK - 1:, 0:1] + i2[PEER_TOPK - 1:, 0:1] < -1.0, 1.0, 0.0)
    wg = w * _gelu_exact(a_act, half)
    acc[...] += jnp.dot(wg.astype(BF16), v_ref[...], preferred_element_type=F32)

    top_s, top_e = _top_candidates(v1, i1, v2, i2)
    ex = jnp.exp(top_s - jnp.max(top_s, axis=0, keepdims=True))
    rows = pl.ds(pl.multiple_of(c * PEER_TOPK, PEER_TOPK), PEER_TOPK)
    gt_buf[route_slot, rows, :] = ex / jnp.sum(ex, axis=0, keepdims=True)
    et_buf[route_slot, rows, :] = top_e

    @pl.when(last_c)
    def _():
        o_ref[...] = xp_ref[...] + acc[...]


def _peer(x, g, w_query, keys1, keys2, eu, ev, layer):
    n_blk = N_TOK // PEER_TOK
    hk = PEER_HEADS * PEER_TOPK
    route_blk = lambda i, c: (jnp.minimum(i + c // (PEER_N_CHUNKS - 1), n_blk - 1), 0)
    peer_blk = lambda i, c: (jnp.maximum(i - 1, 0), 0)
    const2 = lambda i, c: (0, 0)
    chunk = pl.BlockSpec((None, PEER_CHUNK, D_MODEL), lambda i, c: (layer, c, 0))
    return pl.pallas_call(
        _peer_kernel,
        grid=(n_blk + 1, PEER_N_CHUNKS),
        in_specs=[pl.BlockSpec((PEER_TOK, D_MODEL), route_blk),
                  pl.BlockSpec((PEER_TOK, D_MODEL), peer_blk),
                  pl.BlockSpec((1, D_MODEL), const2),
                  pl.BlockSpec((D_MODEL, D_KEY), const2),
                  pl.BlockSpec((D_MODEL, D_KEY), lambda i, c: (0, (c + 1) % PEER_HEADS)),
                  pl.BlockSpec((N_KEYS, D_KEY // 2), const2),
                  pl.BlockSpec((N_KEYS, D_KEY // 2), const2),
                  chunk, chunk],
        out_specs=pl.BlockSpec((PEER_TOK, D_MODEL), peer_blk),
        out_shape=jax.ShapeDtypeStruct((N_TOK, D_MODEL), F32),
        scratch_shapes=[pltpu.VMEM((3, PEER_TOK, D_MODEL), BF16),
                        pltpu.VMEM((2, 2, N_KEYS, PEER_TOK), F32),
                        pltpu.VMEM((2, hk, PEER_TOK), F32),
                        pltpu.VMEM((2, hk, PEER_TOK), F32),
                        pltpu.VMEM((PEER_TOK, hk), F32),
                        pltpu.VMEM((PEER_TOK, hk), F32),
                        pltpu.VMEM((PEER_TOK, hk), F32),
                        pltpu.VMEM((N_KEYS, PEER_TOK, N_KEYS), F32),
                        pltpu.VMEM((PEER_TOK, D_MODEL), F32)],
        compiler_params=pltpu.CompilerParams(dimension_semantics=("arbitrary", "arbitrary"),
                                             vmem_limit_bytes=PEER_VMEM_LIMIT),
        name="peer",
    )(x, x, g, w_query, w_query, keys1, keys2, eu, ev)


def _ple_kernel(x_ref, pa_ref, pb_ref, g_ref, wg_ref, wp_ref, gf_ref, *o_refs, final):
    x = x_ref[...]
    hp = _rms(x, g_ref[...]).astype(BF16)
    gate = jax.nn.sigmoid(jnp.dot(hp, wg_ref[...], preferred_element_type=F32))
    proj = jnp.dot(_pick_part(pa_ref, pb_ref).astype(BF16), wp_ref[...], preferred_element_type=F32)
    y = x + gate * proj
    if not final:
        o_refs[0][...] = y
        return
    y = _rms(y, gf_ref[...])
    is_prompt = pl.program_id(0) < PROMPT_BLOCKS

    @pl.when(is_prompt)
    def _():
        o_refs[0][...] = y

    @pl.when(jnp.logical_not(is_prompt))
    def _():
        o_refs[1][...] = y


def _ple(x, p_prompt, p_sample, layer, g, w_gate, w_proj, g_final, final):
    const2 = lambda i: (0, 0)
    tok = pl.BlockSpec((TOK_BLOCK, D_MODEL), lambda i: (i, 0))
    p_specs = (pl.BlockSpec((None, TOK_BLOCK, PLE_DIM), lambda i: (layer, jnp.minimum(i, PROMPT_BLOCKS - 1), 0)),
               pl.BlockSpec((None, TOK_BLOCK, PLE_DIM), lambda i: (layer, jnp.maximum(i - PROMPT_BLOCKS, 0), 0)))
    if final:
        out_specs = list(_split_specs(D_MODEL, joint=False))
        out_shape = [jax.ShapeDtypeStruct((SEQ, D_MODEL), F32), jax.ShapeDtypeStruct((N_TOK - SEQ, D_MODEL), F32)]
    else:
        out_specs, out_shape = tok, jax.ShapeDtypeStruct((N_TOK, D_MODEL), F32)
    return pl.pallas_call(
        functools.partial(_ple_kernel, final=final),
        grid=(N_BLOCKS,),
        in_specs=[tok, *p_specs,
                  pl.BlockSpec((1, D_MODEL), const2),
                  pl.BlockSpec((D_MODEL, D_MODEL), const2),
                  pl.BlockSpec((PLE_DIM, D_MODEL), const2),
                  pl.BlockSpec((1, D_MODEL), const2)],
        out_specs=out_specs,
        out_shape=out_shape,
        compiler_params=pltpu.CompilerParams(dimension_semantics=("arbitrary",),
                                             vmem_limit_bytes=VMEM_LIMIT),
        name="ple",
    )(x, p_prompt, p_sample, g, w_gate, w_proj, g_final)


def _bias_tables(rpb):
    cols = np.arange(GRID_W)
    col_start = np.clip(cols - WIN_COLS // 2, 0, GRID_W - WIN_COLS)
    kc = np.arange(GRID_W)
    in_win = (kc[None, :] >= col_start[:, None]) & (kc[None, :] < col_start[:, None] + WIN_COLS)
    col_off = kc[None, :] - cols[:, None] + (WIN_COLS - 1)
    onehot = (col_off[None] == np.arange(2 * WIN_COLS - 1)[:, None, None]) & in_win[None]
    delta = np.arange(WIN_ROWS)
    j = np.arange(WIN_ROWS)
    row_off = j[None, :] - delta[:, None] + (WIN_ROWS - 1)
    rows = rpb[:, :, row_off]
    t = jnp.einsum('lhdjo,ock->ldhcjk', rows, jnp.asarray(onehot, F32), precision=lax.Precision.HIGHEST)
    t = t + jnp.asarray(np.where(in_win, 0.0, NEG_BIG), F32)[None, None, None, :, None, :]
    return t.reshape(DEPTH, WIN_ROWS, ATTN_HEADS // 2, 2 * GRID_W, WIN_ROWS * GRID_W)


def kernel(x_prompt, x_sample, p_prompt, p_sample, g_mix, w_in, rpb, conv_w, g_attn_out, g_conv_out, w_out,
           g_ffn, w_query, sub_keys1, sub_keys2, expert_u, expert_v, g_ple, w_ple_gate, w_ple_proj, g_final):
    x = (x_prompt.reshape(SEQ, D_MODEL), x_sample.reshape(DEC_BATCH * DEC_SEQ, D_MODEL))
    pp = p_prompt.reshape(DEPTH, SEQ, PLE_DIM)
    ps = p_sample.reshape(DEPTH, DEC_BATCH * DEC_SEQ, PLE_DIM)
    group = np.arange(ATTN_W) // HEAD_DIM
    gmat = jnp.asarray(group[:, None] == group[None, :], dtype=BF16)
    row = lambda g: g.reshape(1, -1)
    bias = _bias_tables(rpb)
    eu, ev = expert_u.astype(BF16), expert_v.astype(BF16)
    for i in range(DEPTH):
        q, k, v, gb, gc, gu = _proj(x, row(g_mix[i]), w_in[i].astype(BF16))
        x = _mixer(x, q, k, v, gb, gc, gu, bias, i, conv_w[i], row(g_attn_out[i]),
                   row(g_conv_out[i]), gmat, w_out[i].astype(BF16))
        x = _peer(x, row(g_ffn[i]), w_query[i].astype(BF16), sub_keys1[i].astype(BF16),
                  sub_keys2[i].astype(BF16), eu, ev, i)
        x = _ple(x, pp, ps, i, row(g_ple[i]), w_ple_gate[i].astype(BF16), w_ple_proj[i].astype(BF16),
                 row(g_final), final=(i == DEPTH - 1))
    y_prompt, y_sample = x
    return (y_prompt.reshape(1, SEQ, D_MODEL), y_sample.reshape(DEC_BATCH, DEC_SEQ, D_MODEL))
```

```python
import functools

import numpy as np
import jax
import jax.numpy as jnp
from jax import lax
from jax.experimental import pallas as pl
from jax.experimental.pallas import tpu as pltpu

F32 = jnp.float32
BF16 = jnp.bfloat16

D_MODEL = 1024
DEPTH = 2
SEQ = 16384
DEC_BATCH = 4
DEC_SEQ = 4096
N_TOK = SEQ + DEC_BATCH * DEC_SEQ

GRID_W = 64
WIN_ROWS = 8
WIN_COLS = 16
ATTN_HEADS = 8
HEAD_DIM = 64
ATTN_W = ATTN_HEADS * HEAD_DIM
CONV_W = D_MODEL - ATTN_W
N_KEYS = 128
N_EXPERTS = N_KEYS * N_KEYS
PEER_HEADS = 8
D_KEY = 256
PEER_TOPK = 16
PLE_DIM = 256
EPS = 1e-6
NEG_BIG = -1e30

LANES = 128
ROW_BLOCK = WIN_ROWS
TOK_BLOCK = ROW_BLOCK * GRID_W
N_BLOCKS = N_TOK // TOK_BLOCK
PROMPT_ROWS = SEQ // GRID_W
SAMPLE_ROWS = DEC_SEQ // GRID_W
PROMPT_BLOCKS = PROMPT_ROWS // ROW_BLOCK
SAMPLE_BLOCKS = SAMPLE_ROWS // ROW_BLOCK
HALO = 8

PEER_TOK = 256
TOK_GROUP = 8
PEER_CHUNK_KEYS = 16
PEER_CHUNK = PEER_CHUNK_KEYS * N_KEYS
PEER_N_CHUNKS = N_EXPERTS // PEER_CHUNK
assert PEER_N_CHUNKS == PEER_HEADS

VMEM_LIMIT = 48 * 1024 * 1024
PEER_VMEM_LIMIT = 56 * 1024 * 1024


def _rms(x, g):
    return x * lax.rsqrt(jnp.mean(x * x, axis=-1, keepdims=True) + EPS) * g


def _dot_nt(a, b):
    return lax.dot_general(a, b, (((1,), (1,)), ((), ())), preferred_element_type=F32)


def _split_specs(width, joint):
    off = PROMPT_BLOCKS if joint else 0
    return (pl.BlockSpec((TOK_BLOCK, width), lambda i: (jnp.minimum(i, PROMPT_BLOCKS - 1), 0)),
            pl.BlockSpec((TOK_BLOCK, width), lambda i: (jnp.maximum(i - PROMPT_BLOCKS, 0) + off, 0)))


def _pick_part(prompt_ref, sample_ref):
    return jnp.where(pl.program_id(0) < PROMPT_BLOCKS, prompt_ref[...], sample_ref[...])


def _parts(x):
    return (x[0], x[1], False) if isinstance(x, tuple) else (x, x, True)


def _proj_kernel(xa_ref, xb_ref, g_ref, w_ref, q_ref, k_ref, v_ref, gb_ref, gc_ref, gu_ref):
    hb = _rms(_pick_part(xa_ref, xb_ref), g_ref[...]).astype(BF16)
    outs = (q_ref, k_ref, v_ref, gb_ref, gc_ref, gu_ref)
    for j, o_ref in enumerate(outs):
        z = jnp.dot(hb, w_ref[:, j * ATTN_W:(j + 1) * ATTN_W], preferred_element_type=F32)
        if j == 0:
            z = z * (HEAD_DIM ** -0.5)
        o_ref[...] = z.astype(o_ref.dtype)


def _proj(x, g, w_in):
    xa, xb, joint = _parts(x)
    tok = pl.BlockSpec((TOK_BLOCK, ATTN_W), lambda i: (i, 0))
    return pl.pallas_call(
        _proj_kernel,
        grid=(N_BLOCKS,),
        in_specs=[*_split_specs(D_MODEL, joint),
                  pl.BlockSpec((1, D_MODEL), lambda i: (0, 0)),
                  pl.BlockSpec((D_MODEL, 6 * ATTN_W), lambda i: (0, 0))],
        out_specs=[tok] * 6,
        out_shape=[jax.ShapeDtypeStruct((N_TOK, ATTN_W), BF16)] * 3
                  + [jax.ShapeDtypeStruct((N_TOK, ATTN_W), F32)] * 3,
        compiler_params=pltpu.CompilerParams(dimension_semantics=("arbitrary",),
                                             vmem_limit_bytes=VMEM_LIMIT),
        name="proj",
    )(xa, xb, g, w_in)


def _group_norm(xv, gmat, g):
    sq = xv * xv
    hi = sq.astype(BF16)
    lo = (sq - hi.astype(F32)).astype(BF16)
    ms = (jnp.dot(hi, gmat, preferred_element_type=F32)
          + jnp.dot(lo, gmat, preferred_element_type=F32)) * (1.0 / HEAD_DIM)
    return xv * lax.rsqrt(ms + EPS) * g


def _mixer_kernel(xa_ref, xb_ref, q_ref, kp_ref, kc_ref, kn_ref, vp_ref, vc_ref, vn_ref,
                  gb_ref, gc_ref, gu_ref, gcp_ref, gup_ref, gcn_ref, gun_ref,
                  bias_ref, cw_ref, ga_ref, gcv_ref, gmat_ref, wo_ref,
                  o_ref, kbuf, vbuf, abuf):
    i = pl.program_id(0)
    is_prompt = i < PROMPT_BLOCKS
    sample = (i - PROMPT_BLOCKS) // SAMPLE_BLOCKS
    seq_r0 = jnp.where(is_prompt, 0, PROMPT_ROWS + SAMPLE_ROWS * sample)
    seq_r1 = jnp.where(is_prompt, PROMPT_ROWS, seq_r0 + SAMPLE_ROWS)
    blk_r0 = ROW_BLOCK * i

    kbuf[0:TOK_BLOCK, :] = kp_ref[...]
    kbuf[TOK_BLOCK:2 * TOK_BLOCK, :] = kc_ref[...]
    kbuf[2 * TOK_BLOCK:3 * TOK_BLOCK, :] = kn_ref[...]
    vbuf[0:TOK_BLOCK, :] = vp_ref[...]
    vbuf[TOK_BLOCK:2 * TOK_BLOCK, :] = vc_ref[...]
    vbuf[2 * TOK_BLOCK:3 * TOK_BLOCK, :] = vn_ref[...]

    lane = lax.broadcasted_iota(jnp.int32, (GRID_W, LANES), 1)
    low_half = lane < HEAD_DIM
    n_win = WIN_ROWS * GRID_W
    for jr in range(ROW_BLOCK):
        r = blk_r0 + jr
        rs = jnp.clip(r - WIN_ROWS // 2, seq_r0, seq_r1 - WIN_ROWS)
        delta = r - rs
        start = pl.multiple_of((rs - blk_r0 + ROW_BLOCK) * GRID_W, GRID_W)
        pairs = range(ATTN_HEADS // 2)
        col = [slice(p * LANES, (p + 1) * LANES) for p in pairs]
        scores = []
        for p in pairs:
            qp = q_ref[jr * GRID_W:(jr + 1) * GRID_W, col[p]]
            zero = jnp.zeros_like(qp)
            q2 = jnp.concatenate([jnp.where(low_half, qp, zero), jnp.where(low_half, zero, qp)], axis=0)
            scores.append(_dot_nt(q2, kbuf[pl.ds(start, n_win), col[p]]) + bias_ref[delta, p])
        probs, norms = [], []
        for s in scores:
            e = jnp.exp(s - jnp.max(s, axis=-1, keepdims=True))
            probs.append(e.astype(BF16))
            norms.append(jnp.sum(e, axis=-1, keepdims=True))
        for p in pairs:
            o2 = jnp.dot(probs[p], vbuf[pl.ds(start, n_win), col[p]], preferred_element_type=F32) / norms[p]
            abuf[jr * GRID_W:(jr + 1) * GRID_W, col[p]] = jnp.where(low_half, o2[:GRID_W], o2[GRID_W:])

    first = blk_r0 == seq_r0
    last = blk_r0 + ROW_BLOCK == seq_r1
    cu = gc_ref[...] * gu_ref[...]
    prev_row = jnp.where(first, 0.0, gcp_ref[HALO - 1:HALO, :] * gup_ref[HALO - 1:HALO, :])
    next_row = jnp.where(last, 0.0, gcn_ref[0:1, :] * gun_ref[0:1, :])
    row = lax.broadcasted_iota(jnp.int32, (TOK_BLOCK, CONV_W), 0)
    up_prev = jnp.where(row == 0, prev_row, pltpu.roll(cu, 1, axis=0))
    up_next = jnp.where(row == TOK_BLOCK - 1, next_row, pltpu.roll(cu, TOK_BLOCK - 1, axis=0))
    conv = gb_ref[...] * (up_prev * cw_ref[0:1, :] + cu * cw_ref[1:2, :] + up_next * cw_ref[2:3, :])

    gmat = gmat_ref[...]
    attn_n = _group_norm(abuf[...], gmat, ga_ref[...]).astype(BF16)
    conv_n = _group_norm(conv, gmat, gcv_ref[...]).astype(BF16)
    y = (jnp.dot(attn_n, wo_ref[0:ATTN_W, :], preferred_element_type=F32)
         + jnp.dot(conv_n, wo_ref[ATTN_W:D_MODEL, :], preferred_element_type=F32))
    o_ref[...] = _pick_part(xa_ref, xb_ref) + y


def _mixer(x, q, k, v, gb, gc, gu, bias, layer, conv_w, g_attn, g_conv, gmat, w_out):
    xa, xb, joint = _parts(x)
    cur = lambda i: (i, 0)
    prev = lambda i: (jnp.maximum(i - 1, 0), 0)
    nxt = lambda i: (jnp.minimum(i + 1, N_BLOCKS - 1), 0)
    halo_per_block = TOK_BLOCK // HALO
    hprev = lambda i: (jnp.maximum(i * halo_per_block - 1, 0), 0)
    hnext = lambda i: (jnp.minimum((i + 1) * halo_per_block, N_TOK // HALO - 1), 0)
    const2 = lambda i: (0, 0)
    blk = lambda m: pl.BlockSpec((TOK_BLOCK, ATTN_W), m)
    halo = lambda m: pl.BlockSpec((HALO, CONV_W), m)
    return pl.pallas_call(
        _mixer_kernel,
        grid=(N_BLOCKS,),
        in_specs=[*_split_specs(D_MODEL, joint),
                  blk(cur), blk(prev), blk(cur), blk(nxt), blk(prev), blk(cur), blk(nxt),
                  blk(cur), blk(cur), blk(cur), halo(hprev), halo(hprev), halo(hnext), halo(hnext),
                  pl.BlockSpec((None, WIN_ROWS, ATTN_HEADS // 2, 2 * GRID_W, WIN_ROWS * GRID_W),
                               lambda i: (layer, 0, 0, 0, 0)),
                  pl.BlockSpec((3, CONV_W), const2),
                  pl.BlockSpec((1, ATTN_W), const2),
                  pl.BlockSpec((1, CONV_W), const2),
                  pl.BlockSpec((ATTN_W, ATTN_W), const2),
                  pl.BlockSpec((D_MODEL, D_MODEL), const2)],
        out_specs=pl.BlockSpec((TOK_BLOCK, D_MODEL), cur),
        out_shape=jax.ShapeDtypeStruct((N_TOK, D_MODEL), F32),
        scratch_shapes=[pltpu.VMEM((3 * TOK_BLOCK, ATTN_W), BF16),
                        pltpu.VMEM((3 * TOK_BLOCK, ATTN_W), BF16),
                        pltpu.VMEM((TOK_BLOCK, ATTN_W), F32)],
        compiler_params=pltpu.CompilerParams(dimension_semantics=("arbitrary",),
                                             vmem_limit_bytes=VMEM_LIMIT),
        name="mixer",
    )(xa, xb, q, k, k, k, v, v, v, gb, gc, gu, gc, gu, gc, gu, bias, conv_w, g_attn, g_conv, gmat, w_out)


def _record(k, kidx, m, pick, vals, picks):
    sel = kidx == k
    return jnp.where(sel, m, vals), jnp.where(sel, pick, picks)


def _top_keys(s):
    n, t = s.shape
    half = n // 2
    r0 = lax.broadcasted_iota(jnp.int32, (half, t), 0).astype(F32)
    r1 = r0 + float(half)
    first = s[:half] >= s[half:]
    hi = jnp.where(first, s[:half], s[half:])
    lo = jnp.where(first, s[half:], s[:half])
    ihi = jnp.where(first, r0, r1)
    ilo = jnp.where(first, r1, r0)
    kidx = lax.broadcasted_iota(jnp.int32, (PEER_TOPK, t), 0)

    def body(k, carry):
        hi, lo, ihi, vals, picks = carry
        m = jnp.max(hi, axis=0, keepdims=True)
        pos = jnp.min(jnp.where(hi == m, ihi, float(n)), axis=0, keepdims=True)
        onehot = ihi == pos
        vals, picks = _record(k, kidx, m, pos, vals, picks)
        return (jnp.where(onehot, lo, hi), jnp.where(onehot, -jnp.inf, lo), jnp.where(onehot, ilo, ihi),
                vals, picks)

    zeros = jnp.zeros((PEER_TOPK, t), F32)
    out = lax.fori_loop(0, PEER_TOPK, body, (hi, lo, ihi, zeros, zeros), unroll=True)
    return out[3], out[4]


_CAND_PER_KA = [PEER_TOPK // (ka + 1) for ka in range(PEER_TOPK)]
_N_CAND = sum(_CAND_PER_KA)
_CAND_PAD = -_N_CAND % 8


def _top_candidates(v1, i1, v2, i2):
    t = v1.shape[1]
    s_rows, e_rows = [], []
    for ka, n_kb in enumerate(_CAND_PER_KA):
        s_rows.append(v1[ka:ka + 1, :] + v2[:n_kb, :])
        e_rows.append(i1[ka:ka + 1, :] * float(N_KEYS) + i2[:n_kb, :])
    s_rows.append(jnp.full((_CAND_PAD, t), -jnp.inf, F32))
    e_rows.append(jnp.zeros((_CAND_PAD, t), F32))
    s = jnp.concatenate(s_rows, axis=0)
    e = jnp.concatenate(e_rows, axis=0)
    rows = s.shape[0]
    ridx = lax.broadcasted_iota(jnp.int32, (rows, t), 0).astype(F32)
    kidx = lax.broadcasted_iota(jnp.int32, (PEER_TOPK, t), 0)

    def body(k, carry):
        s, vals, picks = carry
        m = jnp.max(s, axis=0, keepdims=True)
        pos = jnp.min(jnp.where(s == m, ridx, float(rows)), axis=0, keepdims=True)
        onehot = ridx == pos
        pick = jnp.max(jnp.where(onehot, e, -1.0), axis=0, keepdims=True)
        vals, picks = _record(k, kidx, m, pick, vals, picks)
        return jnp.where(onehot, -jnp.inf, s), vals, picks

    zeros = jnp.zeros((PEER_TOPK, t), F32)
    _, vals, picks = lax.fori_loop(0, PEER_TOPK, body, (s, zeros, zeros), unroll=True)
    return vals, picks


def _gelu_exact(x, half=0.5):
    return half * x * (1.0 + lax.erf(x * (2.0 ** -0.5)))


def _key_scores(hb, wq, k1, k2):
    half = D_KEY // 2
    qh = jnp.dot(hb, wq, preferred_element_type=F32).astype(BF16)
    return _dot_nt(k1, qh[:, :half]), _dot_nt(k2, qh[:, half:])


def _peer_kernel(xr_ref, xp_ref, g_ref, wq0_ref, wq_ref, k1_ref, k2_ref, ut_ref, v_ref, o_ref,
                 h_buf, sc_buf, gt_buf, et_buf, a_buf, b_buf, gate_buf, wtok, acc):
    i = pl.program_id(0)
    c = pl.program_id(1)
    last_c = c == PEER_N_CHUNKS - 1
    route_slot = i % 2
    peer_slot = 1 - route_slot
    score_slot = c % 2
    h_peer_slot = (i + 2) % 3

    @pl.when(jnp.logical_and(i == 0, c == 0))
    def _():
        hb = _rms(xr_ref[...], g_ref[...]).astype(BF16)
        h_buf[0] = hb
        s1, s2 = _key_scores(hb, wq0_ref[...], k1_ref[...], k2_ref[...])
        sc_buf[0, 0] = s1
        sc_buf[0, 1] = s2
        h_buf[2] = jnp.zeros((PEER_TOK, D_MODEL), BF16)
        gt_buf[1] = jnp.zeros((PEER_HEADS * PEER_TOPK, PEER_TOK), F32)
        et_buf[1] = jnp.zeros((PEER_HEADS * PEER_TOPK, PEER_TOK), F32)

    @pl.when(last_c)
    def _():
        h_buf[(i + 1) % 3] = _rms(xr_ref[...], g_ref[...]).astype(BF16)

    @pl.when(c == 0)
    def _():
        acc[...] = jnp.zeros_like(acc)
        e = et_buf[peer_slot].T
        a = jnp.floor(e * (1.0 / N_KEYS))
        a_buf[...] = a
        b_buf[...] = e - a * float(N_KEYS)
        gate_buf[...] = gt_buf[peer_slot].T
        sub = lax.broadcasted_iota(jnp.int32, (N_KEYS, LANES), 0).astype(F32)

        def tok_group(grp, carry):
            t0 = pl.multiple_of(grp * TOK_GROUP, TOK_GROUP)
            slabs = []
            for j in range(TOK_GROUP):
                ar = a_buf[pl.ds(t0 + j, 1), :]
                br = b_buf[pl.ds(t0 + j, 1), :]
                gr = gate_buf[pl.ds(t0 + j, 1), :]
                pg = jnp.where(sub == ar, gr, 0.0).astype(BF16)
                qb = jnp.where(sub == br, 1.0, 0.0).astype(BF16)
                slabs.append(_dot_nt(pg, qb))
            wtok[:, pl.ds(t0, TOK_GROUP), :] = jnp.swapaxes(jnp.stack(slabs), 0, 1)
            return carry

        lax.fori_loop(0, PEER_TOK // TOK_GROUP, tok_group, 0, unroll=4)

    v1, i1 = _top_keys(sc_buf[score_slot, 0])
    v2, i2 = _top_keys(sc_buf[score_slot, 1])

    h_next_slot = jnp.where(last_c, i + 1, i) % 3
    s1, s2 = _key_scores(h_buf[h_next_slot], wq_ref[...], k1_ref[...], k2_ref[...])
    sc_buf[1 - score_slot, 0] = s1
    sc_buf[1 - score_slot, 1] = s2

    a_act = jnp.dot(h_buf[h_peer_slot], ut_ref[...], preferred_element_type=F32)
    key0 = c * PEER_CHUNK_KEYS
    w = jnp.concatenate([wtok[key0 + j] for j in range(PEER_CHUNK_KEYS)], axis=1)
    half = 0.5 + jnp.where(i1[PEER_TOPK - 1:, 0:1] + i2[PEER_TOPK - 1:, 0:1] < -1.0, 1.0, 0.0)
    wg = w * _gelu_exact(a_act, half)
    acc[...] += jnp.dot(wg.astype(BF16), v_ref[...], preferred_element_type=F32)

    top_s, top_e = _top_candidates(v1, i1, v2, i2)
    ex = jnp.exp(top_s - jnp.max(top_s, axis=0, keepdims=True))
    rows = pl.ds(pl.multiple_of(c * PEER_TOPK, PEER_TOPK), PEER_TOPK)
    gt_buf[route_slot, rows, :] = ex / jnp.sum(ex, axis=0, keepdims=True)
    et_buf[route_slot, rows, :] = top_e

    @pl.when(last_c)
    def _():
        o_ref[...] = xp_ref[...] + acc[...]


def _peer(x, g, w_query, keys1, keys2, eut, ev, layer):
    n_blk = N_TOK // PEER_TOK
    hk = PEER_HEADS * PEER_TOPK
    route_blk = lambda i, c: (jnp.minimum(i + c // (PEER_N_CHUNKS - 1), n_blk - 1), 0)
    peer_blk = lambda i, c: (jnp.maximum(i - 1, 0), 0)
    const2 = lambda i, c: (0, 0)
    chunk = pl.BlockSpec((None, PEER_CHUNK, D_MODEL), lambda i, c: (layer, c, 0))
    chunk_t = pl.BlockSpec((None, D_MODEL, PEER_CHUNK), lambda i, c: (layer, 0, c))
    return pl.pallas_call(
        _peer_kernel,
        grid=(n_blk + 1, PEER_N_CHUNKS),
        in_specs=[pl.BlockSpec((PEER_TOK, D_MODEL), route_blk),
                  pl.BlockSpec((PEER_TOK, D_MODEL), peer_blk),
                  pl.BlockSpec((1, D_MODEL), const2),
                  pl.BlockSpec((D_MODEL, D_KEY), const2),
                  pl.BlockSpec((D_MODEL, D_KEY), lambda i, c: (0, (c + 1) % PEER_HEADS)),
                  pl.BlockSpec((N_KEYS, D_KEY // 2), const2),
                  pl.BlockSpec((N_KEYS, D_KEY // 2), const2),
                  chunk_t, chunk],
        out_specs=pl.BlockSpec((PEER_TOK, D_MODEL), peer_blk),
        out_shape=jax.ShapeDtypeStruct((N_TOK, D_MODEL), F32),
        scratch_shapes=[pltpu.VMEM((3, PEER_TOK, D_MODEL), BF16),
                        pltpu.VMEM((2, 2, N_KEYS, PEER_TOK), F32),
                        pltpu.VMEM((2, hk, PEER_TOK), F32),
                        pltpu.VMEM((2, hk, PEER_TOK), F32),
                        pltpu.VMEM((PEER_TOK, hk), F32),
                        pltpu.VMEM((PEER_TOK, hk), F32),
                        pltpu.VMEM((PEER_TOK, hk), F32),
                        pltpu.VMEM((N_KEYS, PEER_TOK, N_KEYS), F32),
                        pltpu.VMEM((PEER_TOK, D_MODEL), F32)],
        compiler_params=pltpu.CompilerParams(dimension_semantics=("arbitrary", "arbitrary"),
                                             vmem_limit_bytes=PEER_VMEM_LIMIT),
        name="peer",
    )(x, x, g, w_query, w_query, keys1, keys2, eut, ev)


def _ple_kernel(x_ref, pa_ref, pb_ref, g_ref, wg_ref, wp_ref, gf_ref, *o_refs, final):
    x = x_ref[...]
    hp = _rms(x, g_ref[...]).astype(BF16)
    gate = jax.nn.sigmoid(jnp.dot(hp, wg_ref[...], preferred_element_type=F32))
    proj = jnp.dot(_pick_part(pa_ref, pb_ref).astype(BF16), wp_ref[...], preferred_element_type=F32)
    y = x + gate * proj
    if not final:
        o_refs[0][...] = y
        return
    y = _rms(y, gf_ref[...])
    is_prompt = pl.program_id(0) < PROMPT_BLOCKS

    @pl.when(is_prompt)
    def _():
        o_refs[0][...] = y

    @pl.when(jnp.logical_not(is_prompt))
    def _():
        o_refs[1][...] = y


def _ple(x, p_prompt, p_sample, layer, g, w_gate, w_proj, g_final, final):
    const2 = lambda i: (0, 0)
    tok = pl.BlockSpec((TOK_BLOCK, D_MODEL), lambda i: (i, 0))
    p_specs = (pl.BlockSpec((None, TOK_BLOCK, PLE_DIM), lambda i: (layer, jnp.minimum(i, PROMPT_BLOCKS - 1), 0)),
               pl.BlockSpec((None, TOK_BLOCK, PLE_DIM), lambda i: (layer, jnp.maximum(i - PROMPT_BLOCKS, 0), 0)))
    if final:
        out_specs = list(_split_specs(D_MODEL, joint=False))
        out_shape = [jax.ShapeDtypeStruct((SEQ, D_MODEL), F32), jax.ShapeDtypeStruct((N_TOK - SEQ, D_MODEL), F32)]
    else:
        out_specs, out_shape = tok, jax.ShapeDtypeStruct((N_TOK, D_MODEL), F32)
    return pl.pallas_call(
        functools.partial(_ple_kernel, final=final),
        grid=(N_BLOCKS,),
        in_specs=[tok, *p_specs,
                  pl.BlockSpec((1, D_MODEL), const2),
                  pl.BlockSpec((D_MODEL, D_MODEL), const2),
                  pl.BlockSpec((PLE_DIM, D_MODEL), const2),
                  pl.BlockSpec((1, D_MODEL), const2)],
        out_specs=out_specs,
        out_shape=out_shape,
        compiler_params=pltpu.CompilerParams(dimension_semantics=("arbitrary",),
                                             vmem_limit_bytes=VMEM_LIMIT),
        name="ple",
    )(x, p_prompt, p_sample, g, w_gate, w_proj, g_final)


def _bias_tables(rpb):
    cols = np.arange(GRID_W)
    col_start = np.clip(cols - WIN_COLS // 2, 0, GRID_W - WIN_COLS)
    kc = np.arange(GRID_W)
    in_win = (kc[None, :] >= col_start[:, None]) & (kc[None, :] < col_start[:, None] + WIN_COLS)
    col_off = kc[None, :] - cols[:, None] + (WIN_COLS - 1)
    onehot = (col_off[None] == np.arange(2 * WIN_COLS - 1)[:, None, None]) & in_win[None]
    delta = np.arange(WIN_ROWS)
    j = np.arange(WIN_ROWS)
    row_off = j[None, :] - delta[:, None] + (WIN_ROWS - 1)
    rows = rpb[:, :, row_off]
    t = jnp.einsum('lhdjo,ock->ldhcjk', rows, jnp.asarray(onehot, F32), precision=lax.Precision.HIGHEST)
    t = t + jnp.asarray(np.where(in_win, 0.0, NEG_BIG), F32)[None, None, None, :, None, :]
    return t.reshape(DEPTH, WIN_ROWS, ATTN_HEADS // 2, 2 * GRID_W, WIN_ROWS * GRID_W)


def kernel(x_prompt, x_sample, p_prompt, p_sample, g_mix, w_in, rpb, conv_w, g_attn_out, g_conv_out, w_out,
           g_ffn, w_query, sub_keys1, sub_keys2, expert_u, expert_v, g_ple, w_ple_gate, w_ple_proj, g_final):
    x = (x_prompt.reshape(SEQ, D_MODEL), x_sample.reshape(DEC_BATCH * DEC_SEQ, D_MODEL))
    pp = p_prompt.reshape(DEPTH, SEQ, PLE_DIM)
    ps = p_sample.reshape(DEPTH, DEC_BATCH * DEC_SEQ, PLE_DIM)
    group = np.arange(ATTN_W) // HEAD_DIM
    gmat = jnp.asarray(group[:, None] == group[None, :], dtype=BF16)
    row = lambda g: g.reshape(1, -1)
    bias = _bias_tables(rpb)
    eut, ev = jnp.swapaxes(expert_u.astype(BF16), 1, 2), expert_v.astype(BF16)
    for i in range(DEPTH):
        q, k, v, gb, gc, gu = _proj(x, row(g_mix[i]), w_in[i].astype(BF16))
        x = _mixer(x, q, k, v, gb, gc, gu, bias, i, conv_w[i], row(g_attn_out[i]),
                   row(g_conv_out[i]), gmat, w_out[i].astype(BF16))
        x = _peer(x, row(g_ffn[i]), w_query[i].astype(BF16), sub_keys1[i].astype(BF16),
                  sub_keys2[i].astype(BF16), eut, ev, i)
        x = _ple(x, pp, ps, i, row(g_ple[i]), w_ple_gate[i].astype(BF16), w_ple_proj[i].astype(BF16),
                 row(g_final), final=(i == DEPTH - 1))
    y_prompt, y_sample = x
    return (y_prompt.reshape(1, SEQ, D_MODEL), y_sample.reshape(DEC_BATCH, DEC_SEQ, D_MODEL))
```

```python
import functools

import numpy as np
import jax
import jax.numpy as jnp
from jax import lax
from jax.experimental import pallas as pl
from jax.experimental.pallas import tpu as pltpu

F32 = jnp.float32
BF16 = jnp.bfloat16

D_MODEL = 1024
DEPTH = 2
SEQ = 16384
DEC_BATCH = 4
DEC_SEQ = 4096
N_TOK = SEQ + DEC_BATCH * DEC_SEQ

GRID_W = 64
WIN_ROWS = 8
WIN_COLS = 16
ATTN_HEADS = 8
HEAD_DIM = 64
ATTN_W = ATTN_HEADS * HEAD_DIM
CONV_W = D_MODEL - ATTN_W
N_KEYS = 128
N_EXPERTS = N_KEYS * N_KEYS
PEER_HEADS = 8
D_KEY = 256
PEER_TOPK = 16
PLE_DIM = 256
EPS = 1e-6
NEG_BIG = -1e30

LANES = 128
ROW_BLOCK = WIN_ROWS
TOK_BLOCK = ROW_BLOCK * GRID_W
N_BLOCKS = N_TOK // TOK_BLOCK
PROMPT_ROWS = SEQ // GRID_W
SAMPLE_ROWS = DEC_SEQ // GRID_W
PROMPT_BLOCKS = PROMPT_ROWS // ROW_BLOCK
SAMPLE_BLOCKS = SAMPLE_ROWS // ROW_BLOCK
HALO = 8

PEER_TOK = 256
TOK_GROUP = 8
PEER_CHUNK_KEYS = 16
PEER_CHUNK = PEER_CHUNK_KEYS * N_KEYS
PEER_N_CHUNKS = N_EXPERTS // PEER_CHUNK
assert PEER_N_CHUNKS == PEER_HEADS

VMEM_LIMIT = 48 * 1024 * 1024
PEER_VMEM_LIMIT = 56 * 1024 * 1024


def _rms(x, g):
    return x * lax.rsqrt(jnp.mean(x * x, axis=-1, keepdims=True) + EPS) * g


def _dot_nt(a, b):
    return lax.dot_general(a, b, (((1,), (1,)), ((), ())), preferred_element_type=F32)


def _split_specs(width, joint):
    off = PROMPT_BLOCKS if joint else 0
    return (pl.BlockSpec((TOK_BLOCK, width), lambda i: (jnp.minimum(i, PROMPT_BLOCKS - 1), 0)),
            pl.BlockSpec((TOK_BLOCK, width), lambda i: (jnp.maximum(i - PROMPT_BLOCKS, 0) + off, 0)))


def _pick_part(prompt_ref, sample_ref):
    return jnp.where(pl.program_id(0) < PROMPT_BLOCKS, prompt_ref[...], sample_ref[...])


def _parts(x):
    return (x[0], x[1], False) if isinstance(x, tuple) else (x, x, True)


def _proj_kernel(xa_ref, xb_ref, g_ref, w_ref, q_ref, k_ref, v_ref, gb_ref, gc_ref, gu_ref):
    hb = _rms(_pick_part(xa_ref, xb_ref), g_ref[...]).astype(BF16)
    outs = (q_ref, k_ref, v_ref, gb_ref, gc_ref, gu_ref)
    for j, o_ref in enumerate(outs):
        z = jnp.dot(hb, w_ref[:, j * ATTN_W:(j + 1) * ATTN_W], preferred_element_type=F32)
        if j == 0:
            z = z * (HEAD_DIM ** -0.5)
        o_ref[...] = z.astype(o_ref.dtype)


def _proj(x, g, w_in):
    xa, xb, joint = _parts(x)
    tok = pl.BlockSpec((TOK_BLOCK, ATTN_W), lambda i: (i, 0))
    return pl.pallas_call(
        _proj_kernel,
        grid=(N_BLOCKS,),
        in_specs=[*_split_specs(D_MODEL, joint),
                  pl.BlockSpec((1, D_MODEL), lambda i: (0, 0)),
                  pl.BlockSpec((D_MODEL, 6 * ATTN_W), lambda i: (0, 0))],
        out_specs=[tok] * 6,
        out_shape=[jax.ShapeDtypeStruct((N_TOK, ATTN_W), BF16)] * 3
                  + [jax.ShapeDtypeStruct((N_TOK, ATTN_W), F32)] * 3,
        compiler_params=pltpu.CompilerParams(dimension_semantics=("arbitrary",),
                                             vmem_limit_bytes=VMEM_LIMIT),
        name="proj",
    )(xa, xb, g, w_in)


def _group_norm(xv, gmat, g):
    sq = xv * xv
    hi = sq.astype(BF16)
    lo = (sq - hi.astype(F32)).astype(BF16)
    ms = (jnp.dot(hi, gmat, preferred_element_type=F32)
          + jnp.dot(lo, gmat, preferred_element_type=F32)) * (1.0 / HEAD_DIM)
    return xv * lax.rsqrt(ms + EPS) * g


def _mixer_kernel(xa_ref, xb_ref, q_ref, kp_ref, kc_ref, kn_ref, vp_ref, vc_ref, vn_ref,
                  gb_ref, gc_ref, gu_ref, gcp_ref, gup_ref, gcn_ref, gun_ref,
                  bias_ref, cw_ref, ga_ref, gcv_ref, gmat_ref, wo_ref,
                  o_ref, kbuf, vbuf, abuf):
    i = pl.program_id(0)
    is_prompt = i < PROMPT_BLOCKS
    sample = (i - PROMPT_BLOCKS) // SAMPLE_BLOCKS
    seq_r0 = jnp.where(is_prompt, 0, PROMPT_ROWS + SAMPLE_ROWS * sample)
    seq_r1 = jnp.where(is_prompt, PROMPT_ROWS, seq_r0 + SAMPLE_ROWS)
    blk_r0 = ROW_BLOCK * i

    kbuf[0:TOK_BLOCK, :] = kp_ref[...]
    kbuf[TOK_BLOCK:2 * TOK_BLOCK, :] = kc_ref[...]
    kbuf[2 * TOK_BLOCK:3 * TOK_BLOCK, :] = kn_ref[...]
    vbuf[0:TOK_BLOCK, :] = vp_ref[...]
    vbuf[TOK_BLOCK:2 * TOK_BLOCK, :] = vc_ref[...]
    vbuf[2 * TOK_BLOCK:3 * TOK_BLOCK, :] = vn_ref[...]

    lane = lax.broadcasted_iota(jnp.int32, (GRID_W, LANES), 1)
    low_half = lane < HEAD_DIM
    n_win = WIN_ROWS * GRID_W
    for jr in range(ROW_BLOCK):
        r = blk_r0 + jr
        rs = jnp.clip(r - WIN_ROWS // 2, seq_r0, seq_r1 - WIN_ROWS)
        delta = r - rs
        start = pl.multiple_of((rs - blk_r0 + ROW_BLOCK) * GRID_W, GRID_W)
        pairs = range(ATTN_HEADS // 2)
        col = [slice(p * LANES, (p + 1) * LANES) for p in pairs]
        scores = []
        for p in pairs:
            qp = q_ref[jr * GRID_W:(jr + 1) * GRID_W, col[p]]
            zero = jnp.zeros_like(qp)
            q2 = jnp.concatenate([jnp.where(low_half, qp, zero), jnp.where(low_half, zero, qp)], axis=0)
            scores.append(_dot_nt(q2, kbuf[pl.ds(start, n_win), col[p]]) + bias_ref[delta, p])
        probs, norms = [], []
        for s in scores:
            e = jnp.exp(s - jnp.max(s, axis=-1, keepdims=True))
            probs.append(e.astype(BF16))
            norms.append(jnp.sum(e, axis=-1, keepdims=True))
        for p in pairs:
            o2 = jnp.dot(probs[p], vbuf[pl.ds(start, n_win), col[p]], preferred_element_type=F32) / norms[p]
            abuf[jr * GRID_W:(jr + 1) * GRID_W, col[p]] = jnp.where(low_half, o2[:GRID_W], o2[GRID_W:])

    first = blk_r0 == seq_r0
    last = blk_r0 + ROW_BLOCK == seq_r1
    cu = gc_ref[...] * gu_ref[...]
    prev_row = jnp.where(first, 0.0, gcp_ref[HALO - 1:HALO, :] * gup_ref[HALO - 1:HALO, :])
    next_row = jnp.where(last, 0.0, gcn_ref[0:1, :] * gun_ref[0:1, :])
    row = lax.broadcasted_iota(jnp.int32, (TOK_BLOCK, CONV_W), 0)
    up_prev = jnp.where(row == 0, prev_row, pltpu.roll(cu, 1, axis=0))
    up_next = jnp.where(row == TOK_BLOCK - 1, next_row, pltpu.roll(cu, TOK_BLOCK - 1, axis=0))
    conv = gb_ref[...] * (up_prev * cw_ref[0:1, :] + cu * cw_ref[1:2, :] + up_next * cw_ref[2:3, :])

    gmat = gmat_ref[...]
    attn_n = _group_norm(abuf[...], gmat, ga_ref[...]).astype(BF16)
    conv_n = _group_norm(conv, gmat, gcv_ref[...]).astype(BF16)
    y = (jnp.dot(attn_n, wo_ref[0:ATTN_W, :], preferred_element_type=F32)
         + jnp.dot(conv_n, wo_ref[ATTN_W:D_MODEL, :], preferred_element_type=F32))
    o_ref[...] = _pick_part(xa_ref, xb_ref) + y


def _mixer(x, q, k, v, gb, gc, gu, bias, layer, conv_w, g_attn, g_conv, gmat, w_out):
    xa, xb, joint = _parts(x)
    cur = lambda i: (i, 0)
    prev = lambda i: (jnp.maximum(i - 1, 0), 0)
    nxt = lambda i: (jnp.minimum(i + 1, N_BLOCKS - 1), 0)
    halo_per_block = TOK_BLOCK // HALO
    hprev = lambda i: (jnp.maximum(i * halo_per_block - 1, 0), 0)
    hnext = lambda i: (jnp.minimum((i + 1) * halo_per_block, N_TOK // HALO - 1), 0)
    const2 = lambda i: (0, 0)
    blk = lambda m: pl.BlockSpec((TOK_BLOCK, ATTN_W), m)
    halo = lambda m: pl.BlockSpec((HALO, CONV_W), m)
    return pl.pallas_call(
        _mixer_kernel,
        grid=(N_BLOCKS,),
        in_specs=[*_split_specs(D_MODEL, joint),
                  blk(cur), blk(prev), blk(cur), blk(nxt), blk(prev), blk(cur), blk(nxt),
                  blk(cur), blk(cur), blk(cur), halo(hprev), halo(hprev), halo(hnext), halo(hnext),
                  pl.BlockSpec((None, WIN_ROWS, ATTN_HEADS // 2, 2 * GRID_W, WIN_ROWS * GRID_W),
                               lambda i: (layer, 0, 0, 0, 0)),
                  pl.BlockSpec((3, CONV_W), const2),
                  pl.BlockSpec((1, ATTN_W), const2),
                  pl.BlockSpec((1, CONV_W), const2),
                  pl.BlockSpec((ATTN_W, ATTN_W), const2),
                  pl.BlockSpec((D_MODEL, D_MODEL), const2)],
        out_specs=pl.BlockSpec((TOK_BLOCK, D_MODEL), cur),
        out_shape=jax.ShapeDtypeStruct((N_TOK, D_MODEL), F32),
        scratch_shapes=[pltpu.VMEM((3 * TOK_BLOCK, ATTN_W), BF16),
                        pltpu.VMEM((3 * TOK_BLOCK, ATTN_W), BF16),
                        pltpu.VMEM((TOK_BLOCK, ATTN_W), F32)],
        compiler_params=pltpu.CompilerParams(dimension_semantics=("arbitrary",),
                                             vmem_limit_bytes=VMEM_LIMIT),
        name="mixer",
    )(xa, xb, q, k, k, k, v, v, v, gb, gc, gu, gc, gu, gc, gu, bias, conv_w, g_attn, g_conv, gmat, w_out)


def _record(k, kidx, m, pick, vals, picks):
    sel = kidx == k
    return jnp.where(sel, m, vals), jnp.where(sel, pick, picks)


def _top_keys(s):
    n, t = s.shape
    half = n // 2
    r0 = lax.broadcasted_iota(jnp.int32, (half, t), 0).astype(F32)
    r1 = r0 + float(half)
    first = s[:half] >= s[half:]
    hi = jnp.where(first, s[:half], s[half:])
    lo = jnp.where(first, s[half:], s[:half])
    ihi = jnp.where(first, r0, r1)
    ilo = jnp.where(first, r1, r0)
    kidx = lax.broadcasted_iota(jnp.int32, (PEER_TOPK, t), 0)

    def body(k, carry):
        hi, lo, ihi, vals, picks = carry
        m = jnp.max(hi, axis=0, keepdims=True)
        pos = jnp.min(jnp.where(hi == m, ihi, float(n)), axis=0, keepdims=True)
        onehot = ihi == pos
        vals, picks = _record(k, kidx, m, pos, vals, picks)
        return (jnp.where(onehot, lo, hi), jnp.where(onehot, -jnp.inf, lo), jnp.where(onehot, ilo, ihi),
                vals, picks)

    zeros = jnp.zeros((PEER_TOPK, t), F32)
    out = lax.fori_loop(0, PEER_TOPK, body, (hi, lo, ihi, zeros, zeros), unroll=True)
    return out[3], out[4]


_CAND_PER_KA = [PEER_TOPK // (ka + 1) for ka in range(PEER_TOPK)]
_N_CAND = sum(_CAND_PER_KA)
_CAND_PAD = -_N_CAND % 8


def _top_candidates(v1, i1, v2, i2):
    t = v1.shape[1]
    s_rows, e_rows = [], []
    for ka, n_kb in enumerate(_CAND_PER_KA):
        s_rows.append(v1[ka:ka + 1, :] + v2[:n_kb, :])
        e_rows.append(i1[ka:ka + 1, :] * float(N_KEYS) + i2[:n_kb, :])
    s_rows.append(jnp.full((_CAND_PAD, t), -jnp.inf, F32))
    e_rows.append(jnp.zeros((_CAND_PAD, t), F32))
    s = jnp.concatenate(s_rows, axis=0)
    e = jnp.concatenate(e_rows, axis=0)
    rows = s.shape[0]
    ridx = lax.broadcasted_iota(jnp.int32, (rows, t), 0).astype(F32)
    kidx = lax.broadcasted_iota(jnp.int32, (PEER_TOPK, t), 0)

    def body(k, carry):
        s, vals, picks = carry
        m = jnp.max(s, axis=0, keepdims=True)
        pos = jnp.min(jnp.where(s == m, ridx, float(rows)), axis=0, keepdims=True)
        onehot = ridx == pos
        pick = jnp.max(jnp.where(onehot, e, -1.0), axis=0, keepdims=True)
        vals, picks = _record(k, kidx, m, pick, vals, picks)
        return jnp.where(onehot, -jnp.inf, s), vals, picks

    zeros = jnp.zeros((PEER_TOPK, t), F32)
    _, vals, picks = lax.fori_loop(0, PEER_TOPK, body, (s, zeros, zeros), unroll=True)
    return vals, picks


def _gelu_exact(x, half=0.5):
    return half * x * (1.0 + lax.erf(x * (2.0 ** -0.5)))


def _key_scores(qh, k1, k2):
    half = D_KEY // 2
    return _dot_nt(k1, qh[:, :half]), _dot_nt(k2, qh[:, half:])


def _head_queries(hb, wq):
    return jnp.dot(hb, wq, preferred_element_type=F32).astype(BF16)


def _peer_kernel(x0_ref, xr_ref, xp_ref, g_ref, wq_ref, k1_ref, k2_ref, ut_ref, v_ref, o_ref,
                 h_buf, qh_buf, sc_buf, gt_buf, et_buf, a_buf, b_buf, gate_buf, wtok, acc):
    i = pl.program_id(0)
    c = pl.program_id(1)
    last_c = c == PEER_N_CHUNKS - 1
    route_slot = i % 2
    peer_slot = 1 - route_slot
    score_slot = c % 2
    h_peer_slot = (i + 2) % 3
    h_new_slot = (i + 1) % 3

    @pl.when(jnp.logical_and(i == 0, c == 0))
    def _():
        hb = _rms(x0_ref[...], g_ref[...]).astype(BF16)
        h_buf[0] = hb
        for h in range(PEER_HEADS):
            qh_buf[0, h] = _head_queries(hb, wq_ref[h])
        s1, s2 = _key_scores(qh_buf[0, 0], k1_ref[...], k2_ref[...])
        sc_buf[0, 0] = s1
        sc_buf[0, 1] = s2
        h_buf[2] = jnp.zeros((PEER_TOK, D_MODEL), BF16)
        gt_buf[1] = jnp.zeros((PEER_HEADS * PEER_TOPK, PEER_TOK), F32)
        et_buf[1] = jnp.zeros((PEER_HEADS * PEER_TOPK, PEER_TOK), F32)

    @pl.when(c == 0)
    def _():
        h_buf[h_new_slot] = _rms(xr_ref[...], g_ref[...]).astype(BF16)
        acc[...] = jnp.zeros_like(acc)
        e = et_buf[peer_slot].T
        a = jnp.floor(e * (1.0 / N_KEYS))
        a_buf[...] = a
        b_buf[...] = e - a * float(N_KEYS)
        gate_buf[...] = gt_buf[peer_slot].T
        sub = lax.broadcasted_iota(jnp.int32, (N_KEYS, LANES), 0).astype(F32)

        def tok_group(grp):
            t0 = pl.multiple_of(grp * TOK_GROUP, TOK_GROUP)
            slabs = []
            for j in range(TOK_GROUP):
                ar = a_buf[pl.ds(t0 + j, 1), :]
                br = b_buf[pl.ds(t0 + j, 1), :]
                gr = gate_buf[pl.ds(t0 + j, 1), :]
                pg = jnp.where(sub == ar, gr, 0.0).astype(BF16)
                qb = jnp.where(sub == br, 1.0, 0.0).astype(BF16)
                slabs.append(_dot_nt(pg, qb))
            wtok[:, pl.ds(t0, TOK_GROUP), :] = jnp.swapaxes(jnp.stack(slabs), 0, 1)

        groups_per_head = PEER_TOK // TOK_GROUP // PEER_HEADS

        def slab_trip(h, carry):
            for u in range(groups_per_head):
                tok_group(h * groups_per_head + u)
            qh_buf[1 - route_slot, h] = _head_queries(h_buf[h_new_slot], wq_ref[h])
            return carry

        lax.fori_loop(0, PEER_HEADS, slab_trip, 0)

    v1, i1 = _top_keys(sc_buf[score_slot, 0])
    v2, i2 = _top_keys(sc_buf[score_slot, 1])

    next_slot = jnp.where(last_c, 1 - route_slot, route_slot)
    next_head = jnp.where(last_c, 0, c + 1)
    s1, s2 = _key_scores(qh_buf[next_slot, next_head], k1_ref[...], k2_ref[...])
    sc_buf[1 - score_slot, 0] = s1
    sc_buf[1 - score_slot, 1] = s2

    a_act = jnp.dot(h_buf[h_peer_slot], ut_ref[...], preferred_element_type=F32)
    key0 = c * PEER_CHUNK_KEYS
    w = jnp.concatenate([wtok[key0 + j] for j in range(PEER_CHUNK_KEYS)], axis=1)
    half = 0.5 + jnp.where(i1[PEER_TOPK - 1:, 0:1] + i2[PEER_TOPK - 1:, 0:1] < -1.0, 1.0, 0.0)
    wg = w * _gelu_exact(a_act, half)
    acc[...] += jnp.dot(wg.astype(BF16), v_ref[...], preferred_element_type=F32)

    top_s, top_e = _top_candidates(v1, i1, v2, i2)
    ex = jnp.exp(top_s - jnp.max(top_s, axis=0, keepdims=True))
    rows = pl.ds(pl.multiple_of(c * PEER_TOPK, PEER_TOPK), PEER_TOPK)
    gt_buf[route_slot, rows, :] = ex / jnp.sum(ex, axis=0, keepdims=True)
    et_buf[route_slot, rows, :] = top_e

    @pl.when(last_c)
    def _():
        o_ref[...] = xp_ref[...] + acc[...]


def _peer(x, g, wq_heads, keys1, keys2, eut, ev, layer):
    n_blk = N_TOK // PEER_TOK
    hk = PEER_HEADS * PEER_TOPK
    new_blk = lambda i, c: (jnp.minimum(i + 1, n_blk - 1), 0)
    peer_blk = lambda i, c: (jnp.maximum(i - 1, 0), 0)
    const2 = lambda i, c: (0, 0)
    chunk = pl.BlockSpec((None, PEER_CHUNK, D_MODEL), lambda i, c: (layer, c, 0))
    chunk_t = pl.BlockSpec((None, D_MODEL, PEER_CHUNK), lambda i, c: (layer, 0, c))
    return pl.pallas_call(
        _peer_kernel,
        grid=(n_blk + 1, PEER_N_CHUNKS),
        in_specs=[pl.BlockSpec((PEER_TOK, D_MODEL), const2),
                  pl.BlockSpec((PEER_TOK, D_MODEL), new_blk),
                  pl.BlockSpec((PEER_TOK, D_MODEL), peer_blk),
                  pl.BlockSpec((1, D_MODEL), const2),
                  pl.BlockSpec((PEER_HEADS, D_MODEL, D_KEY), lambda i, c: (0, 0, 0)),
                  pl.BlockSpec((N_KEYS, D_KEY // 2), const2),
                  pl.BlockSpec((N_KEYS, D_KEY // 2), const2),
                  chunk_t, chunk],
        out_specs=pl.BlockSpec((PEER_TOK, D_MODEL), peer_blk),
        out_shape=jax.ShapeDtypeStruct((N_TOK, D_MODEL), F32),
        scratch_shapes=[pltpu.VMEM((3, PEER_TOK, D_MODEL), BF16),
                        pltpu.VMEM((2, PEER_HEADS, PEER_TOK, D_KEY), BF16),
                        pltpu.VMEM((2, 2, N_KEYS, PEER_TOK), F32),
                        pltpu.VMEM((2, hk, PEER_TOK), F32),
                        pltpu.VMEM((2, hk, PEER_TOK), F32),
                        pltpu.VMEM((PEER_TOK, hk), F32),
                        pltpu.VMEM((PEER_TOK, hk), F32),
                        pltpu.VMEM((PEER_TOK, hk), F32),
                        pltpu.VMEM((N_KEYS, PEER_TOK, N_KEYS), F32),
                        pltpu.VMEM((PEER_TOK, D_MODEL), F32)],
        compiler_params=pltpu.CompilerParams(dimension_semantics=("arbitrary", "arbitrary"),
                                             vmem_limit_bytes=PEER_VMEM_LIMIT),
        name="peer",
    )(x, x, x, g, wq_heads, keys1, keys2, eut, ev)


def _ple_kernel(x_ref, pa_ref, pb_ref, g_ref, wg_ref, wp_ref, gf_ref, *o_refs, final):
    x = x_ref[...]
    hp = _rms(x, g_ref[...]).astype(BF16)
    gate = jax.nn.sigmoid(jnp.dot(hp, wg_ref[...], preferred_element_type=F32))
    proj = jnp.dot(_pick_part(pa_ref, pb_ref).astype(BF16), wp_ref[...], preferred_element_type=F32)
    y = x + gate * proj
    if not final:
        o_refs[0][...] = y
        return
    y = _rms(y, gf_ref[...])
    is_prompt = pl.program_id(0) < PROMPT_BLOCKS

    @pl.when(is_prompt)
    def _():
        o_refs[0][...] = y

    @pl.when(jnp.logical_not(is_prompt))
    def _():
        o_refs[1][...] = y


def _ple(x, p_prompt, p_sample, layer, g, w_gate, w_proj, g_final, final):
    const2 = lambda i: (0, 0)
    tok = pl.BlockSpec((TOK_BLOCK, D_MODEL), lambda i: (i, 0))
    p_specs = (pl.BlockSpec((None, TOK_BLOCK, PLE_DIM), lambda i: (layer, jnp.minimum(i, PROMPT_BLOCKS - 1), 0)),
               pl.BlockSpec((None, TOK_BLOCK, PLE_DIM), lambda i: (layer, jnp.maximum(i - PROMPT_BLOCKS, 0), 0)))
    if final:
        out_specs = list(_split_specs(D_MODEL, joint=False))
        out_shape = [jax.ShapeDtypeStruct((SEQ, D_MODEL), F32), jax.ShapeDtypeStruct((N_TOK - SEQ, D_MODEL), F32)]
    else:
        out_specs, out_shape = tok, jax.ShapeDtypeStruct((N_TOK, D_MODEL), F32)
    return pl.pallas_call(
        functools.partial(_ple_kernel, final=final),
        grid=(N_BLOCKS,),
        in_specs=[tok, *p_specs,
                  pl.BlockSpec((1, D_MODEL), const2),
                  pl.BlockSpec((D_MODEL, D_MODEL), const2),
                  pl.BlockSpec((PLE_DIM, D_MODEL), const2),
                  pl.BlockSpec((1, D_MODEL), const2)],
        out_specs=out_specs,
        out_shape=out_shape,
        compiler_params=pltpu.CompilerParams(dimension_semantics=("arbitrary",),
                                             vmem_limit_bytes=VMEM_LIMIT),
        name="ple",
    )(x, p_prompt, p_sample, g, w_gate, w_proj, g_final)


def _bias_tables(rpb):
    cols = np.arange(GRID_W)
    col_start = np.clip(cols - WIN_COLS // 2, 0, GRID_W - WIN_COLS)
    kc = np.arange(GRID_W)
    in_win = (kc[None, :] >= col_start[:, None]) & (kc[None, :] < col_start[:, None] + WIN_COLS)
    col_off = kc[None, :] - cols[:, None] + (WIN_COLS - 1)
    onehot = (col_off[None] == np.arange(2 * WIN_COLS - 1)[:, None, None]) & in_win[None]
    delta = np.arange(WIN_ROWS)
    j = np.arange(WIN_ROWS)
    row_off = j[None, :] - delta[:, None] + (WIN_ROWS - 1)
    rows = rpb[:, :, row_off]
    t = jnp.einsum('lhdjo,ock->ldhcjk', rows, jnp.asarray(onehot, F32), precision=lax.Precision.HIGHEST)
    t = t + jnp.asarray(np.where(in_win, 0.0, NEG_BIG), F32)[None, None, None, :, None, :]
    return t.reshape(DEPTH, WIN_ROWS, ATTN_HEADS // 2, 2 * GRID_W, WIN_ROWS * GRID_W)


def kernel(x_prompt, x_sample, p_prompt, p_sample, g_mix, w_in, rpb, conv_w, g_attn_out, g_conv_out, w_out,
           g_ffn, w_query, sub_keys1, sub_keys2, expert_u, expert_v, g_ple, w_ple_gate, w_ple_proj, g_final):
    x = (x_prompt.reshape(SEQ, D_MODEL), x_sample.reshape(DEC_BATCH * DEC_SEQ, D_MODEL))
    pp = p_prompt.reshape(DEPTH, SEQ, PLE_DIM)
    ps = p_sample.reshape(DEPTH, DEC_BATCH * DEC_SEQ, PLE_DIM)
    group = np.arange(ATTN_W) // HEAD_DIM
    gmat = jnp.asarray(group[:, None] == group[None, :], dtype=BF16)
    row = lambda g: g.reshape(1, -1)
    bias = _bias_tables(rpb)
    eut, ev = jnp.swapaxes(expert_u.astype(BF16), 1, 2), expert_v.astype(BF16)
    for i in range(DEPTH):
        q, k, v, gb, gc, gu = _proj(x, row(g_mix[i]), w_in[i].astype(BF16))
        x = _mixer(x, q, k, v, gb, gc, gu, bias, i, conv_w[i], row(g_attn_out[i]),
                   row(g_conv_out[i]), gmat, w_out[i].astype(BF16))
        wq_heads = w_query[i].astype(BF16).reshape(D_MODEL, PEER_HEADS, D_KEY).swapaxes(0, 1)
        x = _peer(x, row(g_ffn[i]), wq_heads, sub_keys1[i].astype(BF16),
                  sub_keys2[i].astype(BF16), eut, ev, i)
        x = _ple(x, pp, ps, i, row(g_ple[i]), w_ple_gate[i].astype(BF16), w_ple_proj[i].astype(BF16),
                 row(g_final), final=(i == DEPTH - 1))
    y_prompt, y_sample = x
    return (y_prompt.reshape(1, SEQ, D_MODEL), y_sample.reshape(DEC_BATCH, DEC_SEQ, D_MODEL))
```

```python
import functools

import numpy as np
import jax
import jax.numpy as jnp
from jax import lax
from jax.experimental import pallas as pl
from jax.experimental.pallas import tpu as pltpu

F32 = jnp.float32
BF16 = jnp.bfloat16

D_MODEL = 1024
DEPTH = 2
SEQ = 16384
DEC_BATCH = 4
DEC_SEQ = 4096
N_TOK = SEQ + DEC_BATCH * DEC_SEQ

GRID_W = 64
WIN_ROWS = 8
WIN_COLS = 16
ATTN_HEADS = 8
HEAD_DIM = 64
ATTN_W = ATTN_HEADS * HEAD_DIM
CONV_W = D_MODEL - ATTN_W
N_KEYS = 128
N_EXPERTS = N_KEYS * N_KEYS
PEER_HEADS = 8
D_KEY = 256
PEER_TOPK = 16
PLE_DIM = 256
EPS = 1e-6
NEG_BIG = -1e30

LANES = 128
ROW_BLOCK = WIN_ROWS
TOK_BLOCK = ROW_BLOCK * GRID_W
N_BLOCKS = N_TOK // TOK_BLOCK
PROMPT_ROWS = SEQ // GRID_W
SAMPLE_ROWS = DEC_SEQ // GRID_W
PROMPT_BLOCKS = PROMPT_ROWS // ROW_BLOCK
SAMPLE_BLOCKS = SAMPLE_ROWS // ROW_BLOCK
HALO = 8

PEER_TOK = 256
TOK_GROUP = 16
PEER_CHUNK_KEYS = 16
PEER_CHUNK = PEER_CHUNK_KEYS * N_KEYS
PEER_N_CHUNKS = N_EXPERTS // PEER_CHUNK
assert PEER_N_CHUNKS == PEER_HEADS

VMEM_LIMIT = 48 * 1024 * 1024
PEER_VMEM_LIMIT = 60 * 1024 * 1024


def _rms(x, g):
    return x * lax.rsqrt(jnp.mean(x * x, axis=-1, keepdims=True) + EPS) * g


def _dot_nt(a, b):
    return lax.dot_general(a, b, (((1,), (1,)), ((), ())), preferred_element_type=F32)


def _split_specs(width, joint):
    off = PROMPT_BLOCKS if joint else 0
    return (pl.BlockSpec((TOK_BLOCK, width), lambda i: (jnp.minimum(i, PROMPT_BLOCKS - 1), 0)),
            pl.BlockSpec((TOK_BLOCK, width), lambda i: (jnp.maximum(i - PROMPT_BLOCKS, 0) + off, 0)))


def _pick_part(prompt_ref, sample_ref):
    return jnp.where(pl.program_id(0) < PROMPT_BLOCKS, prompt_ref[...], sample_ref[...])


def _parts(x):
    return (x[0], x[1], False) if isinstance(x, tuple) else (x, x, True)


def _proj_kernel(xa_ref, xb_ref, g_ref, w_ref, q_ref, k_ref, v_ref, gb_ref, gc_ref, gu_ref):
    hb = _rms(_pick_part(xa_ref, xb_ref), g_ref[...]).astype(BF16)
    outs = (q_ref, k_ref, v_ref, gb_ref, gc_ref, gu_ref)
    for j, o_ref in enumerate(outs):
        z = jnp.dot(hb, w_ref[:, j * ATTN_W:(j + 1) * ATTN_W], preferred_element_type=F32)
        if j == 0:
            z = z * (HEAD_DIM ** -0.5)
        o_ref[...] = z.astype(o_ref.dtype)


def _proj(x, g, w_in):
    xa, xb, joint = _parts(x)
    tok = pl.BlockSpec((TOK_BLOCK, ATTN_W), lambda i: (i, 0))
    return pl.pallas_call(
        _proj_kernel,
        grid=(N_BLOCKS,),
        in_specs=[*_split_specs(D_MODEL, joint),
                  pl.BlockSpec((1, D_MODEL), lambda i: (0, 0)),
                  pl.BlockSpec((D_MODEL, 6 * ATTN_W), lambda i: (0, 0))],
        out_specs=[tok] * 6,
        out_shape=[jax.ShapeDtypeStruct((N_TOK, ATTN_W), BF16)] * 3
                  + [jax.ShapeDtypeStruct((N_TOK, ATTN_W), F32)] * 3,
        compiler_params=pltpu.CompilerParams(dimension_semantics=("arbitrary",),
                                             vmem_limit_bytes=VMEM_LIMIT),
        name="proj",
    )(xa, xb, g, w_in)


def _group_norm(xv, gmat, g):
    sq = xv * xv
    hi = sq.astype(BF16)
    lo = (sq - hi.astype(F32)).astype(BF16)
    ms = (jnp.dot(hi, gmat, preferred_element_type=F32)
          + jnp.dot(lo, gmat, preferred_element_type=F32)) * (1.0 / HEAD_DIM)
    return xv * lax.rsqrt(ms + EPS) * g


def _mixer_kernel(xa_ref, xb_ref, q_ref, kp_ref, kc_ref, kn_ref, vp_ref, vc_ref, vn_ref,
                  gb_ref, gc_ref, gu_ref, gcp_ref, gup_ref, gcn_ref, gun_ref,
                  bias_ref, cw_ref, ga_ref, gcv_ref, gmat_ref, wo_ref,
                  o_ref, kbuf, vbuf, abuf):
    i = pl.program_id(0)
    is_prompt = i < PROMPT_BLOCKS
    sample = (i - PROMPT_BLOCKS) // SAMPLE_BLOCKS
    seq_r0 = jnp.where(is_prompt, 0, PROMPT_ROWS + SAMPLE_ROWS * sample)
    seq_r1 = jnp.where(is_prompt, PROMPT_ROWS, seq_r0 + SAMPLE_ROWS)
    blk_r0 = ROW_BLOCK * i

    kbuf[0:TOK_BLOCK, :] = kp_ref[...]
    kbuf[TOK_BLOCK:2 * TOK_BLOCK, :] = kc_ref[...]
    kbuf[2 * TOK_BLOCK:3 * TOK_BLOCK, :] = kn_ref[...]
    vbuf[0:TOK_BLOCK, :] = vp_ref[...]
    vbuf[TOK_BLOCK:2 * TOK_BLOCK, :] = vc_ref[...]
    vbuf[2 * TOK_BLOCK:3 * TOK_BLOCK, :] = vn_ref[...]

    lane = lax.broadcasted_iota(jnp.int32, (GRID_W, LANES), 1)
    low_half = lane < HEAD_DIM
    n_win = WIN_ROWS * GRID_W
    for jr in range(ROW_BLOCK):
        r = blk_r0 + jr
        rs = jnp.clip(r - WIN_ROWS // 2, seq_r0, seq_r1 - WIN_ROWS)
        delta = r - rs
        start = pl.multiple_of((rs - blk_r0 + ROW_BLOCK) * GRID_W, GRID_W)
        pairs = range(ATTN_HEADS // 2)
        col = [slice(p * LANES, (p + 1) * LANES) for p in pairs]
        scores = []
        for p in pairs:
            qp = q_ref[jr * GRID_W:(jr + 1) * GRID_W, col[p]]
            zero = jnp.zeros_like(qp)
            q2 = jnp.concatenate([jnp.where(low_half, qp, zero), jnp.where(low_half, zero, qp)], axis=0)
            scores.append(_dot_nt(q2, kbuf[pl.ds(start, n_win), col[p]]) + bias_ref[delta, p])
        probs, norms = [], []
        for s in scores:
            e = jnp.exp(s - jnp.max(s, axis=-1, keepdims=True))
            probs.append(e.astype(BF16))
            norms.append(jnp.sum(e, axis=-1, keepdims=True))
        for p in pairs:
            o2 = jnp.dot(probs[p], vbuf[pl.ds(start, n_win), col[p]], preferred_element_type=F32) / norms[p]
            abuf[jr * GRID_W:(jr + 1) * GRID_W, col[p]] = jnp.where(low_half, o2[:GRID_W], o2[GRID_W:])

    first = blk_r0 == seq_r0
    last = blk_r0 + ROW_BLOCK == seq_r1
    cu = gc_ref[...] * gu_ref[...]
    prev_row = jnp.where(first, 0.0, gcp_ref[HALO - 1:HALO, :] * gup_ref[HALO - 1:HALO, :])
    next_row = jnp.where(last, 0.0, gcn_ref[0:1, :] * gun_ref[0:1, :])
    row = lax.broadcasted_iota(jnp.int32, (TOK_BLOCK, CONV_W), 0)
    up_prev = jnp.where(row == 0, prev_row, pltpu.roll(cu, 1, axis=0))
    up_next = jnp.where(row == TOK_BLOCK - 1, next_row, pltpu.roll(cu, TOK_BLOCK - 1, axis=0))
    conv = gb_ref[...] * (up_prev * cw_ref[0:1, :] + cu * cw_ref[1:2, :] + up_next * cw_ref[2:3, :])

    gmat = gmat_ref[...]
    attn_n = _group_norm(abuf[...], gmat, ga_ref[...]).astype(BF16)
    conv_n = _group_norm(conv, gmat, gcv_ref[...]).astype(BF16)
    y = (jnp.dot(attn_n, wo_ref[0:ATTN_W, :], preferred_element_type=F32)
         + jnp.dot(conv_n, wo_ref[ATTN_W:D_MODEL, :], preferred_element_type=F32))
    o_ref[...] = _pick_part(xa_ref, xb_ref) + y


def _mixer(x, q, k, v, gb, gc, gu, bias, layer, conv_w, g_attn, g_conv, gmat, w_out):
    xa, xb, joint = _parts(x)
    cur = lambda i: (i, 0)
    prev = lambda i: (jnp.maximum(i - 1, 0), 0)
    nxt = lambda i: (jnp.minimum(i + 1, N_BLOCKS - 1), 0)
    halo_per_block = TOK_BLOCK // HALO
    hprev = lambda i: (jnp.maximum(i * halo_per_block - 1, 0), 0)
    hnext = lambda i: (jnp.minimum((i + 1) * halo_per_block, N_TOK // HALO - 1), 0)
    const2 = lambda i: (0, 0)
    blk = lambda m: pl.BlockSpec((TOK_BLOCK, ATTN_W), m)
    halo = lambda m: pl.BlockSpec((HALO, CONV_W), m)
    return pl.pallas_call(
        _mixer_kernel,
        grid=(N_BLOCKS,),
        in_specs=[*_split_specs(D_MODEL, joint),
                  blk(cur), blk(prev), blk(cur), blk(nxt), blk(prev), blk(cur), blk(nxt),
                  blk(cur), blk(cur), blk(cur), halo(hprev), halo(hprev), halo(hnext), halo(hnext),
                  pl.BlockSpec((None, WIN_ROWS, ATTN_HEADS // 2, 2 * GRID_W, WIN_ROWS * GRID_W),
                               lambda i: (layer, 0, 0, 0, 0)),
                  pl.BlockSpec((3, CONV_W), const2),
                  pl.BlockSpec((1, ATTN_W), const2),
                  pl.BlockSpec((1, CONV_W), const2),
                  pl.BlockSpec((ATTN_W, ATTN_W), const2),
                  pl.BlockSpec((D_MODEL, D_MODEL), const2)],
        out_specs=pl.BlockSpec((TOK_BLOCK, D_MODEL), cur),
        out_shape=jax.ShapeDtypeStruct((N_TOK, D_MODEL), F32),
        scratch_shapes=[pltpu.VMEM((3 * TOK_BLOCK, ATTN_W), BF16),
                        pltpu.VMEM((3 * TOK_BLOCK, ATTN_W), BF16),
                        pltpu.VMEM((TOK_BLOCK, ATTN_W), F32)],
        compiler_params=pltpu.CompilerParams(dimension_semantics=("arbitrary",),
                                             vmem_limit_bytes=VMEM_LIMIT),
        name="mixer",
    )(xa, xb, q, k, k, k, v, v, v, gb, gc, gu, gc, gu, gc, gu, bias, conv_w, g_attn, g_conv, gmat, w_out)


def _record(k, kidx, m, pick, vals, picks):
    sel = kidx == k
    return jnp.where(sel, m, vals), jnp.where(sel, pick, picks)


def _top_keys(s):
    n, t = s.shape
    half = n // 2
    r0 = lax.broadcasted_iota(jnp.int32, (half, t), 0).astype(F32)
    r1 = r0 + float(half)
    first = s[:half] >= s[half:]
    hi = jnp.where(first, s[:half], s[half:])
    lo = jnp.where(first, s[half:], s[:half])
    ihi = jnp.where(first, r0, r1)
    ilo = jnp.where(first, r1, r0)
    kidx = lax.broadcasted_iota(jnp.int32, (PEER_TOPK, t), 0)

    def body(k, carry):
        hi, lo, ihi, vals, picks = carry
        m = jnp.max(hi, axis=0, keepdims=True)
        pos = jnp.min(jnp.where(hi == m, ihi, float(n)), axis=0, keepdims=True)
        onehot = ihi == pos
        vals, picks = _record(k, kidx, m, pos, vals, picks)
        return (jnp.where(onehot, lo, hi), jnp.where(onehot, -jnp.inf, lo), jnp.where(onehot, ilo, ihi),
                vals, picks)

    zeros = jnp.zeros((PEER_TOPK, t), F32)
    out = lax.fori_loop(0, PEER_TOPK, body, (hi, lo, ihi, zeros, zeros), unroll=True)
    return out[3], out[4]


_CAND_PER_KA = [PEER_TOPK // (ka + 1) for ka in range(PEER_TOPK)]
_N_CAND = sum(_CAND_PER_KA)
_CAND_PAD = -_N_CAND % 8


def _top_candidates(v1, i1, v2, i2):
    t = v1.shape[1]
    s_rows, e_rows = [], []
    for ka, n_kb in enumerate(_CAND_PER_KA):
        s_rows.append(v1[ka:ka + 1, :] + v2[:n_kb, :])
        e_rows.append(i1[ka:ka + 1, :] * float(N_KEYS) + i2[:n_kb, :])
    s_rows.append(jnp.full((_CAND_PAD, t), -jnp.inf, F32))
    e_rows.append(jnp.zeros((_CAND_PAD, t), F32))
    s = jnp.concatenate(s_rows, axis=0)
    e = jnp.concatenate(e_rows, axis=0)
    rows = s.shape[0]
    ridx = lax.broadcasted_iota(jnp.int32, (rows, t), 0).astype(F32)
    kidx = lax.broadcasted_iota(jnp.int32, (PEER_TOPK, t), 0)

    def body(k, carry):
        s, vals, picks = carry
        m = jnp.max(s, axis=0, keepdims=True)
        pos = jnp.min(jnp.where(s == m, ridx, float(rows)), axis=0, keepdims=True)
        onehot = ridx == pos
        pick = jnp.max(jnp.where(onehot, e, -1.0), axis=0, keepdims=True)
        vals, picks = _record(k, kidx, m, pick, vals, picks)
        return jnp.where(onehot, -jnp.inf, s), vals, picks

    zeros = jnp.zeros((PEER_TOPK, t), F32)
    _, vals, picks = lax.fori_loop(0, PEER_TOPK, body, (s, zeros, zeros), unroll=True)
    return vals, picks


def _gelu_exact(x, half=0.5):
    return half * x * (1.0 + lax.erf(x * (2.0 ** -0.5)))


def _key_scores(hb, wq, k1, k2):
    half = D_KEY // 2
    qh = jnp.dot(hb, wq, preferred_element_type=F32).astype(BF16)
    return _dot_nt(k1, qh[:, :half]), _dot_nt(k2, qh[:, half:])


def _peer_kernel(x0_ref, xr_ref, xp_ref, g_ref, wq0_ref, wq_ref, k1_ref, k2_ref, ut_ref, v_ref, o_ref,
                 h_buf, sc_buf, gt_buf, et_buf, a_buf, b_buf, gate_buf, wtok, acc):
    i = pl.program_id(0)
    c = pl.program_id(1)
    s = pl.program_id(2)
    last_c = c == PEER_N_CHUNKS - 1
    route_unit = 2 * i + s
    route_slot = route_unit % 4
    done_slot = (route_unit + 2) % 4
    h_done_slot = (route_unit + 4) % 6

    @pl.when(jnp.logical_and(i == 0, jnp.logical_and(c == 0, s == 0)))
    def _():
        hb = _rms(x0_ref[...], g_ref[...]).astype(BF16)
        h_buf[0] = hb
        h_buf[1] = _rms(xr_ref[...], g_ref[...]).astype(BF16)
        s1, s2 = _key_scores(hb, wq0_ref[...], k1_ref[...], k2_ref[...])
        sc_buf[0, 0] = s1
        sc_buf[0, 1] = s2
        for u in (2, 3):
            gt_buf[u] = jnp.zeros((PEER_HEADS * PEER_TOPK, PEER_TOK), F32)
            et_buf[u] = jnp.zeros((PEER_HEADS * PEER_TOPK, PEER_TOK), F32)
        for u in (4, 5):
            h_buf[u] = jnp.zeros((PEER_TOK, D_MODEL), BF16)

    @pl.when(last_c)
    def _():
        h_buf[(route_unit + 2) % 6] = _rms(xr_ref[...], g_ref[...]).astype(BF16)

    @pl.when(c == 0)
    def _():
        acc[s] = jnp.zeros((PEER_TOK, D_MODEL), F32)
        e = et_buf[done_slot].T
        a = jnp.floor(e * (1.0 / N_KEYS))
        a_buf[...] = a
        b_buf[...] = e - a * float(N_KEYS)
        gate_buf[...] = gt_buf[done_slot].T
        sub = lax.broadcasted_iota(jnp.int32, (N_KEYS, LANES), 0).astype(F32)

        def tok_group(grp, carry):
            t0 = pl.multiple_of(grp * TOK_GROUP, TOK_GROUP)
            slabs = []
            for j in range(TOK_GROUP):
                ar = a_buf[pl.ds(t0 + j, 1), :]
                br = b_buf[pl.ds(t0 + j, 1), :]
                gr = gate_buf[pl.ds(t0 + j, 1), :]
                pg = jnp.where(sub == ar, gr, 0.0).astype(BF16)
                qb = jnp.where(sub == br, 1.0, 0.0).astype(BF16)
                slabs.append(_dot_nt(pg, qb))
            wtok[s, :, pl.ds(t0, TOK_GROUP), :] = jnp.swapaxes(jnp.stack(slabs), 0, 1).astype(BF16)
            return carry

        lax.fori_loop(0, PEER_TOK // TOK_GROUP, tok_group, 0, unroll=2)

    v1, i1 = _top_keys(sc_buf[s, 0])
    v2, i2 = _top_keys(sc_buf[s, 1])

    next_unit = jnp.where(s == 0, 2 * i + 1, jnp.where(last_c, 2 * i + 2, 2 * i))
    s1, s2 = _key_scores(h_buf[next_unit % 6], wq_ref[...], k1_ref[...], k2_ref[...])
    sc_buf[1 - s, 0] = s1
    sc_buf[1 - s, 1] = s2

    a_act = jnp.dot(h_buf[h_done_slot], ut_ref[...], preferred_element_type=F32)
    key0 = c * PEER_CHUNK_KEYS
    w = jnp.concatenate([wtok[s, key0 + j] for j in range(PEER_CHUNK_KEYS)], axis=1).astype(F32)
    half = 0.5 + jnp.where(i1[PEER_TOPK - 1:, 0:1] + i2[PEER_TOPK - 1:, 0:1] < -1.0, 1.0, 0.0)
    wg = w * _gelu_exact(a_act, half)
    acc[s] += jnp.dot(wg.astype(BF16), v_ref[...], preferred_element_type=F32)

    top_s, top_e = _top_candidates(v1, i1, v2, i2)
    ex = jnp.exp(top_s - jnp.max(top_s, axis=0, keepdims=True))
    rows = pl.ds(pl.multiple_of(c * PEER_TOPK, PEER_TOPK), PEER_TOPK)
    gt_buf[route_slot, rows, :] = ex / jnp.sum(ex, axis=0, keepdims=True)
    et_buf[route_slot, rows, :] = top_e

    @pl.when(last_c)
    def _():
        o_ref[...] = xp_ref[...] + acc[s]


def _peer(x, g, w_query, keys1, keys2, eut, ev, layer):
    n_unit = N_TOK // PEER_TOK
    n_pair = n_unit // 2
    hk = PEER_HEADS * PEER_TOPK
    last = PEER_N_CHUNKS - 1
    route_blk = lambda i, c, s: (jnp.minimum(2 * i + 1 + (c // last) * (1 + s), n_unit - 1), 0)
    done_blk = lambda i, c, s: (jnp.maximum(jnp.where(c == last, 2 * i - 2 + s, 2 * i - 3), 0), 0)
    const2 = lambda i, c, s: (0, 0)
    chunk = pl.BlockSpec((None, PEER_CHUNK, D_MODEL), lambda i, c, s: (layer, c, 0))
    chunk_t = pl.BlockSpec((None, D_MODEL, PEER_CHUNK), lambda i, c, s: (layer, 0, c))
    return pl.pallas_call(
        _peer_kernel,
        grid=(n_pair + 1, PEER_N_CHUNKS, 2),
        in_specs=[pl.BlockSpec((PEER_TOK, D_MODEL), const2),
                  pl.BlockSpec((PEER_TOK, D_MODEL), route_blk),
                  pl.BlockSpec((PEER_TOK, D_MODEL), done_blk),
                  pl.BlockSpec((1, D_MODEL), const2),
                  pl.BlockSpec((D_MODEL, D_KEY), const2),
                  pl.BlockSpec((D_MODEL, D_KEY), lambda i, c, s: (0, (c + s) % PEER_HEADS)),
                  pl.BlockSpec((N_KEYS, D_KEY // 2), const2),
                  pl.BlockSpec((N_KEYS, D_KEY // 2), const2),
                  chunk_t, chunk],
        out_specs=pl.BlockSpec((PEER_TOK, D_MODEL), done_blk),
        out_shape=jax.ShapeDtypeStruct((N_TOK, D_MODEL), F32),
        scratch_shapes=[pltpu.VMEM((6, PEER_TOK, D_MODEL), BF16),
                        pltpu.VMEM((2, 2, N_KEYS, PEER_TOK), F32),
                        pltpu.VMEM((4, hk, PEER_TOK), F32),
                        pltpu.VMEM((4, hk, PEER_TOK), F32),
                        pltpu.VMEM((PEER_TOK, hk), F32),
                        pltpu.VMEM((PEER_TOK, hk), F32),
                        pltpu.VMEM((PEER_TOK, hk), F32),
                        pltpu.VMEM((2, N_KEYS, PEER_TOK, N_KEYS), BF16),
                        pltpu.VMEM((2, PEER_TOK, D_MODEL), F32)],
        compiler_params=pltpu.CompilerParams(dimension_semantics=("arbitrary", "arbitrary", "arbitrary"),
                                             vmem_limit_bytes=PEER_VMEM_LIMIT),
        name="peer",
    )(x, x, x, g, w_query, w_query, keys1, keys2, eut, ev)


def _ple_kernel(x_ref, pa_ref, pb_ref, g_ref, wg_ref, wp_ref, gf_ref, *o_refs, final):
    x = x_ref[...]
    hp = _rms(x, g_ref[...]).astype(BF16)
    gate = jax.nn.sigmoid(jnp.dot(hp, wg_ref[...], preferred_element_type=F32))
    proj = jnp.dot(_pick_part(pa_ref, pb_ref).astype(BF16), wp_ref[...], preferred_element_type=F32)
    y = x + gate * proj
    if not final:
        o_refs[0][...] = y
        return
    y = _rms(y, gf_ref[...])
    is_prompt = pl.program_id(0) < PROMPT_BLOCKS

    @pl.when(is_prompt)
    def _():
        o_refs[0][...] = y

    @pl.when(jnp.logical_not(is_prompt))
    def _():
        o_refs[1][...] = y


def _ple(x, p_prompt, p_sample, layer, g, w_gate, w_proj, g_final, final):
    const2 = lambda i: (0, 0)
    tok = pl.BlockSpec((TOK_BLOCK, D_MODEL), lambda i: (i, 0))
    p_specs = (pl.BlockSpec((None, TOK_BLOCK, PLE_DIM), lambda i: (layer, jnp.minimum(i, PROMPT_BLOCKS - 1), 0)),
               pl.BlockSpec((None, TOK_BLOCK, PLE_DIM), lambda i: (layer, jnp.maximum(i - PROMPT_BLOCKS, 0), 0)))
    if final:
        out_specs = list(_split_specs(D_MODEL, joint=False))
        out_shape = [jax.ShapeDtypeStruct((SEQ, D_MODEL), F32), jax.ShapeDtypeStruct((N_TOK - SEQ, D_MODEL), F32)]
    else:
        out_specs, out_shape = tok, jax.ShapeDtypeStruct((N_TOK, D_MODEL), F32)
    return pl.pallas_call(
        functools.partial(_ple_kernel, final=final),
        grid=(N_BLOCKS,),
        in_specs=[tok, *p_specs,
                  pl.BlockSpec((1, D_MODEL), const2),
                  pl.BlockSpec((D_MODEL, D_MODEL), const2),
                  pl.BlockSpec((PLE_DIM, D_MODEL), const2),
                  pl.BlockSpec((1, D_MODEL), const2)],
        out_specs=out_specs,
        out_shape=out_shape,
        compiler_params=pltpu.CompilerParams(dimension_semantics=("arbitrary",),
                                             vmem_limit_bytes=VMEM_LIMIT),
        name="ple",
    )(x, p_prompt, p_sample, g, w_gate, w_proj, g_final)


def _bias_tables(rpb):
    cols = np.arange(GRID_W)
    col_start = np.clip(cols - WIN_COLS // 2, 0, GRID_W - WIN_COLS)
    kc = np.arange(GRID_W)
    in_win = (kc[None, :] >= col_start[:, None]) & (kc[None, :] < col_start[:, None] + WIN_COLS)
    col_off = kc[None, :] - cols[:, None] + (WIN_COLS - 1)
    onehot = (col_off[None] == np.arange(2 * WIN_COLS - 1)[:, None, None]) & in_win[None]
    delta = np.arange(WIN_ROWS)
    j = np.arange(WIN_ROWS)
    row_off = j[None, :] - delta[:, None] + (WIN_ROWS - 1)
    rows = rpb[:, :, row_off]
    t = jnp.einsum('lhdjo,ock->ldhcjk', rows, jnp.asarray(onehot, F32), precision=lax.Precision.HIGHEST)
    t = t + jnp.asarray(np.where(in_win, 0.0, NEG_BIG), F32)[None, None, None, :, None, :]
    return t.reshape(DEPTH, WIN_ROWS, ATTN_HEADS // 2, 2 * GRID_W, WIN_ROWS * GRID_W)


def kernel(x_prompt, x_sample, p_prompt, p_sample, g_mix, w_in, rpb, conv_w, g_attn_out, g_conv_out, w_out,
           g_ffn, w_query, sub_keys1, sub_keys2, expert_u, expert_v, g_ple, w_ple_gate, w_ple_proj, g_final):
    x = (x_prompt.reshape(SEQ, D_MODEL), x_sample.reshape(DEC_BATCH * DEC_SEQ, D_MODEL))
    pp = p_prompt.reshape(DEPTH, SEQ, PLE_DIM)
    ps = p_sample.reshape(DEPTH, DEC_BATCH * DEC_SEQ, PLE_DIM)
    group = np.arange(ATTN_W) // HEAD_DIM
    gmat = jnp.asarray(group[:, None] == group[None, :], dtype=BF16)
    row = lambda g: g.reshape(1, -1)
    bias = _bias_tables(rpb)
    eut, ev = jnp.swapaxes(expert_u.astype(BF16), 1, 2), expert_v.astype(BF16)
    for i in range(DEPTH):
        q, k, v, gb, gc, gu = _proj(x, row(g_mix[i]), w_in[i].astype(BF16))
        x = _mixer(x, q, k, v, gb, gc, gu, bias, i, conv_w[i], row(g_attn_out[i]),
                   row(g_conv_out[i]), gmat, w_out[i].astype(BF16))
        x = _peer(x, row(g_ffn[i]), w_query[i].astype(BF16), sub_keys1[i].astype(BF16),
                  sub_keys2[i].astype(BF16), eut, ev, i)
        x = _ple(x, pp, ps, i, row(g_ple[i]), w_ple_gate[i].astype(BF16), w_ple_proj[i].astype(BF16),
                 row(g_final), final=(i == DEPTH - 1))
    y_prompt, y_sample = x
    return (y_prompt.reshape(1, SEQ, D_MODEL), y_sample.reshape(DEC_BATCH, DEC_SEQ, D_MODEL))
```

```python
import functools

import numpy as np
import jax
import jax.numpy as jnp
from jax import lax
from jax.experimental import pallas as pl
from jax.experimental.pallas import tpu as pltpu

F32 = jnp.float32
BF16 = jnp.bfloat16

D_MODEL = 1024
DEPTH = 2
SEQ = 16384
DEC_BATCH = 4
DEC_SEQ = 4096
N_TOK = SEQ + DEC_BATCH * DEC_SEQ

GRID_W = 64
WIN_ROWS = 8
WIN_COLS = 16
ATTN_HEADS = 8
HEAD_DIM = 64
ATTN_W = ATTN_HEADS * HEAD_DIM
CONV_W = D_MODEL - ATTN_W
N_KEYS = 128
N_EXPERTS = N_KEYS * N_KEYS
PEER_HEADS = 8
D_KEY = 256
PEER_TOPK = 16
PLE_DIM = 256
EPS = 1e-6
NEG_BIG = -1e30

LANES = 128
ROW_BLOCK = WIN_ROWS
TOK_BLOCK = ROW_BLOCK * GRID_W
N_BLOCKS = N_TOK // TOK_BLOCK
PROMPT_ROWS = SEQ // GRID_W
SAMPLE_ROWS = DEC_SEQ // GRID_W
PROMPT_BLOCKS = PROMPT_ROWS // ROW_BLOCK
SAMPLE_BLOCKS = SAMPLE_ROWS // ROW_BLOCK
HALO = 8

PEER_TOK = 256
TOK_GROUP = 8
PEER_CHUNK_KEYS = 16
PEER_CHUNK = PEER_CHUNK_KEYS * N_KEYS
PEER_N_CHUNKS = N_EXPERTS // PEER_CHUNK
assert PEER_N_CHUNKS == PEER_HEADS

VMEM_LIMIT = 48 * 1024 * 1024
PEER_VMEM_LIMIT = 56 * 1024 * 1024


def _rms(x, g):
    return x * lax.rsqrt(jnp.mean(x * x, axis=-1, keepdims=True) + EPS) * g


def _dot_nt(a, b):
    return lax.dot_general(a, b, (((1,), (1,)), ((), ())), preferred_element_type=F32)


def _split_specs(width, joint):
    off = PROMPT_BLOCKS if joint else 0
    return (pl.BlockSpec((TOK_BLOCK, width), lambda i: (jnp.minimum(i, PROMPT_BLOCKS - 1), 0)),
            pl.BlockSpec((TOK_BLOCK, width), lambda i: (jnp.maximum(i - PROMPT_BLOCKS, 0) + off, 0)))


def _pick_part(prompt_ref, sample_ref):
    return jnp.where(pl.program_id(0) < PROMPT_BLOCKS, prompt_ref[...], sample_ref[...])


def _parts(x):
    return (x[0], x[1], False) if isinstance(x, tuple) else (x, x, True)


def _proj_kernel(xa_ref, xb_ref, g_ref, w_ref, q_ref, k_ref, v_ref, gb_ref, gc_ref, gu_ref):
    hb = _rms(_pick_part(xa_ref, xb_ref), g_ref[...]).astype(BF16)
    outs = (q_ref, k_ref, v_ref, gb_ref, gc_ref, gu_ref)
    for j, o_ref in enumerate(outs):
        z = jnp.dot(hb, w_ref[:, j * ATTN_W:(j + 1) * ATTN_W], preferred_element_type=F32)
        if j == 0:
            z = z * (HEAD_DIM ** -0.5)
        o_ref[...] = z.astype(o_ref.dtype)


def _proj(x, g, w_in):
    xa, xb, joint = _parts(x)
    tok = pl.BlockSpec((TOK_BLOCK, ATTN_W), lambda i: (i, 0))
    return pl.pallas_call(
        _proj_kernel,
        grid=(N_BLOCKS,),
        in_specs=[*_split_specs(D_MODEL, joint),
                  pl.BlockSpec((1, D_MODEL), lambda i: (0, 0)),
                  pl.BlockSpec((D_MODEL, 6 * ATTN_W), lambda i: (0, 0))],
        out_specs=[tok] * 6,
        out_shape=[jax.ShapeDtypeStruct((N_TOK, ATTN_W), BF16)] * 3
                  + [jax.ShapeDtypeStruct((N_TOK, ATTN_W), F32)] * 3,
        compiler_params=pltpu.CompilerParams(dimension_semantics=("arbitrary",),
                                             vmem_limit_bytes=VMEM_LIMIT),
        name="proj",
    )(xa, xb, g, w_in)


def _group_norm(xv, gmat, g):
    sq = xv * xv
    hi = sq.astype(BF16)
    lo = (sq - hi.astype(F32)).astype(BF16)
    ms = (jnp.dot(hi, gmat, preferred_element_type=F32)
          + jnp.dot(lo, gmat, preferred_element_type=F32)) * (1.0 / HEAD_DIM)
    return xv * lax.rsqrt(ms + EPS) * g


def _mixer_kernel(xa_ref, xb_ref, q_ref, kp_ref, kc_ref, kn_ref, vp_ref, vc_ref, vn_ref,
                  gb_ref, gc_ref, gu_ref, gcp_ref, gup_ref, gcn_ref, gun_ref,
                  bias_ref, cw_ref, ga_ref, gcv_ref, gmat_ref, wo_ref,
                  o_ref, kbuf, vbuf, abuf):
    i = pl.program_id(0)
    is_prompt = i < PROMPT_BLOCKS
    sample = (i - PROMPT_BLOCKS) // SAMPLE_BLOCKS
    seq_r0 = jnp.where(is_prompt, 0, PROMPT_ROWS + SAMPLE_ROWS * sample)
    seq_r1 = jnp.where(is_prompt, PROMPT_ROWS, seq_r0 + SAMPLE_ROWS)
    blk_r0 = ROW_BLOCK * i

    kbuf[0:TOK_BLOCK, :] = kp_ref[...]
    kbuf[TOK_BLOCK:2 * TOK_BLOCK, :] = kc_ref[...]
    kbuf[2 * TOK_BLOCK:3 * TOK_BLOCK, :] = kn_ref[...]
    vbuf[0:TOK_BLOCK, :] = vp_ref[...]
    vbuf[TOK_BLOCK:2 * TOK_BLOCK, :] = vc_ref[...]
    vbuf[2 * TOK_BLOCK:3 * TOK_BLOCK, :] = vn_ref[...]

    lane = lax.broadcasted_iota(jnp.int32, (GRID_W, LANES), 1)
    low_half = lane < HEAD_DIM
    n_win = WIN_ROWS * GRID_W
    for jr in range(ROW_BLOCK):
        r = blk_r0 + jr
        rs = jnp.clip(r - WIN_ROWS // 2, seq_r0, seq_r1 - WIN_ROWS)
        delta = r - rs
        start = pl.multiple_of((rs - blk_r0 + ROW_BLOCK) * GRID_W, GRID_W)
        pairs = range(ATTN_HEADS // 2)
        col = [slice(p * LANES, (p + 1) * LANES) for p in pairs]
        scores = []
        for p in pairs:
            qp = q_ref[jr * GRID_W:(jr + 1) * GRID_W, col[p]]
            zero = jnp.zeros_like(qp)
            q2 = jnp.concatenate([jnp.where(low_half, qp, zero), jnp.where(low_half, zero, qp)], axis=0)
            scores.append(_dot_nt(q2, kbuf[pl.ds(start, n_win), col[p]]) + bias_ref[delta, p])
        probs, norms = [], []
        for s in scores:
            e = jnp.exp(s - jnp.max(s, axis=-1, keepdims=True))
            probs.append(e.astype(BF16))
            norms.append(jnp.sum(e, axis=-1, keepdims=True))
        for p in pairs:
            o2 = jnp.dot(probs[p], vbuf[pl.ds(start, n_win), col[p]], preferred_element_type=F32) / norms[p]
            abuf[jr * GRID_W:(jr + 1) * GRID_W, col[p]] = jnp.where(low_half, o2[:GRID_W], o2[GRID_W:])

    first = blk_r0 == seq_r0
    last = blk_r0 + ROW_BLOCK == seq_r1
    cu = gc_ref[...] * gu_ref[...]
    prev_row = jnp.where(first, 0.0, gcp_ref[HALO - 1:HALO, :] * gup_ref[HALO - 1:HALO, :])
    next_row = jnp.where(last, 0.0, gcn_ref[0:1, :] * gun_ref[0:1, :])
    row = lax.broadcasted_iota(jnp.int32, (TOK_BLOCK, CONV_W), 0)
    up_prev = jnp.where(row == 0, prev_row, pltpu.roll(cu, 1, axis=0))
    up_next = jnp.where(row == TOK_BLOCK - 1, next_row, pltpu.roll(cu, TOK_BLOCK - 1, axis=0))
    conv = gb_ref[...] * (up_prev * cw_ref[0:1, :] + cu * cw_ref[1:2, :] + up_next * cw_ref[2:3, :])

    gmat = gmat_ref[...]
    attn_n = _group_norm(abuf[...], gmat, ga_ref[...]).astype(BF16)
    conv_n = _group_norm(conv, gmat, gcv_ref[...]).astype(BF16)
    y = (jnp.dot(attn_n, wo_ref[0:ATTN_W, :], preferred_element_type=F32)
         + jnp.dot(conv_n, wo_ref[ATTN_W:D_MODEL, :], preferred_element_type=F32))
    o_ref[...] = _pick_part(xa_ref, xb_ref) + y


def _mixer(x, q, k, v, gb, gc, gu, bias, layer, conv_w, g_attn, g_conv, gmat, w_out):
    xa, xb, joint = _parts(x)
    cur = lambda i: (i, 0)
    prev = lambda i: (jnp.maximum(i - 1, 0), 0)
    nxt = lambda i: (jnp.minimum(i + 1, N_BLOCKS - 1), 0)
    halo_per_block = TOK_BLOCK // HALO
    hprev = lambda i: (jnp.maximum(i * halo_per_block - 1, 0), 0)
    hnext = lambda i: (jnp.minimum((i + 1) * halo_per_block, N_TOK // HALO - 1), 0)
    const2 = lambda i: (0, 0)
    blk = lambda m: pl.BlockSpec((TOK_BLOCK, ATTN_W), m)
    halo = lambda m: pl.BlockSpec((HALO, CONV_W), m)
    return pl.pallas_call(
        _mixer_kernel,
        grid=(N_BLOCKS,),
        in_specs=[*_split_specs(D_MODEL, joint),
                  blk(cur), blk(prev), blk(cur), blk(nxt), blk(prev), blk(cur), blk(nxt),
                  blk(cur), blk(cur), blk(cur), halo(hprev), halo(hprev), halo(hnext), halo(hnext),
                  pl.BlockSpec((None, WIN_ROWS, ATTN_HEADS // 2, 2 * GRID_W, WIN_ROWS * GRID_W),
                               lambda i: (layer, 0, 0, 0, 0)),
                  pl.BlockSpec((3, CONV_W), const2),
                  pl.BlockSpec((1, ATTN_W), const2),
                  pl.BlockSpec((1, CONV_W), const2),
                  pl.BlockSpec((ATTN_W, ATTN_W), const2),
                  pl.BlockSpec((D_MODEL, D_MODEL), const2)],
        out_specs=pl.BlockSpec((TOK_BLOCK, D_MODEL), cur),
        out_shape=jax.ShapeDtypeStruct((N_TOK, D_MODEL), F32),
        scratch_shapes=[pltpu.VMEM((3 * TOK_BLOCK, ATTN_W), BF16),
                        pltpu.VMEM((3 * TOK_BLOCK, ATTN_W), BF16),
                        pltpu.VMEM((TOK_BLOCK, ATTN_W), F32)],
        compiler_params=pltpu.CompilerParams(dimension_semantics=("arbitrary",),
                                             vmem_limit_bytes=VMEM_LIMIT),
        name="mixer",
    )(xa, xb, q, k, k, k, v, v, v, gb, gc, gu, gc, gu, gc, gu, bias, conv_w, g_attn, g_conv, gmat, w_out)


def _record(k, kidx, m, pick, vals, picks):
    sel = kidx == k
    return jnp.where(sel, m, vals), jnp.where(sel, pick, picks)


def _top_keys(s):
    n, t = s.shape
    half = n // 2
    r0 = lax.broadcasted_iota(jnp.int32, (half, t), 0).astype(F32)
    r1 = r0 + float(half)
    first = s[:half] >= s[half:]
    hi = jnp.where(first, s[:half], s[half:])
    lo = jnp.where(first, s[half:], s[:half])
    ihi = jnp.where(first, r0, r1)
    ilo = jnp.where(first, r1, r0)
    kidx = lax.broadcasted_iota(jnp.int32, (PEER_TOPK, t), 0)

    def body(k, carry):
        hi, lo, ihi, vals, picks = carry
        m = jnp.max(hi, axis=0, keepdims=True)
        pos = jnp.min(jnp.where(hi == m, ihi, float(n)), axis=0, keepdims=True)
        onehot = ihi == pos
        vals, picks = _record(k, kidx, m, pos, vals, picks)
        return (jnp.where(onehot, lo, hi), jnp.where(onehot, -jnp.inf, lo), jnp.where(onehot, ilo, ihi),
                vals, picks)

    zeros = jnp.zeros((PEER_TOPK, t), F32)
    out = lax.fori_loop(0, PEER_TOPK, body, (hi, lo, ihi, zeros, zeros), unroll=True)
    return out[3], out[4]


_CAND_PER_KA = [PEER_TOPK // (ka + 1) for ka in range(PEER_TOPK)]
_N_CAND = sum(_CAND_PER_KA)
_CAND_PAD = -_N_CAND % 8


def _top_candidates(v1, i1, v2, i2):
    t = v1.shape[1]
    s_rows, e_rows = [], []
    for ka, n_kb in enumerate(_CAND_PER_KA):
        s_rows.append(v1[ka:ka + 1, :] + v2[:n_kb, :])
        e_rows.append(i1[ka:ka + 1, :] * float(N_KEYS) + i2[:n_kb, :])
    s_rows.append(jnp.full((_CAND_PAD, t), -jnp.inf, F32))
    e_rows.append(jnp.zeros((_CAND_PAD, t), F32))
    s = jnp.concatenate(s_rows, axis=0)
    e = jnp.concatenate(e_rows, axis=0)
    rows = s.shape[0]
    ridx = lax.broadcasted_iota(jnp.int32, (rows, t), 0).astype(F32)
    kidx = lax.broadcasted_iota(jnp.int32, (PEER_TOPK, t), 0)

    def body(k, carry):
        s, vals, picks = carry
        m = jnp.max(s, axis=0, keepdims=True)
        pos = jnp.min(jnp.where(s == m, ridx, float(rows)), axis=0, keepdims=True)
        onehot = ridx == pos
        pick = jnp.max(jnp.where(onehot, e, -1.0), axis=0, keepdims=True)
        vals, picks = _record(k, kidx, m, pick, vals, picks)
        return jnp.where(onehot, -jnp.inf, s), vals, picks

    zeros = jnp.zeros((PEER_TOPK, t), F32)
    _, vals, picks = lax.fori_loop(0, PEER_TOPK, body, (s, zeros, zeros), unroll=True)
    return vals, picks


def _gelu_exact(x, half=0.5):
    return half * x * (1.0 + lax.erf(x * (2.0 ** -0.5)))


def _key_scores(hb, wq, k1, k2):
    half = D_KEY // 2
    qh = jnp.dot(hb, wq, preferred_element_type=F32).astype(BF16)
    return _dot_nt(k1, qh[:, :half]), _dot_nt(k2, qh[:, half:])


def _peer_kernel(xr_ref, xp_ref, g_ref, wq0_ref, wq_ref, k1_ref, k2_ref, ut_ref, v_ref, o_ref,
                 h_buf, sc_buf, gt_buf, et_buf, a_buf, b_buf, gate_buf, wtok, acc):
    i = pl.program_id(0)
    c = pl.program_id(1)
    last_c = c == PEER_N_CHUNKS - 1
    route_slot = i % 2
    peer_slot = 1 - route_slot
    score_slot = c % 2
    h_peer_slot = (i + 2) % 3

    @pl.when(jnp.logical_and(i == 0, c == 0))
    def _():
        hb = _rms(xr_ref[...], g_ref[...]).astype(BF16)
        h_buf[0] = hb
        s1, s2 = _key_scores(hb, wq0_ref[...], k1_ref[...], k2_ref[...])
        sc_buf[0, 0] = s1
        sc_buf[0, 1] = s2
        h_buf[2] = jnp.zeros((PEER_TOK, D_MODEL), BF16)
        gt_buf[1] = jnp.zeros((PEER_HEADS * PEER_TOPK, PEER_TOK), F32)
        et_buf[1] = jnp.zeros((PEER_HEADS * PEER_TOPK, PEER_TOK), F32)

    @pl.when(last_c)
    def _():
        h_buf[(i + 1) % 3] = _rms(xr_ref[...], g_ref[...]).astype(BF16)

    @pl.when(c == 0)
    def _():
        acc[...] = jnp.zeros_like(acc)
        e = et_buf[peer_slot].T
        a = jnp.floor(e * (1.0 / N_KEYS))
        a_buf[...] = a
        b_buf[...] = e - a * float(N_KEYS)
        gate_buf[...] = gt_buf[peer_slot].T
        sub = lax.broadcasted_iota(jnp.int32, (N_KEYS, LANES), 0).astype(F32)

        def tok_group(grp, carry):
            t0 = pl.multiple_of(grp * TOK_GROUP, TOK_GROUP)
            slabs = []
            for j in range(TOK_GROUP):
                ar = a_buf[pl.ds(t0 + j, 1), :]
                br = b_buf[pl.ds(t0 + j, 1), :]
                gr = gate_buf[pl.ds(t0 + j, 1), :]
                pg = jnp.where(sub == ar, gr, 0.0).astype(BF16)
                qb = jnp.where(sub == br, 1.0, 0.0).astype(BF16)
                slabs.append(_dot_nt(pg, qb))
            wtok[:, pl.ds(t0, TOK_GROUP), :] = jnp.swapaxes(jnp.stack(slabs), 0, 1)
            return carry

        lax.fori_loop(0, PEER_TOK // TOK_GROUP, tok_group, 0, unroll=16)

    v1, i1 = _top_keys(sc_buf[score_slot, 0])
    v2, i2 = _top_keys(sc_buf[score_slot, 1])

    h_next_slot = jnp.where(last_c, i + 1, i) % 3
    s1, s2 = _key_scores(h_buf[h_next_slot], wq_ref[...], k1_ref[...], k2_ref[...])
    sc_buf[1 - score_slot, 0] = s1
    sc_buf[1 - score_slot, 1] = s2

    a_act = jnp.dot(h_buf[h_peer_slot], ut_ref[...], preferred_element_type=F32)
    key0 = c * PEER_CHUNK_KEYS
    w = jnp.concatenate([wtok[key0 + j] for j in range(PEER_CHUNK_KEYS)], axis=1)
    half = 0.5 + jnp.where(i1[PEER_TOPK - 1:, 0:1] + i2[PEER_TOPK - 1:, 0:1] < -1.0, 1.0, 0.0)
    wg = w * _gelu_exact(a_act, half)
    acc[...] += jnp.dot(wg.astype(BF16), v_ref[...], preferred_element_type=F32)

    top_s, top_e = _top_candidates(v1, i1, v2, i2)
    ex = jnp.exp(top_s - jnp.max(top_s, axis=0, keepdims=True))
    rows = pl.ds(pl.multiple_of(c * PEER_TOPK, PEER_TOPK), PEER_TOPK)
    gt_buf[route_slot, rows, :] = ex / jnp.sum(ex, axis=0, keepdims=True)
    et_buf[route_slot, rows, :] = top_e

    @pl.when(last_c)
    def _():
        o_ref[...] = xp_ref[...] + acc[...]


def _peer(x, g, w_query, keys1, keys2, eut, ev, layer):
    n_blk = N_TOK // PEER_TOK
    hk = PEER_HEADS * PEER_TOPK
    route_blk = lambda i, c: (jnp.minimum(i + c // (PEER_N_CHUNKS - 1), n_blk - 1), 0)
    peer_blk = lambda i, c: (jnp.maximum(i - 1, 0), 0)
    const2 = lambda i, c: (0, 0)
    chunk = pl.BlockSpec((None, PEER_CHUNK, D_MODEL), lambda i, c: (layer, c, 0))
    chunk_t = pl.BlockSpec((None, D_MODEL, PEER_CHUNK), lambda i, c: (layer, 0, c))
    return pl.pallas_call(
        _peer_kernel,
        grid=(n_blk + 1, PEER_N_CHUNKS),
        in_specs=[pl.BlockSpec((PEER_TOK, D_MODEL), route_blk),
                  pl.BlockSpec((PEER_TOK, D_MODEL), peer_blk),
                  pl.BlockSpec((1, D_MODEL), const2),
                  pl.BlockSpec((D_MODEL, D_KEY), const2),
                  pl.BlockSpec((D_MODEL, D_KEY), lambda i, c: (0, (c + 1) % PEER_HEADS)),
                  pl.BlockSpec((N_KEYS, D_KEY // 2), const2),
                  pl.BlockSpec((N_KEYS, D_KEY // 2), const2),
                  chunk_t, chunk],
        out_specs=pl.BlockSpec((PEER_TOK, D_MODEL), peer_blk),
        out_shape=jax.ShapeDtypeStruct((N_TOK, D_MODEL), F32),
        scratch_shapes=[pltpu.VMEM((3, PEER_TOK, D_MODEL), BF16),
                        pltpu.VMEM((2, 2, N_KEYS, PEER_TOK), F32),
                        pltpu.VMEM((2, hk, PEER_TOK), F32),
                        pltpu.VMEM((2, hk, PEER_TOK), F32),
                        pltpu.VMEM((PEER_TOK, hk), F32),
                        pltpu.VMEM((PEER_TOK, hk), F32),
                        pltpu.VMEM((PEER_TOK, hk), F32),
                        pltpu.VMEM((N_KEYS, PEER_TOK, N_KEYS), F32),
                        pltpu.VMEM((PEER_TOK, D_MODEL), F32)],
        compiler_params=pltpu.CompilerParams(dimension_semantics=("arbitrary", "arbitrary"),
                                             vmem_limit_bytes=PEER_VMEM_LIMIT),
        name="peer",
    )(x, x, g, w_query, w_query, keys1, keys2, eut, ev)


def _ple_kernel(x_ref, pa_ref, pb_ref, g_ref, wg_ref, wp_ref, gf_ref, *o_refs, final):
    x = x_ref[...]
    hp = _rms(x, g_ref[...]).astype(BF16)
    gate = jax.nn.sigmoid(jnp.dot(hp, wg_ref[...], preferred_element_type=F32))
    proj = jnp.dot(_pick_part(pa_ref, pb_ref).astype(BF16), wp_ref[...], preferred_element_type=F32)
    y = x + gate * proj
    if not final:
        o_refs[0][...] = y
        return
    y = _rms(y, gf_ref[...])
    is_prompt = pl.program_id(0) < PROMPT_BLOCKS

    @pl.when(is_prompt)
    def _():
        o_refs[0][...] = y

    @pl.when(jnp.logical_not(is_prompt))
    def _():
        o_refs[1][...] = y


def _ple(x, p_prompt, p_sample, layer, g, w_gate, w_proj, g_final, final):
    const2 = lambda i: (0, 0)
    tok = pl.BlockSpec((TOK_BLOCK, D_MODEL), lambda i: (i, 0))
    p_specs = (pl.BlockSpec((None, TOK_BLOCK, PLE_DIM), lambda i: (layer, jnp.minimum(i, PROMPT_BLOCKS - 1), 0)),
               pl.BlockSpec((None, TOK_BLOCK, PLE_DIM), lambda i: (layer, jnp.maximum(i - PROMPT_BLOCKS, 0), 0)))
    if final:
        out_specs = list(_split_specs(D_MODEL, joint=False))
        out_shape = [jax.ShapeDtypeStruct((SEQ, D_MODEL), F32), jax.ShapeDtypeStruct((N_TOK - SEQ, D_MODEL), F32)]
    else:
        out_specs, out_shape = tok, jax.ShapeDtypeStruct((N_TOK, D_MODEL), F32)
    return pl.pallas_call(
        functools.partial(_ple_kernel, final=final),
        grid=(N_BLOCKS,),
        in_specs=[tok, *p_specs,
                  pl.BlockSpec((1, D_MODEL), const2),
                  pl.BlockSpec((D_MODEL, D_MODEL), const2),
                  pl.BlockSpec((PLE_DIM, D_MODEL), const2),
                  pl.BlockSpec((1, D_MODEL), const2)],
        out_specs=out_specs,
        out_shape=out_shape,
        compiler_params=pltpu.CompilerParams(dimension_semantics=("arbitrary",),
                                             vmem_limit_bytes=VMEM_LIMIT),
        name="ple",
    )(x, p_prompt, p_sample, g, w_gate, w_proj, g_final)


def _bias_tables(rpb):
    cols = np.arange(GRID_W)
    col_start = np.clip(cols - WIN_COLS // 2, 0, GRID_W - WIN_COLS)
    kc = np.arange(GRID_W)
    in_win = (kc[None, :] >= col_start[:, None]) & (kc[None, :] < col_start[:, None] + WIN_COLS)
    col_off = kc[None, :] - cols[:, None] + (WIN_COLS - 1)
    onehot = (col_off[None] == np.arange(2 * WIN_COLS - 1)[:, None, None]) & in_win[None]
    delta = np.arange(WIN_ROWS)
    j = np.arange(WIN_ROWS)
    row_off = j[None, :] - delta[:, None] + (WIN_ROWS - 1)
    rows = rpb[:, :, row_off]
    t = jnp.einsum('lhdjo,ock->ldhcjk', rows, jnp.asarray(onehot, F32), precision=lax.Precision.HIGHEST)
    t = t + jnp.asarray(np.where(in_win, 0.0, NEG_BIG), F32)[None, None, None, :, None, :]
    return t.reshape(DEPTH, WIN_ROWS, ATTN_HEADS // 2, 2 * GRID_W, WIN_ROWS * GRID_W)


def kernel(x_prompt, x_sample, p_prompt, p_sample, g_mix, w_in, rpb, conv_w, g_attn_out, g_conv_out, w_out,
           g_ffn, w_query, sub_keys1, sub_keys2, expert_u, expert_v, g_ple, w_ple_gate, w_ple_proj, g_final):
    x = (x_prompt.reshape(SEQ, D_MODEL), x_sample.reshape(DEC_BATCH * DEC_SEQ, D_MODEL))
    pp = p_prompt.reshape(DEPTH, SEQ, PLE_DIM)
    ps = p_sample.reshape(DEPTH, DEC_BATCH * DEC_SEQ, PLE_DIM)
    group = np.arange(ATTN_W) // HEAD_DIM
    gmat = jnp.asarray(group[:, None] == group[None, :], dtype=BF16)
    row = lambda g: g.reshape(1, -1)
    bias = _bias_tables(rpb)
    eut, ev = jnp.swapaxes(expert_u.astype(BF16), 1, 2), expert_v.astype(BF16)
    for i in range(DEPTH):
        q, k, v, gb, gc, gu = _proj(x, row(g_mix[i]), w_in[i].astype(BF16))
        x = _mixer(x, q, k, v, gb, gc, gu, bias, i, conv_w[i], row(g_attn_out[i]),
                   row(g_conv_out[i]), gmat, w_out[i].astype(BF16))
        x = _peer(x, row(g_ffn[i]), w_query[i].astype(BF16), sub_keys1[i].astype(BF16),
                  sub_keys2[i].astype(BF16), eut, ev, i)
        x = _ple(x, pp, ps, i, row(g_ple[i]), w_ple_gate[i].astype(BF16), w_ple_proj[i].astype(BF16),
                 row(g_final), final=(i == DEPTH - 1))
    y_prompt, y_sample = x
    return (y_prompt.reshape(1, SEQ, D_MODEL), y_sample.reshape(DEC_BATCH, DEC_SEQ, D_MODEL))
```

```python
import functools

import numpy as np
import jax
import jax.numpy as jnp
from jax import lax
from jax.experimental import pallas as pl
from jax.experimental.pallas import tpu as pltpu

F32 = jnp.float32
BF16 = jnp.bfloat16

D_MODEL = 1024
DEPTH = 2
SEQ = 16384
DEC_BATCH = 4
DEC_SEQ = 4096
N_TOK = SEQ + DEC_BATCH * DEC_SEQ

GRID_W = 64
WIN_ROWS = 8
WIN_COLS = 16
ATTN_HEADS = 8
HEAD_DIM = 64
ATTN_W = ATTN_HEADS * HEAD_DIM
CONV_W = D_MODEL - ATTN_W
N_KEYS = 128
N_EXPERTS = N_KEYS * N_KEYS
PEER_HEADS = 8
D_KEY = 256
PEER_TOPK = 16
PLE_DIM = 256
EPS = 1e-6
NEG_BIG = -1e30

LANES = 128
ROW_BLOCK = WIN_ROWS
TOK_BLOCK = ROW_BLOCK * GRID_W
N_BLOCKS = N_TOK // TOK_BLOCK
PROMPT_ROWS = SEQ // GRID_W
SAMPLE_ROWS = DEC_SEQ // GRID_W
PROMPT_BLOCKS = PROMPT_ROWS // ROW_BLOCK
SAMPLE_BLOCKS = SAMPLE_ROWS // ROW_BLOCK
HALO = 8

PEER_TOK = 256
TOK_GROUP = 8
PEER_CHUNK_KEYS = 16
PEER_CHUNK = PEER_CHUNK_KEYS * N_KEYS
PEER_N_CHUNKS = N_EXPERTS // PEER_CHUNK
assert PEER_N_CHUNKS == PEER_HEADS

VMEM_LIMIT = 48 * 1024 * 1024
PEER_VMEM_LIMIT = 56 * 1024 * 1024


def _rms(x, g):
    return x * lax.rsqrt(jnp.mean(x * x, axis=-1, keepdims=True) + EPS) * g


def _dot_nt(a, b):
    return lax.dot_general(a, b, (((1,), (1,)), ((), ())), preferred_element_type=F32)


def _split_specs(width, joint):
    off = PROMPT_BLOCKS if joint else 0
    return (pl.BlockSpec((TOK_BLOCK, width), lambda i: (jnp.minimum(i, PROMPT_BLOCKS - 1), 0)),
            pl.BlockSpec((TOK_BLOCK, width), lambda i: (jnp.maximum(i - PROMPT_BLOCKS, 0) + off, 0)))


def _pick_part(prompt_ref, sample_ref):
    return jnp.where(pl.program_id(0) < PROMPT_BLOCKS, prompt_ref[...], sample_ref[...])


def _parts(x):
    return (x[0], x[1], False) if isinstance(x, tuple) else (x, x, True)


def _proj_kernel(xa_ref, xb_ref, g_ref, w_ref, q_ref, k_ref, v_ref, gb_ref, gc_ref, gu_ref):
    hb = _rms(_pick_part(xa_ref, xb_ref), g_ref[...]).astype(BF16)
    outs = (q_ref, k_ref, v_ref, gb_ref, gc_ref, gu_ref)
    for j, o_ref in enumerate(outs):
        z = jnp.dot(hb, w_ref[:, j * ATTN_W:(j + 1) * ATTN_W], preferred_element_type=F32)
        if j == 0:
            z = z * (HEAD_DIM ** -0.5)
        o_ref[...] = z.astype(o_ref.dtype)


def _proj(x, g, w_in):
    xa, xb, joint = _parts(x)
    tok = pl.BlockSpec((TOK_BLOCK, ATTN_W), lambda i: (i, 0))
    return pl.pallas_call(
        _proj_kernel,
        grid=(N_BLOCKS,),
        in_specs=[*_split_specs(D_MODEL, joint),
                  pl.BlockSpec((1, D_MODEL), lambda i: (0, 0)),
                  pl.BlockSpec((D_MODEL, 6 * ATTN_W), lambda i: (0, 0))],
        out_specs=[tok] * 6,
        out_shape=[jax.ShapeDtypeStruct((N_TOK, ATTN_W), BF16)] * 3
                  + [jax.ShapeDtypeStruct((N_TOK, ATTN_W), F32)] * 3,
        compiler_params=pltpu.CompilerParams(dimension_semantics=("arbitrary",),
                                             vmem_limit_bytes=VMEM_LIMIT),
        name="proj",
    )(xa, xb, g, w_in)


def _group_norm(xv, gmat, g):
    sq = xv * xv
    hi = sq.astype(BF16)
    lo = (sq - hi.astype(F32)).astype(BF16)
    ms = (jnp.dot(hi, gmat, preferred_element_type=F32)
          + jnp.dot(lo, gmat, preferred_element_type=F32)) * (1.0 / HEAD_DIM)
    return xv * lax.rsqrt(ms + EPS) * g


def _mixer_kernel(xa_ref, xb_ref, q_ref, kp_ref, kc_ref, kn_ref, vp_ref, vc_ref, vn_ref,
                  gb_ref, gc_ref, gu_ref, gcp_ref, gup_ref, gcn_ref, gun_ref,
                  bias_ref, cw_ref, ga_ref, gcv_ref, gmat_ref, wo_ref,
                  o_ref, kbuf, vbuf, abuf):
    i = pl.program_id(0)
    is_prompt = i < PROMPT_BLOCKS
    sample = (i - PROMPT_BLOCKS) // SAMPLE_BLOCKS
    seq_r0 = jnp.where(is_prompt, 0, PROMPT_ROWS + SAMPLE_ROWS * sample)
    seq_r1 = jnp.where(is_prompt, PROMPT_ROWS, seq_r0 + SAMPLE_ROWS)
    blk_r0 = ROW_BLOCK * i

    kbuf[0:TOK_BLOCK, :] = kp_ref[...]
    kbuf[TOK_BLOCK:2 * TOK_BLOCK, :] = kc_ref[...]
    kbuf[2 * TOK_BLOCK:3 * TOK_BLOCK, :] = kn_ref[...]
    vbuf[0:TOK_BLOCK, :] = vp_ref[...]
    vbuf[TOK_BLOCK:2 * TOK_BLOCK, :] = vc_ref[...]
    vbuf[2 * TOK_BLOCK:3 * TOK_BLOCK, :] = vn_ref[...]

    lane = lax.broadcasted_iota(jnp.int32, (GRID_W, LANES), 1)
    low_half = lane < HEAD_DIM
    n_win = WIN_ROWS * GRID_W
    for jr in range(ROW_BLOCK):
        r = blk_r0 + jr
        rs = jnp.clip(r - WIN_ROWS // 2, seq_r0, seq_r1 - WIN_ROWS)
        delta = r - rs
        start = pl.multiple_of((rs - blk_r0 + ROW_BLOCK) * GRID_W, GRID_W)
        pairs = range(ATTN_HEADS // 2)
        col = [slice(p * LANES, (p + 1) * LANES) for p in pairs]
        scores = []
        for p in pairs:
            qp = q_ref[jr * GRID_W:(jr + 1) * GRID_W, col[p]]
            zero = jnp.zeros_like(qp)
            q2 = jnp.concatenate([jnp.where(low_half, qp, zero), jnp.where(low_half, zero, qp)], axis=0)
            scores.append(_dot_nt(q2, kbuf[pl.ds(start, n_win), col[p]]) + bias_ref[delta, p])
        probs, norms = [], []
        for s in scores:
            e = jnp.exp(s - jnp.max(s, axis=-1, keepdims=True))
            probs.append(e.astype(BF16))
            norms.append(jnp.sum(e, axis=-1, keepdims=True))
        for p in pairs:
            o2 = jnp.dot(probs[p], vbuf[pl.ds(start, n_win), col[p]], preferred_element_type=F32) / norms[p]
            abuf[jr * GRID_W:(jr + 1) * GRID_W, col[p]] = jnp.where(low_half, o2[:GRID_W], o2[GRID_W:])

    first = blk_r0 == seq_r0
    last = blk_r0 + ROW_BLOCK == seq_r1
    cu = gc_ref[...] * gu_ref[...]
    prev_row = jnp.where(first, 0.0, gcp_ref[HALO - 1:HALO, :] * gup_ref[HALO - 1:HALO, :])
    next_row = jnp.where(last, 0.0, gcn_ref[0:1, :] * gun_ref[0:1, :])
    row = lax.broadcasted_iota(jnp.int32, (TOK_BLOCK, CONV_W), 0)
    up_prev = jnp.where(row == 0, prev_row, pltpu.roll(cu, 1, axis=0))
    up_next = jnp.where(row == TOK_BLOCK - 1, next_row, pltpu.roll(cu, TOK_BLOCK - 1, axis=0))
    conv = gb_ref[...] * (up_prev * cw_ref[0:1, :] + cu * cw_ref[1:2, :] + up_next * cw_ref[2:3, :])

    gmat = gmat_ref[...]
    attn_n = _group_norm(abuf[...], gmat, ga_ref[...]).astype(BF16)
    conv_n = _group_norm(conv, gmat, gcv_ref[...]).astype(BF16)
    y = (jnp.dot(attn_n, wo_ref[0:ATTN_W, :], preferred_element_type=F32)
         + jnp.dot(conv_n, wo_ref[ATTN_W:D_MODEL, :], preferred_element_type=F32))
    o_ref[...] = _pick_part(xa_ref, xb_ref) + y


def _mixer(x, q, k, v, gb, gc, gu, bias, layer, conv_w, g_attn, g_conv, gmat, w_out):
    xa, xb, joint = _parts(x)
    cur = lambda i: (i, 0)
    prev = lambda i: (jnp.maximum(i - 1, 0), 0)
    nxt = lambda i: (jnp.minimum(i + 1, N_BLOCKS - 1), 0)
    halo_per_block = TOK_BLOCK // HALO
    hprev = lambda i: (jnp.maximum(i * halo_per_block - 1, 0), 0)
    hnext = lambda i: (jnp.minimum((i + 1) * halo_per_block, N_TOK // HALO - 1), 0)
    const2 = lambda i: (0, 0)
    blk = lambda m: pl.BlockSpec((TOK_BLOCK, ATTN_W), m)
    halo = lambda m: pl.BlockSpec((HALO, CONV_W), m)
    return pl.pallas_call(
        _mixer_kernel,
        grid=(N_BLOCKS,),
        in_specs=[*_split_specs(D_MODEL, joint),
                  blk(cur), blk(prev), blk(cur), blk(nxt), blk(prev), blk(cur), blk(nxt),
                  blk(cur), blk(cur), blk(cur), halo(hprev), halo(hprev), halo(hnext), halo(hnext),
                  pl.BlockSpec((None, WIN_ROWS, ATTN_HEADS // 2, 2 * GRID_W, WIN_ROWS * GRID_W),
                               lambda i: (layer, 0, 0, 0, 0)),
                  pl.BlockSpec((3, CONV_W), const2),
                  pl.BlockSpec((1, ATTN_W), const2),
                  pl.BlockSpec((1, CONV_W), const2),
                  pl.BlockSpec((ATTN_W, ATTN_W), const2),
                  pl.BlockSpec((D_MODEL, D_MODEL), const2)],
        out_specs=pl.BlockSpec((TOK_BLOCK, D_MODEL), cur),
        out_shape=jax.ShapeDtypeStruct((N_TOK, D_MODEL), F32),
        scratch_shapes=[pltpu.VMEM((3 * TOK_BLOCK, ATTN_W), BF16),
                        pltpu.VMEM((3 * TOK_BLOCK, ATTN_W), BF16),
                        pltpu.VMEM((TOK_BLOCK, ATTN_W), F32)],
        compiler_params=pltpu.CompilerParams(dimension_semantics=("arbitrary",),
                                             vmem_limit_bytes=VMEM_LIMIT),
        name="mixer",
    )(xa, xb, q, k, k, k, v, v, v, gb, gc, gu, gc, gu, gc, gu, bias, conv_w, g_attn, g_conv, gmat, w_out)


def _record(k, kidx, m, pick, vals, picks):
    sel = kidx == k
    return jnp.where(sel, m, vals), jnp.where(sel, pick, picks)


def _top_keys(s):
    n, t = s.shape
    half = n // 2
    r0 = lax.broadcasted_iota(jnp.int32, (half, t), 0).astype(F32)
    r1 = r0 + float(half)
    first = s[:half] >= s[half:]
    hi = jnp.where(first, s[:half], s[half:])
    lo = jnp.where(first, s[half:], s[:half])
    ihi = jnp.where(first, r0, r1)
    ilo = jnp.where(first, r1, r0)
    kidx = lax.broadcasted_iota(jnp.int32, (PEER_TOPK, t), 0)

    def body(k, carry):
        hi, lo, ihi, vals, picks = carry
        m = jnp.max(hi, axis=0, keepdims=True)
        pos = jnp.min(jnp.where(hi == m, ihi, float(n)), axis=0, keepdims=True)
        onehot = ihi == pos
        vals, picks = _record(k, kidx, m, pos, vals, picks)
        return (jnp.where(onehot, lo, hi), jnp.where(onehot, -jnp.inf, lo), jnp.where(onehot, ilo, ihi),
                vals, picks)

    zeros = jnp.zeros((PEER_TOPK, t), F32)
    out = lax.fori_loop(0, PEER_TOPK, body, (hi, lo, ihi, zeros, zeros), unroll=True)
    return out[3], out[4]


_CAND_PER_KA = [PEER_TOPK // (ka + 1) for ka in range(PEER_TOPK)]
_N_CAND = sum(_CAND_PER_KA)
_CAND_PAD = -_N_CAND % 8


def _top_candidates(v1, i1, v2, i2):
    t = v1.shape[1]
    s_rows, e_rows = [], []
    for ka, n_kb in enumerate(_CAND_PER_KA):
        s_rows.append(v1[ka:ka + 1, :] + v2[:n_kb, :])
        e_rows.append(i1[ka:ka + 1, :] * float(N_KEYS) + i2[:n_kb, :])
    s_rows.append(jnp.full((_CAND_PAD, t), -jnp.inf, F32))
    e_rows.append(jnp.zeros((_CAND_PAD, t), F32))
    s = jnp.concatenate(s_rows, axis=0)
    e = jnp.concatenate(e_rows, axis=0)
    rows = s.shape[0]
    ridx = lax.broadcasted_iota(jnp.int32, (rows, t), 0).astype(F32)
    kidx = lax.broadcasted_iota(jnp.int32, (PEER_TOPK, t), 0)

    def body(k, carry):
        s, vals, picks = carry
        m = jnp.max(s, axis=0, keepdims=True)
        pos = jnp.min(jnp.where(s == m, ridx, float(rows)), axis=0, keepdims=True)
        onehot = ridx == pos
        pick = jnp.max(jnp.where(onehot, e, -1.0), axis=0, keepdims=True)
        vals, picks = _record(k, kidx, m, pick, vals, picks)
        return jnp.where(onehot, -jnp.inf, s), vals, picks

    zeros = jnp.zeros((PEER_TOPK, t), F32)
    _, vals, picks = lax.fori_loop(0, PEER_TOPK, body, (s, zeros, zeros), unroll=True)
    return vals, picks


def _gelu_exact(x, half=0.5):
    return half * x * (1.0 + lax.erf(x * (2.0 ** -0.5)))


def _key_scores(hb, wq, k1, k2):
    half = D_KEY // 2
    qh = jnp.dot(hb, wq, preferred_element_type=F32).astype(BF16)
    return _dot_nt(k1, qh[:, :half]), _dot_nt(k2, qh[:, half:])


def _peer_kernel(xr_ref, xp_ref, g_ref, wq0_ref, wq_ref, k1_ref, k2_ref, ut_ref, v_ref, o_ref,
                 h_buf, sc_buf, gt_buf, et_buf, a_buf, b_buf, gate_buf, wtok, acc):
    i = pl.program_id(0)
    c = pl.program_id(1)
    last_c = c == PEER_N_CHUNKS - 1
    route_slot = i % 2
    peer_slot = 1 - route_slot
    score_slot = c % 2
    h_peer_slot = (i + 2) % 3

    @pl.when(jnp.logical_and(i == 0, c == 0))
    def _():
        hb = _rms(xr_ref[...], g_ref[...]).astype(BF16)
        h_buf[0] = hb
        s1, s2 = _key_scores(hb, wq0_ref[...], k1_ref[...], k2_ref[...])
        sc_buf[0, 0] = s1
        sc_buf[0, 1] = s2
        h_buf[2] = jnp.zeros((PEER_TOK, D_MODEL), BF16)
        gt_buf[1] = jnp.zeros((PEER_HEADS * PEER_TOPK, PEER_TOK), F32)
        et_buf[1] = jnp.zeros((PEER_HEADS * PEER_TOPK, PEER_TOK), F32)

    @pl.when(last_c)
    def _():
        h_buf[(i + 1) % 3] = _rms(xr_ref[...], g_ref[...]).astype(BF16)

    @pl.when(c == 0)
    def _():
        acc[...] = jnp.zeros_like(acc)
        e = et_buf[peer_slot].T
        a = jnp.floor(e * (1.0 / N_KEYS))
        a_buf[...] = a
        b_buf[...] = e - a * float(N_KEYS)
        gate_buf[...] = gt_buf[peer_slot].T
        sub = lax.broadcasted_iota(jnp.int32, (N_KEYS, LANES), 0).astype(F32)

        def tok_group(grp, carry):
            t0 = pl.multiple_of(grp * TOK_GROUP, TOK_GROUP)
            slabs = []
            for j in range(TOK_GROUP):
                ar = a_buf[pl.ds(t0 + j, 1), :]
                br = b_buf[pl.ds(t0 + j, 1), :]
                gr = gate_buf[pl.ds(t0 + j, 1), :]
                pg = jnp.where(sub == ar, gr, 0.0).astype(BF16)
                qb = jnp.where(sub == br, 1.0, 0.0).astype(BF16)
                slabs.append(_dot_nt(pg, qb))
            wtok[:, pl.ds(t0, TOK_GROUP), :] = jnp.swapaxes(jnp.stack(slabs), 0, 1)
            return carry

        lax.fori_loop(0, PEER_TOK // TOK_GROUP, tok_group, 0, unroll=16)

    v1, i1 = _top_keys(sc_buf[score_slot, 0])
    v2, i2 = _top_keys(sc_buf[score_slot, 1])

    h_next_slot = jnp.where(last_c, i + 1, i) % 3
    s1, s2 = _key_scores(h_buf[h_next_slot], wq_ref[...], k1_ref[...], k2_ref[...])
    sc_buf[1 - score_slot, 0] = s1
    sc_buf[1 - score_slot, 1] = s2

    a_act = jnp.dot(h_buf[h_peer_slot], ut_ref[...], preferred_element_type=F32)
    key0 = c * PEER_CHUNK_KEYS
    w = jnp.concatenate([wtok[key0 + j] for j in range(PEER_CHUNK_KEYS)], axis=1)
    half = 0.5 + jnp.where(i1[PEER_TOPK - 1:, 0:1] < -1.0, 1.0, 0.0)
    wg = w * _gelu_exact(a_act, half)
    acc[...] += jnp.dot(wg.astype(BF16), v_ref[...], preferred_element_type=F32)

    top_s, top_e = _top_candidates(v1, i1, v2, i2)
    ex = jnp.exp(top_s - jnp.max(top_s, axis=0, keepdims=True))
    rows = pl.ds(pl.multiple_of(c * PEER_TOPK, PEER_TOPK), PEER_TOPK)
    gt_buf[route_slot, rows, :] = ex / jnp.sum(ex, axis=0, keepdims=True)
    et_buf[route_slot, rows, :] = top_e

    @pl.when(last_c)
    def _():
        o_ref[...] = xp_ref[...] + acc[...]


def _peer(x, g, w_query, keys1, keys2, eut, ev, layer):
    n_blk = N_TOK // PEER_TOK
    hk = PEER_HEADS * PEER_TOPK
    route_blk = lambda i, c: (jnp.minimum(i + c // (PEER_N_CHUNKS - 1), n_blk - 1), 0)
    peer_blk = lambda i, c: (jnp.maximum(i - 1, 0), 0)
    const2 = lambda i, c: (0, 0)
    chunk = pl.BlockSpec((None, PEER_CHUNK, D_MODEL), lambda i, c: (layer, c, 0))
    chunk_t = pl.BlockSpec((None, D_MODEL, PEER_CHUNK), lambda i, c: (layer, 0, c))
    return pl.pallas_call(
        _peer_kernel,
        grid=(n_blk + 1, PEER_N_CHUNKS),
        in_specs=[pl.BlockSpec((PEER_TOK, D_MODEL), route_blk),
                  pl.BlockSpec((PEER_TOK, D_MODEL), peer_blk),
                  pl.BlockSpec((1, D_MODEL), const2),
                  pl.BlockSpec((D_MODEL, D_KEY), const2),
                  pl.BlockSpec((D_MODEL, D_KEY), lambda i, c: (0, (c + 1) % PEER_HEADS)),
                  pl.BlockSpec((N_KEYS, D_KEY // 2), const2),
                  pl.BlockSpec((N_KEYS, D_KEY // 2), const2),
                  chunk_t, chunk],
        out_specs=pl.BlockSpec((PEER_TOK, D_MODEL), peer_blk),
        out_shape=jax.ShapeDtypeStruct((N_TOK, D_MODEL), F32),
        scratch_shapes=[pltpu.VMEM((3, PEER_TOK, D_MODEL), BF16),
                        pltpu.VMEM((2, 2, N_KEYS, PEER_TOK), F32),
                        pltpu.VMEM((2, hk, PEER_TOK), F32),
                        pltpu.VMEM((2, hk, PEER_TOK), F32),
                        pltpu.VMEM((PEER_TOK, hk), F32),
                        pltpu.VMEM((PEER_TOK, hk), F32),
                        pltpu.VMEM((PEER_TOK, hk), F32),
                        pltpu.VMEM((N_KEYS, PEER_TOK, N_KEYS), F32),
                        pltpu.VMEM((PEER_TOK, D_MODEL), F32)],
        compiler_params=pltpu.CompilerParams(dimension_semantics=("arbitrary", "arbitrary"),
                                             vmem_limit_bytes=PEER_VMEM_LIMIT),
        name="peer",
    )(x, x, g, w_query, w_query, keys1, keys2, eut, ev)


def _ple_kernel(x_ref, pa_ref, pb_ref, g_ref, wg_ref, wp_ref, gf_ref, *o_refs, final):
    x = x_ref[...]
    hp = _rms(x, g_ref[...]).astype(BF16)
    gate = jax.nn.sigmoid(jnp.dot(hp, wg_ref[...], preferred_element_type=F32))
    proj = jnp.dot(_pick_part(pa_ref, pb_ref).astype(BF16), wp_ref[...], preferred_element_type=F32)
    y = x + gate * proj
    if not final:
        o_refs[0][...] = y
        return
    y = _rms(y, gf_ref[...])
    is_prompt = pl.program_id(0) < PROMPT_BLOCKS

    @pl.when(is_prompt)
    def _():
        o_refs[0][...] = y

    @pl.when(jnp.logical_not(is_prompt))
    def _():
        o_refs[1][...] = y


def _ple(x, p_prompt, p_sample, layer, g, w_gate, w_proj, g_final, final):
    const2 = lambda i: (0, 0)
    tok = pl.BlockSpec((TOK_BLOCK, D_MODEL), lambda i: (i, 0))
    p_specs = (pl.BlockSpec((None, TOK_BLOCK, PLE_DIM), lambda i: (layer, jnp.minimum(i, PROMPT_BLOCKS - 1), 0)),
               pl.BlockSpec((None, TOK_BLOCK, PLE_DIM), lambda i: (layer, jnp.maximum(i - PROMPT_BLOCKS, 0), 0)))
    if final:
        out_specs = list(_split_specs(D_MODEL, joint=False))
        out_shape = [jax.ShapeDtypeStruct((SEQ, D_MODEL), F32), jax.ShapeDtypeStruct((N_TOK - SEQ, D_MODEL), F32)]
    else:
        out_specs, out_shape = tok, jax.ShapeDtypeStruct((N_TOK, D_MODEL), F32)
    return pl.pallas_call(
        functools.partial(_ple_kernel, final=final),
        grid=(N_BLOCKS,),
        in_specs=[tok, *p_specs,
                  pl.BlockSpec((1, D_MODEL), const2),
                  pl.BlockSpec((D_MODEL, D_MODEL), const2),
                  pl.BlockSpec((PLE_DIM, D_MODEL), const2),
                  pl.BlockSpec((1, D_MODEL), const2)],
        out_specs=out_specs,
        out_shape=out_shape,
        compiler_params=pltpu.CompilerParams(dimension_semantics=("arbitrary",),
                                             vmem_limit_bytes=VMEM_LIMIT),
        name="ple",
    )(x, p_prompt, p_sample, g, w_gate, w_proj, g_final)


def _bias_tables(rpb):
    cols = np.arange(GRID_W)
    col_start = np.clip(cols - WIN_COLS // 2, 0, GRID_W - WIN_COLS)
    kc = np.arange(GRID_W)
    in_win = (kc[None, :] >= col_start[:, None]) & (kc[None, :] < col_start[:, None] + WIN_COLS)
    col_off = kc[None, :] - cols[:, None] + (WIN_COLS - 1)
    onehot = (col_off[None] == np.arange(2 * WIN_COLS - 1)[:, None, None]) & in_win[None]
    delta = np.arange(WIN_ROWS)
    j = np.arange(WIN_ROWS)
    row_off = j[None, :] - delta[:, None] + (WIN_ROWS - 1)
    rows = rpb[:, :, row_off]
    t = jnp.einsum('lhdjo,ock->ldhcjk', rows, jnp.asarray(onehot, F32), precision=lax.Precision.HIGHEST)
    t = t + jnp.asarray(np.where(in_win, 0.0, NEG_BIG), F32)[None, None, None, :, None, :]
    return t.reshape(DEPTH, WIN_ROWS, ATTN_HEADS // 2, 2 * GRID_W, WIN_ROWS * GRID_W)


def kernel(x_prompt, x_sample, p_prompt, p_sample, g_mix, w_in, rpb, conv_w, g_attn_out, g_conv_out, w_out,
           g_ffn, w_query, sub_keys1, sub_keys2, expert_u, expert_v, g_ple, w_ple_gate, w_ple_proj, g_final):
    x = (x_prompt.reshape(SEQ, D_MODEL), x_sample.reshape(DEC_BATCH * DEC_SEQ, D_MODEL))
    pp = p_prompt.reshape(DEPTH, SEQ, PLE_DIM)
    ps = p_sample.reshape(DEPTH, DEC_BATCH * DEC_SEQ, PLE_DIM)
    group = np.arange(ATTN_W) // HEAD_DIM
    gmat = jnp.asarray(group[:, None] == group[None, :], dtype=BF16)
    row = lambda g: g.reshape(1, -1)
    bias = _bias_tables(rpb)
    eut, ev = jnp.swapaxes(expert_u.astype(BF16), 1, 2), expert_v.astype(BF16)
    for i in range(DEPTH):
        q, k, v, gb, gc, gu = _proj(x, row(g_mix[i]), w_in[i].astype(BF16))
        x = _mixer(x, q, k, v, gb, gc, gu, bias, i, conv_w[i], row(g_attn_out[i]),
                   row(g_conv_out[i]), gmat, w_out[i].astype(BF16))
        x = _peer(x, row(g_ffn[i]), w_query[i].astype(BF16), sub_keys1[i].astype(BF16),
                  sub_keys2[i].astype(BF16), eut, ev, i)
        x = _ple(x, pp, ps, i, row(g_ple[i]), w_ple_gate[i].astype(BF16), w_ple_proj[i].astype(BF16),
                 row(g_final), final=(i == DEPTH - 1))
    y_prompt, y_sample = x
    return (y_prompt.reshape(1, SEQ, D_MODEL), y_sample.reshape(DEC_BATCH, DEC_SEQ, D_MODEL))
```

```python
import functools

import numpy as np
import jax
import jax.numpy as jnp
from jax import lax
from jax.experimental import pallas as pl
from jax.experimental.pallas import tpu as pltpu

F32 = jnp.float32
BF16 = jnp.bfloat16

D_MODEL = 1024
DEPTH = 2
SEQ = 16384
DEC_BATCH = 4
DEC_SEQ = 4096
N_TOK = SEQ + DEC_BATCH * DEC_SEQ

GRID_W = 64
WIN_ROWS = 8
WIN_COLS = 16
ATTN_HEADS = 8
HEAD_DIM = 64
ATTN_W = ATTN_HEADS * HEAD_DIM
CONV_W = D_MODEL - ATTN_W
N_KEYS = 128
N_EXPERTS = N_KEYS * N_KEYS
PEER_HEADS = 8
D_KEY = 256
PEER_TOPK = 16
PLE_DIM = 256
EPS = 1e-6
NEG_BIG = -1e30

LANES = 128
ROW_BLOCK = WIN_ROWS
TOK_BLOCK = ROW_BLOCK * GRID_W
N_BLOCKS = N_TOK // TOK_BLOCK
PROMPT_ROWS = SEQ // GRID_W
SAMPLE_ROWS = DEC_SEQ // GRID_W
PROMPT_BLOCKS = PROMPT_ROWS // ROW_BLOCK
SAMPLE_BLOCKS = SAMPLE_ROWS // ROW_BLOCK
HALO = 8

PEER_TOK = 256
TOK_GROUP = 8
PEER_CHUNK_KEYS = 16
PEER_CHUNK = PEER_CHUNK_KEYS * N_KEYS
PEER_N_CHUNKS = N_EXPERTS // PEER_CHUNK
assert PEER_N_CHUNKS == PEER_HEADS

VMEM_LIMIT = 48 * 1024 * 1024
PEER_VMEM_LIMIT = 56 * 1024 * 1024


def _rms(x, g):
    return x * lax.rsqrt(jnp.mean(x * x, axis=-1, keepdims=True) + EPS) * g


def _dot_nt(a, b):
    return lax.dot_general(a, b, (((1,), (1,)), ((), ())), preferred_element_type=F32)


def _split_specs(width, joint):
    off = PROMPT_BLOCKS if joint else 0
    return (pl.BlockSpec((TOK_BLOCK, width), lambda i: (jnp.minimum(i, PROMPT_BLOCKS - 1), 0)),
            pl.BlockSpec((TOK_BLOCK, width), lambda i: (jnp.maximum(i - PROMPT_BLOCKS, 0) + off, 0)))


def _pick_part(prompt_ref, sample_ref):
    return jnp.where(pl.program_id(0) < PROMPT_BLOCKS, prompt_ref[...], sample_ref[...])


def _parts(x):
    return (x[0], x[1], False) if isinstance(x, tuple) else (x, x, True)


def _proj_kernel(xa_ref, xb_ref, g_ref, w_ref, q_ref, k_ref, v_ref, gb_ref, gc_ref, gu_ref):
    hb = _rms(_pick_part(xa_ref, xb_ref), g_ref[...]).astype(BF16)
    outs = (q_ref, k_ref, v_ref, gb_ref, gc_ref, gu_ref)
    for j, o_ref in enumerate(outs):
        z = jnp.dot(hb, w_ref[:, j * ATTN_W:(j + 1) * ATTN_W], preferred_element_type=F32)
        if j == 0:
            z = z * (HEAD_DIM ** -0.5)
        o_ref[...] = z.astype(o_ref.dtype)


def _proj(x, g, w_in):
    xa, xb, joint = _parts(x)
    tok = pl.BlockSpec((TOK_BLOCK, ATTN_W), lambda i: (i, 0))
    return pl.pallas_call(
        _proj_kernel,
        grid=(N_BLOCKS,),
        in_specs=[*_split_specs(D_MODEL, joint),
                  pl.BlockSpec((1, D_MODEL), lambda i: (0, 0)),
                  pl.BlockSpec((D_MODEL, 6 * ATTN_W), lambda i: (0, 0))],
        out_specs=[tok] * 6,
        out_shape=[jax.ShapeDtypeStruct((N_TOK, ATTN_W), BF16)] * 3
                  + [jax.ShapeDtypeStruct((N_TOK, ATTN_W), F32)] * 3,
        compiler_params=pltpu.CompilerParams(dimension_semantics=("arbitrary",),
                                             vmem_limit_bytes=VMEM_LIMIT),
        name="proj",
    )(xa, xb, g, w_in)


def _group_norm(xv, gmat, g):
    sq = xv * xv
    hi = sq.astype(BF16)
    lo = (sq - hi.astype(F32)).astype(BF16)
    ms = (jnp.dot(hi, gmat, preferred_element_type=F32)
          + jnp.dot(lo, gmat, preferred_element_type=F32)) * (1.0 / HEAD_DIM)
    return xv * lax.rsqrt(ms + EPS) * g


def _mixer_kernel(xa_ref, xb_ref, q_ref, kp_ref, kc_ref, kn_ref, vp_ref, vc_ref, vn_ref,
                  gb_ref, gc_ref, gu_ref, gcp_ref, gup_ref, gcn_ref, gun_ref,
                  bias_ref, cw_ref, ga_ref, gcv_ref, gmat_ref, wo_ref,
                  o_ref, kbuf, vbuf, abuf):
    i = pl.program_id(0)
    is_prompt = i < PROMPT_BLOCKS
    sample = (i - PROMPT_BLOCKS) // SAMPLE_BLOCKS
    seq_r0 = jnp.where(is_prompt, 0, PROMPT_ROWS + SAMPLE_ROWS * sample)
    seq_r1 = jnp.where(is_prompt, PROMPT_ROWS, seq_r0 + SAMPLE_ROWS)
    blk_r0 = ROW_BLOCK * i

    kbuf[0:TOK_BLOCK, :] = kp_ref[...]
    kbuf[TOK_BLOCK:2 * TOK_BLOCK, :] = kc_ref[...]
    kbuf[2 * TOK_BLOCK:3 * TOK_BLOCK, :] = kn_ref[...]
    vbuf[0:TOK_BLOCK, :] = vp_ref[...]
    vbuf[TOK_BLOCK:2 * TOK_BLOCK, :] = vc_ref[...]
    vbuf[2 * TOK_BLOCK:3 * TOK_BLOCK, :] = vn_ref[...]

    lane = lax.broadcasted_iota(jnp.int32, (GRID_W, LANES), 1)
    low_half = lane < HEAD_DIM
    n_win = WIN_ROWS * GRID_W
    for jr in range(ROW_BLOCK):
        r = blk_r0 + jr
        rs = jnp.clip(r - WIN_ROWS // 2, seq_r0, seq_r1 - WIN_ROWS)
        delta = r - rs
        start = pl.multiple_of((rs - blk_r0 + ROW_BLOCK) * GRID_W, GRID_W)
        pairs = range(ATTN_HEADS // 2)
        col = [slice(p * LANES, (p + 1) * LANES) for p in pairs]
        scores = []
        for p in pairs:
            qp = q_ref[jr * GRID_W:(jr + 1) * GRID_W, col[p]]
            zero = jnp.zeros_like(qp)
            q2 = jnp.concatenate([jnp.where(low_half, qp, zero), jnp.where(low_half, zero, qp)], axis=0)
            scores.append(_dot_nt(q2, kbuf[pl.ds(start, n_win), col[p]]) + bias_ref[delta, p])
        probs, norms = [], []
        for s in scores:
            e = jnp.exp(s - jnp.max(s, axis=-1, keepdims=True))
            probs.append(e.astype(BF16))
            norms.append(jnp.sum(e, axis=-1, keepdims=True))
        for p in pairs:
            o2 = jnp.dot(probs[p], vbuf[pl.ds(start, n_win), col[p]], preferred_element_type=F32) / norms[p]
            abuf[jr * GRID_W:(jr + 1) * GRID_W, col[p]] = jnp.where(low_half, o2[:GRID_W], o2[GRID_W:])

    first = blk_r0 == seq_r0
    last = blk_r0 + ROW_BLOCK == seq_r1
    cu = gc_ref[...] * gu_ref[...]
    prev_row = jnp.where(first, 0.0, gcp_ref[HALO - 1:HALO, :] * gup_ref[HALO - 1:HALO, :])
    next_row = jnp.where(last, 0.0, gcn_ref[0:1, :] * gun_ref[0:1, :])
    row = lax.broadcasted_iota(jnp.int32, (TOK_BLOCK, CONV_W), 0)
    up_prev = jnp.where(row == 0, prev_row, pltpu.roll(cu, 1, axis=0))
    up_next = jnp.where(row == TOK_BLOCK - 1, next_row, pltpu.roll(cu, TOK_BLOCK - 1, axis=0))
    conv = gb_ref[...] * (up_prev * cw_ref[0:1, :] + cu * cw_ref[1:2, :] + up_next * cw_ref[2:3, :])

    gmat = gmat_ref[...]
    attn_n = _group_norm(abuf[...], gmat, ga_ref[...]).astype(BF16)
    conv_n = _group_norm(conv, gmat, gcv_ref[...]).astype(BF16)
    y = (jnp.dot(attn_n, wo_ref[0:ATTN_W, :], preferred_element_type=F32)
         + jnp.dot(conv_n, wo_ref[ATTN_W:D_MODEL, :], preferred_element_type=F32))
    o_ref[...] = _pick_part(xa_ref, xb_ref) + y


def _mixer(x, q, k, v, gb, gc, gu, bias, layer, conv_w, g_attn, g_conv, gmat, w_out):
    xa, xb, joint = _parts(x)
    cur = lambda i: (i, 0)
    prev = lambda i: (jnp.maximum(i - 1, 0), 0)
    nxt = lambda i: (jnp.minimum(i + 1, N_BLOCKS - 1), 0)
    halo_per_block = TOK_BLOCK // HALO
    hprev = lambda i: (jnp.maximum(i * halo_per_block - 1, 0), 0)
    hnext = lambda i: (jnp.minimum((i + 1) * halo_per_block, N_TOK // HALO - 1), 0)
    const2 = lambda i: (0, 0)
    blk = lambda m: pl.BlockSpec((TOK_BLOCK, ATTN_W), m)
    halo = lambda m: pl.BlockSpec((HALO, CONV_W), m)
    return pl.pallas_call(
        _mixer_kernel,
        grid=(N_BLOCKS,),
        in_specs=[*_split_specs(D_MODEL, joint),
                  blk(cur), blk(prev), blk(cur), blk(nxt), blk(prev), blk(cur), blk(nxt),
                  blk(cur), blk(cur), blk(cur), halo(hprev), halo(hprev), halo(hnext), halo(hnext),
                  pl.BlockSpec((None, WIN_ROWS, ATTN_HEADS // 2, 2 * GRID_W, WIN_ROWS * GRID_W),
                               lambda i: (layer, 0, 0, 0, 0)),
                  pl.BlockSpec((3, CONV_W), const2),
                  pl.BlockSpec((1, ATTN_W), const2),
                  pl.BlockSpec((1, CONV_W), const2),
                  pl.BlockSpec((ATTN_W, ATTN_W), const2),
                  pl.BlockSpec((D_MODEL, D_MODEL), const2)],
        out_specs=pl.BlockSpec((TOK_BLOCK, D_MODEL), cur),
        out_shape=jax.ShapeDtypeStruct((N_TOK, D_MODEL), F32),
        scratch_shapes=[pltpu.VMEM((3 * TOK_BLOCK, ATTN_W), BF16),
                        pltpu.VMEM((3 * TOK_BLOCK, ATTN_W), BF16),
                        pltpu.VMEM((TOK_BLOCK, ATTN_W), F32)],
        compiler_params=pltpu.CompilerParams(dimension_semantics=("arbitrary",),
                                             vmem_limit_bytes=VMEM_LIMIT),
        name="mixer",
    )(xa, xb, q, k, k, k, v, v, v, gb, gc, gu, gc, gu, gc, gu, bias, conv_w, g_attn, g_conv, gmat, w_out)


def _record(k, kidx, m, pick, vals, picks):
    sel = kidx == k
    return jnp.where(sel, m, vals), jnp.where(sel, pick, picks)


def _top_keys(s):
    n, t = s.shape
    half = n // 2
    r0 = lax.broadcasted_iota(jnp.int32, (half, t), 0).astype(F32)
    r1 = r0 + float(half)
    first = s[:half] >= s[half:]
    hi = jnp.where(first, s[:half], s[half:])
    lo = jnp.where(first, s[half:], s[:half])
    ihi = jnp.where(first, r0, r1)
    ilo = jnp.where(first, r1, r0)
    kidx = lax.broadcasted_iota(jnp.int32, (PEER_TOPK, t), 0)

    def body(k, carry):
        hi, lo, ihi, vals, picks = carry
        m = jnp.max(hi, axis=0, keepdims=True)
        pos = jnp.min(jnp.where(hi == m, ihi, float(n)), axis=0, keepdims=True)
        onehot = ihi == pos
        vals, picks = _record(k, kidx, m, pos, vals, picks)
        return (jnp.where(onehot, lo, hi), jnp.where(onehot, -jnp.inf, lo), jnp.where(onehot, ilo, ihi),
                vals, picks)

    zeros = jnp.zeros((PEER_TOPK, t), F32)
    out = lax.fori_loop(0, PEER_TOPK, body, (hi, lo, ihi, zeros, zeros), unroll=True)
    return out[3], out[4]


_CAND_PER_KA = [PEER_TOPK // (ka + 1) for ka in range(PEER_TOPK)]
_N_CAND = sum(_CAND_PER_KA)
_CAND_PAD = -_N_CAND % 8


def _top_candidates(v1, i1, v2, i2):
    t = v1.shape[1]
    s_rows, e_rows = [], []
    for ka, n_kb in enumerate(_CAND_PER_KA):
        s_rows.append(v1[ka:ka + 1, :] + v2[:n_kb, :])
        e_rows.append(i1[ka:ka + 1, :] * float(N_KEYS) + i2[:n_kb, :])
    s_rows.append(jnp.full((_CAND_PAD, t), -jnp.inf, F32))
    e_rows.append(jnp.zeros((_CAND_PAD, t), F32))
    s = jnp.concatenate(s_rows, axis=0)
    e = jnp.concatenate(e_rows, axis=0)
    rows = s.shape[0]
    ridx = lax.broadcasted_iota(jnp.int32, (rows, t), 0).astype(F32)
    kidx = lax.broadcasted_iota(jnp.int32, (PEER_TOPK, t), 0)

    def body(k, carry):
        s, vals, picks = carry
        m = jnp.max(s, axis=0, keepdims=True)
        pos = jnp.min(jnp.where(s == m, ridx, float(rows)), axis=0, keepdims=True)
        onehot = ridx == pos
        pick = jnp.max(jnp.where(onehot, e, -1.0), axis=0, keepdims=True)
        vals, picks = _record(k, kidx, m, pick, vals, picks)
        return jnp.where(onehot, -jnp.inf, s), vals, picks

    zeros = jnp.zeros((PEER_TOPK, t), F32)
    _, vals, picks = lax.fori_loop(0, PEER_TOPK, body, (s, zeros, zeros), unroll=True)
    return vals, picks


def _gelu_exact(x, half=0.5):
    return half * x * (1.0 + lax.erf(x * (2.0 ** -0.5)))


def _key_scores(hb, wq, k1, k2):
    half = D_KEY // 2
    qh = jnp.dot(hb, wq, preferred_element_type=F32).astype(BF16)
    return _dot_nt(k1, qh[:, :half]), _dot_nt(k2, qh[:, half:])


def _peer_kernel(xr_ref, xp_ref, g_ref, wq_ref, k1_ref, k2_ref, ut_ref, v_ref, o_ref,
                 h_buf, sc_buf, gt_buf, et_buf, a_buf, b_buf, gate_buf, wtok, acc):
    i = pl.program_id(0)
    c = pl.program_id(1)
    last_c = c == PEER_N_CHUNKS - 1
    route_slot = i % 2
    peer_slot = 1 - route_slot
    score_slot = c % 2
    h_peer_slot = (i + 2) % 3

    @pl.when(jnp.logical_and(i == 0, c == 0))
    def _():
        hb = _rms(xr_ref[...], g_ref[...]).astype(BF16)
        h_buf[0] = hb
        s1, s2 = _key_scores(hb, wq_ref[0], k1_ref[...], k2_ref[...])
        sc_buf[0, 0] = s1
        sc_buf[0, 1] = s2
        h_buf[2] = jnp.zeros((PEER_TOK, D_MODEL), BF16)
        gt_buf[1] = jnp.zeros((PEER_HEADS * PEER_TOPK, PEER_TOK), F32)
        et_buf[1] = jnp.zeros((PEER_HEADS * PEER_TOPK, PEER_TOK), F32)

    @pl.when(last_c)
    def _():
        h_buf[(i + 1) % 3] = _rms(xr_ref[...], g_ref[...]).astype(BF16)

    @pl.when(c == 0)
    def _():
        acc[...] = jnp.zeros_like(acc)
        e = et_buf[peer_slot].T
        a = jnp.floor(e * (1.0 / N_KEYS))
        a_buf[...] = a
        b_buf[...] = e - a * float(N_KEYS)
        gate_buf[...] = gt_buf[peer_slot].T
        sub = lax.broadcasted_iota(jnp.int32, (N_KEYS, LANES), 0).astype(F32)

        def tok_group(grp, carry):
            t0 = pl.multiple_of(grp * TOK_GROUP, TOK_GROUP)
            slabs = []
            for j in range(TOK_GROUP):
                ar = a_buf[pl.ds(t0 + j, 1), :]
                br = b_buf[pl.ds(t0 + j, 1), :]
                gr = gate_buf[pl.ds(t0 + j, 1), :]
                pg = jnp.where(sub == ar, gr, 0.0).astype(BF16)
                qb = jnp.where(sub == br, 1.0, 0.0).astype(BF16)
                slabs.append(_dot_nt(pg, qb))
            wtok[:, pl.ds(t0, TOK_GROUP), :] = jnp.swapaxes(jnp.stack(slabs), 0, 1)
            return carry

        lax.fori_loop(0, PEER_TOK // TOK_GROUP, tok_group, 0, unroll=16)

    v1, i1 = _top_keys(sc_buf[score_slot, 0])
    v2, i2 = _top_keys(sc_buf[score_slot, 1])

    h_next_slot = jnp.where(last_c, i + 1, i) % 3
    next_head = jnp.where(last_c, 0, c + 1)
    s1, s2 = _key_scores(h_buf[h_next_slot], wq_ref[next_head], k1_ref[...], k2_ref[...])
    sc_buf[1 - score_slot, 0] = s1
    sc_buf[1 - score_slot, 1] = s2

    a_act = jnp.dot(h_buf[h_peer_slot], ut_ref[...], preferred_element_type=F32)
    key0 = c * PEER_CHUNK_KEYS
    w = jnp.concatenate([wtok[key0 + j] for j in range(PEER_CHUNK_KEYS)], axis=1)
    half = 0.5 + jnp.where(i1[PEER_TOPK - 1:, 0:1] < -1.0, 1.0, 0.0)
    wg = w * _gelu_exact(a_act, half)
    acc[...] += jnp.dot(wg.astype(BF16), v_ref[...], preferred_element_type=F32)

    top_s, top_e = _top_candidates(v1, i1, v2, i2)
    ex = jnp.exp(top_s - jnp.max(top_s, axis=0, keepdims=True))
    rows = pl.ds(pl.multiple_of(c * PEER_TOPK, PEER_TOPK), PEER_TOPK)
    gt_buf[route_slot, rows, :] = ex / jnp.sum(ex, axis=0, keepdims=True)
    et_buf[route_slot, rows, :] = top_e

    @pl.when(last_c)
    def _():
        o_ref[...] = xp_ref[...] + acc[...]


def _peer(x, g, wq_heads, keys1, keys2, eut, ev, layer):
    n_blk = N_TOK // PEER_TOK
    hk = PEER_HEADS * PEER_TOPK
    route_blk = lambda i, c: (jnp.minimum(i + c // (PEER_N_CHUNKS - 1), n_blk - 1), 0)
    peer_blk = lambda i, c: (jnp.maximum(i - 1, 0), 0)
    const2 = lambda i, c: (0, 0)
    chunk = pl.BlockSpec((None, PEER_CHUNK, D_MODEL), lambda i, c: (layer, c, 0))
    chunk_t = pl.BlockSpec((None, D_MODEL, PEER_CHUNK), lambda i, c: (layer, 0, c))
    return pl.pallas_call(
        _peer_kernel,
        grid=(n_blk + 1, PEER_N_CHUNKS),
        in_specs=[pl.BlockSpec((PEER_TOK, D_MODEL), route_blk),
                  pl.BlockSpec((PEER_TOK, D_MODEL), peer_blk),
                  pl.BlockSpec((1, D_MODEL), const2),
                  pl.BlockSpec((PEER_HEADS, D_MODEL, D_KEY), lambda i, c: (0, 0, 0)),
                  pl.BlockSpec((N_KEYS, D_KEY // 2), const2),
                  pl.BlockSpec((N_KEYS, D_KEY // 2), const2),
                  chunk_t, chunk],
        out_specs=pl.BlockSpec((PEER_TOK, D_MODEL), peer_blk),
        out_shape=jax.ShapeDtypeStruct((N_TOK, D_MODEL), F32),
        scratch_shapes=[pltpu.VMEM((3, PEER_TOK, D_MODEL), BF16),
                        pltpu.VMEM((2, 2, N_KEYS, PEER_TOK), F32),
                        pltpu.VMEM((2, hk, PEER_TOK), F32),
                        pltpu.VMEM((2, hk, PEER_TOK), F32),
                        pltpu.VMEM((PEER_TOK, hk), F32),
                        pltpu.VMEM((PEER_TOK, hk), F32),
                        pltpu.VMEM((PEER_TOK, hk), F32),
                        pltpu.VMEM((N_KEYS, PEER_TOK, N_KEYS), F32),
                        pltpu.VMEM((PEER_TOK, D_MODEL), F32)],
        compiler_params=pltpu.CompilerParams(dimension_semantics=("arbitrary", "arbitrary"),
                                             vmem_limit_bytes=PEER_VMEM_LIMIT),
        name="peer",
    )(x, x, g, wq_heads, keys1, keys2, eut, ev)


def _ple_kernel(x_ref, pa_ref, pb_ref, g_ref, wg_ref, wp_ref, gf_ref, *o_refs, final):
    x = x_ref[...]
    hp = _rms(x, g_ref[...]).astype(BF16)
    gate = jax.nn.sigmoid(jnp.dot(hp, wg_ref[...], preferred_element_type=F32))
    proj = jnp.dot(_pick_part(pa_ref, pb_ref).astype(BF16), wp_ref[...], preferred_element_type=F32)
    y = x + gate * proj
    if not final:
        o_refs[0][...] = y
        return
    y = _rms(y, gf_ref[...])
    is_prompt = pl.program_id(0) < PROMPT_BLOCKS

    @pl.when(is_prompt)
    def _():
        o_refs[0][...] = y

    @pl.when(jnp.logical_not(is_prompt))
    def _():
        o_refs[1][...] = y


def _ple(x, p_prompt, p_sample, layer, g, w_gate, w_proj, g_final, final):
    const2 = lambda i: (0, 0)
    tok = pl.BlockSpec((TOK_BLOCK, D_MODEL), lambda i: (i, 0))
    p_specs = (pl.BlockSpec((None, TOK_BLOCK, PLE_DIM), lambda i: (layer, jnp.minimum(i, PROMPT_BLOCKS - 1), 0)),
               pl.BlockSpec((None, TOK_BLOCK, PLE_DIM), lambda i: (layer, jnp.maximum(i - PROMPT_BLOCKS, 0), 0)))
    if final:
        out_specs = list(_split_specs(D_MODEL, joint=False))
        out_shape = [jax.ShapeDtypeStruct((SEQ, D_MODEL), F32), jax.ShapeDtypeStruct((N_TOK - SEQ, D_MODEL), F32)]
    else:
        out_specs, out_shape = tok, jax.ShapeDtypeStruct((N_TOK, D_MODEL), F32)
    return pl.pallas_call(
        functools.partial(_ple_kernel, final=final),
        grid=(N_BLOCKS,),
        in_specs=[tok, *p_specs,
                  pl.BlockSpec((1, D_MODEL), const2),
                  pl.BlockSpec((D_MODEL, D_MODEL), const2),
                  pl.BlockSpec((PLE_DIM, D_MODEL), const2),
                  pl.BlockSpec((1, D_MODEL), const2)],
        out_specs=out_specs,
        out_shape=out_shape,
        compiler_params=pltpu.CompilerParams(dimension_semantics=("arbitrary",),
                                             vmem_limit_bytes=VMEM_LIMIT),
        name="ple",
    )(x, p_prompt, p_sample, g, w_gate, w_proj, g_final)


def _bias_tables(rpb):
    cols = np.arange(GRID_W)
    col_start = np.clip(cols - WIN_COLS // 2, 0, GRID_W - WIN_COLS)
    kc = np.arange(GRID_W)
    in_win = (kc[None, :] >= col_start[:, None]) & (kc[None, :] < col_start[:, None] + WIN_COLS)
    col_off = kc[None, :] - cols[:, None] + (WIN_COLS - 1)
    onehot = (col_off[None] == np.arange(2 * WIN_COLS - 1)[:, None, None]) & in_win[None]
    delta = np.arange(WIN_ROWS)
    j = np.arange(WIN_ROWS)
    row_off = j[None, :] - delta[:, None] + (WIN_ROWS - 1)
    rows = rpb[:, :, row_off]
    t = jnp.einsum('lhdjo,ock->ldhcjk', rows, jnp.asarray(onehot, F32), precision=lax.Precision.HIGHEST)
    t = t + jnp.asarray(np.where(in_win, 0.0, NEG_BIG), F32)[None, None, None, :, None, :]
    return t.reshape(DEPTH, WIN_ROWS, ATTN_HEADS // 2, 2 * GRID_W, WIN_ROWS * GRID_W)


def kernel(x_prompt, x_sample, p_prompt, p_sample, g_mix, w_in, rpb, conv_w, g_attn_out, g_conv_out, w_out,
           g_ffn, w_query, sub_keys1, sub_keys2, expert_u, expert_v, g_ple, w_ple_gate, w_ple_proj, g_final):
    x = (x_prompt.reshape(SEQ, D_MODEL), x_sample.reshape(DEC_BATCH * DEC_SEQ, D_MODEL))
    pp = p_prompt.reshape(DEPTH, SEQ, PLE_DIM)
    ps = p_sample.reshape(DEPTH, DEC_BATCH * DEC_SEQ, PLE_DIM)
    group = np.arange(ATTN_W) // HEAD_DIM
    gmat = jnp.asarray(group[:, None] == group[None, :], dtype=BF16)
    row = lambda g: g.reshape(1, -1)
    bias = _bias_tables(rpb)
    eut, ev = jnp.swapaxes(expert_u.astype(BF16), 1, 2), expert_v.astype(BF16)
    for i in range(DEPTH):
        q, k, v, gb, gc, gu = _proj(x, row(g_mix[i]), w_in[i].astype(BF16))
        x = _mixer(x, q, k, v, gb, gc, gu, bias, i, conv_w[i], row(g_attn_out[i]),
                   row(g_conv_out[i]), gmat, w_out[i].astype(BF16))
        wq_heads = w_query[i].astype(BF16).reshape(D_MODEL, PEER_HEADS, D_KEY).swapaxes(0, 1)
        x = _peer(x, row(g_ffn[i]), wq_heads, sub_keys1[i].astype(BF16),
                  sub_keys2[i].astype(BF16), eut, ev, i)
        x = _ple(x, pp, ps, i, row(g_ple[i]), w_ple_gate[i].astype(BF16), w_ple_proj[i].astype(BF16),
                 row(g_final), final=(i == DEPTH - 1))
    y_prompt, y_sample = x
    return (y_prompt.reshape(1, SEQ, D_MODEL), y_sample.reshape(DEC_BATCH, DEC_SEQ, D_MODEL))
```

```python
import functools

import numpy as np
import jax
import jax.numpy as jnp
from jax import lax
from jax.experimental import pallas as pl
from jax.experimental.pallas import tpu as pltpu

F32 = jnp.float32
BF16 = jnp.bfloat16

D_MODEL = 1024
DEPTH = 2
SEQ = 16384
DEC_BATCH = 4
DEC_SEQ = 4096
N_TOK = SEQ + DEC_BATCH * DEC_SEQ

GRID_W = 64
WIN_ROWS = 8
WIN_COLS = 16
ATTN_HEADS = 8
HEAD_DIM = 64
ATTN_W = ATTN_HEADS * HEAD_DIM
CONV_W = D_MODEL - ATTN_W
N_KEYS = 128
N_EXPERTS = N_KEYS * N_KEYS
PEER_HEADS = 8
D_KEY = 256
PEER_TOPK = 16
PLE_DIM = 256
EPS = 1e-6
NEG_BIG = -1e30

LANES = 128
ROW_BLOCK = WIN_ROWS
TOK_BLOCK = ROW_BLOCK * GRID_W
N_BLOCKS = N_TOK // TOK_BLOCK
PROMPT_ROWS = SEQ // GRID_W
SAMPLE_ROWS = DEC_SEQ // GRID_W
PROMPT_BLOCKS = PROMPT_ROWS // ROW_BLOCK
SAMPLE_BLOCKS = SAMPLE_ROWS // ROW_BLOCK
HALO = 8

PEER_TOK = 256
TOK_GROUP = 8
PEER_CHUNK_KEYS = 16
PEER_CHUNK = PEER_CHUNK_KEYS * N_KEYS
PEER_N_CHUNKS = N_EXPERTS // PEER_CHUNK
assert PEER_N_CHUNKS == PEER_HEADS

VMEM_LIMIT = 48 * 1024 * 1024
PEER_VMEM_LIMIT = 56 * 1024 * 1024


def _rms(x, g):
    return x * lax.rsqrt(jnp.mean(x * x, axis=-1, keepdims=True) + EPS) * g


def _dot_nt(a, b):
    return lax.dot_general(a, b, (((1,), (1,)), ((), ())), preferred_element_type=F32)


def _split_specs(width, joint):
    off = PROMPT_BLOCKS if joint else 0
    return (pl.BlockSpec((TOK_BLOCK, width), lambda i: (jnp.minimum(i, PROMPT_BLOCKS - 1), 0)),
            pl.BlockSpec((TOK_BLOCK, width), lambda i: (jnp.maximum(i - PROMPT_BLOCKS, 0) + off, 0)))


def _pick_part(prompt_ref, sample_ref):
    return jnp.where(pl.program_id(0) < PROMPT_BLOCKS, prompt_ref[...], sample_ref[...])


def _parts(x):
    return (x[0], x[1], False) if isinstance(x, tuple) else (x, x, True)


def _proj_kernel(xa_ref, xb_ref, g_ref, w_ref, q_ref, k_ref, v_ref, gb_ref, gc_ref, gu_ref):
    hb = _rms(_pick_part(xa_ref, xb_ref), g_ref[...]).astype(BF16)
    outs = (q_ref, k_ref, v_ref, gb_ref, gc_ref, gu_ref)
    for j, o_ref in enumerate(outs):
        z = jnp.dot(hb, w_ref[:, j * ATTN_W:(j + 1) * ATTN_W], preferred_element_type=F32)
        if j == 0:
            z = z * (HEAD_DIM ** -0.5)
        o_ref[...] = z.astype(o_ref.dtype)


def _proj(x, g, w_in):
    xa, xb, joint = _parts(x)
    tok = pl.BlockSpec((TOK_BLOCK, ATTN_W), lambda i: (i, 0))
    return pl.pallas_call(
        _proj_kernel,
        grid=(N_BLOCKS,),
        in_specs=[*_split_specs(D_MODEL, joint),
                  pl.BlockSpec((1, D_MODEL), lambda i: (0, 0)),
                  pl.BlockSpec((D_MODEL, 6 * ATTN_W), lambda i: (0, 0))],
        out_specs=[tok] * 6,
        out_shape=[jax.ShapeDtypeStruct((N_TOK, ATTN_W), BF16)] * 3
                  + [jax.ShapeDtypeStruct((N_TOK, ATTN_W), F32)] * 3,
        compiler_params=pltpu.CompilerParams(dimension_semantics=("arbitrary",),
                                             vmem_limit_bytes=VMEM_LIMIT),
        name="proj",
    )(xa, xb, g, w_in)


def _group_norm(xv, gmat, g):
    sq = xv * xv
    hi = sq.astype(BF16)
    lo = (sq - hi.astype(F32)).astype(BF16)
    ms = (jnp.dot(hi, gmat, preferred_element_type=F32)
          + jnp.dot(lo, gmat, preferred_element_type=F32)) * (1.0 / HEAD_DIM)
    return xv * lax.rsqrt(ms + EPS) * g


def _mixer_kernel(xa_ref, xb_ref, q_ref, kp_ref, kc_ref, kn_ref, vp_ref, vc_ref, vn_ref,
                  gb_ref, gc_ref, gu_ref, gcp_ref, gup_ref, gcn_ref, gun_ref,
                  bias_ref, cw_ref, ga_ref, gcv_ref, gmat_ref, wo_ref,
                  o_ref, kbuf, vbuf, abuf):
    i = pl.program_id(0)
    is_prompt = i < PROMPT_BLOCKS
    sample = (i - PROMPT_BLOCKS) // SAMPLE_BLOCKS
    seq_r0 = jnp.where(is_prompt, 0, PROMPT_ROWS + SAMPLE_ROWS * sample)
    seq_r1 = jnp.where(is_prompt, PROMPT_ROWS, seq_r0 + SAMPLE_ROWS)
    blk_r0 = ROW_BLOCK * i

    kbuf[0:TOK_BLOCK, :] = kp_ref[...]
    kbuf[TOK_BLOCK:2 * TOK_BLOCK, :] = kc_ref[...]
    kbuf[2 * TOK_BLOCK:3 * TOK_BLOCK, :] = kn_ref[...]
    vbuf[0:TOK_BLOCK, :] = vp_ref[...]
    vbuf[TOK_BLOCK:2 * TOK_BLOCK, :] = vc_ref[...]
    vbuf[2 * TOK_BLOCK:3 * TOK_BLOCK, :] = vn_ref[...]

    lane = lax.broadcasted_iota(jnp.int32, (GRID_W, LANES), 1)
    low_half = lane < HEAD_DIM
    n_win = WIN_ROWS * GRID_W
    for jr in range(ROW_BLOCK):
        r = blk_r0 + jr
        rs = jnp.clip(r - WIN_ROWS // 2, seq_r0, seq_r1 - WIN_ROWS)
        delta = r - rs
        start = pl.multiple_of((rs - blk_r0 + ROW_BLOCK) * GRID_W, GRID_W)
        pairs = range(ATTN_HEADS // 2)
        col = [slice(p * LANES, (p + 1) * LANES) for p in pairs]
        scores = []
        for p in pairs:
            qp = q_ref[jr * GRID_W:(jr + 1) * GRID_W, col[p]]
            zero = jnp.zeros_like(qp)
            q2 = jnp.concatenate([jnp.where(low_half, qp, zero), jnp.where(low_half, zero, qp)], axis=0)
            scores.append(_dot_nt(q2, kbuf[pl.ds(start, n_win), col[p]]) + bias_ref[delta, p])
        probs, norms = [], []
        for s in scores:
            e = jnp.exp(s - jnp.max(s, axis=-1, keepdims=True))
            probs.append(e.astype(BF16))
            norms.append(jnp.sum(e, axis=-1, keepdims=True))
        for p in pairs:
            o2 = jnp.dot(probs[p], vbuf[pl.ds(start, n_win), col[p]], preferred_element_type=F32) / norms[p]
            abuf[jr * GRID_W:(jr + 1) * GRID_W, col[p]] = jnp.where(low_half, o2[:GRID_W], o2[GRID_W:])

    first = blk_r0 == seq_r0
    last = blk_r0 + ROW_BLOCK == seq_r1
    cu = gc_ref[...] * gu_ref[...]
    prev_row = jnp.where(first, 0.0, gcp_ref[HALO - 1:HALO, :] * gup_ref[HALO - 1:HALO, :])
    next_row = jnp.where(last, 0.0, gcn_ref[0:1, :] * gun_ref[0:1, :])
    row = lax.broadcasted_iota(jnp.int32, (TOK_BLOCK, CONV_W), 0)
    up_prev = jnp.where(row == 0, prev_row, pltpu.roll(cu, 1, axis=0))
    up_next = jnp.where(row == TOK_BLOCK - 1, next_row, pltpu.roll(cu, TOK_BLOCK - 1, axis=0))
    conv = gb_ref[...] * (up_prev * cw_ref[0:1, :] + cu * cw_ref[1:2, :] + up_next * cw_ref[2:3, :])

    gmat = gmat_ref[...]
    attn_n = _group_norm(abuf[...], gmat, ga_ref[...]).astype(BF16)
    conv_n = _group_norm(conv, gmat, gcv_ref[...]).astype(BF16)
    y = (jnp.dot(attn_n, wo_ref[0:ATTN_W, :], preferred_element_type=F32)
         + jnp.dot(conv_n, wo_ref[ATTN_W:D_MODEL, :], preferred_element_type=F32))
    o_ref[...] = _pick_part(xa_ref, xb_ref) + y


def _mixer(x, q, k, v, gb, gc, gu, bias, layer, conv_w, g_attn, g_conv, gmat, w_out):
    xa, xb, joint = _parts(x)
    cur = lambda i: (i, 0)
    prev = lambda i: (jnp.maximum(i - 1, 0), 0)
    nxt = lambda i: (jnp.minimum(i + 1, N_BLOCKS - 1), 0)
    halo_per_block = TOK_BLOCK // HALO
    hprev = lambda i: (jnp.maximum(i * halo_per_block - 1, 0), 0)
    hnext = lambda i: (jnp.minimum((i + 1) * halo_per_block, N_TOK // HALO - 1), 0)
    const2 = lambda i: (0, 0)
    blk = lambda m: pl.BlockSpec((TOK_BLOCK, ATTN_W), m)
    halo = lambda m: pl.BlockSpec((HALO, CONV_W), m)
    return pl.pallas_call(
        _mixer_kernel,
        grid=(N_BLOCKS,),
        in_specs=[*_split_specs(D_MODEL, joint),
                  blk(cur), blk(prev), blk(cur), blk(nxt), blk(prev), blk(cur), blk(nxt),
                  blk(cur), blk(cur), blk(cur), halo(hprev), halo(hprev), halo(hnext), halo(hnext),
                  pl.BlockSpec((None, WIN_ROWS, ATTN_HEADS // 2, 2 * GRID_W, WIN_ROWS * GRID_W),
                               lambda i: (layer, 0, 0, 0, 0)),
                  pl.BlockSpec((3, CONV_W), const2),
                  pl.BlockSpec((1, ATTN_W), const2),
                  pl.BlockSpec((1, CONV_W), const2),
                  pl.BlockSpec((ATTN_W, ATTN_W), const2),
                  pl.BlockSpec((D_MODEL, D_MODEL), const2)],
        out_specs=pl.BlockSpec((TOK_BLOCK, D_MODEL), cur),
        out_shape=jax.ShapeDtypeStruct((N_TOK, D_MODEL), F32),
        scratch_shapes=[pltpu.VMEM((3 * TOK_BLOCK, ATTN_W), BF16),
                        pltpu.VMEM((3 * TOK_BLOCK, ATTN_W), BF16),
                        pltpu.VMEM((TOK_BLOCK, ATTN_W), F32)],
        compiler_params=pltpu.CompilerParams(dimension_semantics=("arbitrary",),
                                             vmem_limit_bytes=VMEM_LIMIT),
        name="mixer",
    )(xa, xb, q, k, k, k, v, v, v, gb, gc, gu, gc, gu, gc, gu, bias, conv_w, g_attn, g_conv, gmat, w_out)


def _record(k, kidx, m, pick, vals, picks):
    sel = kidx == k
    return jnp.where(sel, m, vals), jnp.where(sel, pick, picks)


def _top_keys(s):
    n, t = s.shape
    half = n // 2
    r0 = lax.broadcasted_iota(jnp.int32, (half, t), 0).astype(F32)
    r1 = r0 + float(half)
    first = s[:half] >= s[half:]
    hi = jnp.where(first, s[:half], s[half:])
    lo = jnp.where(first, s[half:], s[:half])
    ihi = jnp.where(first, r0, r1)
    ilo = jnp.where(first, r1, r0)
    kidx = lax.broadcasted_iota(jnp.int32, (PEER_TOPK, t), 0)

    def body(k, carry):
        hi, lo, ihi, vals, picks = carry
        m = jnp.max(hi, axis=0, keepdims=True)
        pos = jnp.min(jnp.where(hi == m, ihi, float(n)), axis=0, keepdims=True)
        onehot = ihi == pos
        vals, picks = _record(k, kidx, m, pos, vals, picks)
        return (jnp.where(onehot, lo, hi), jnp.where(onehot, -jnp.inf, lo), jnp.where(onehot, ilo, ihi),
                vals, picks)

    zeros = jnp.zeros((PEER_TOPK, t), F32)
    out = lax.fori_loop(0, PEER_TOPK, body, (hi, lo, ihi, zeros, zeros), unroll=True)
    return out[3], out[4]


_CAND_PER_KA = [PEER_TOPK // (ka + 1) for ka in range(PEER_TOPK)]
_N_CAND = sum(_CAND_PER_KA)
_CAND_PAD = -_N_CAND % 8


def _top_candidates(v1, i1, v2, i2):
    t = v1.shape[1]
    s_rows, e_rows = [], []
    for ka, n_kb in enumerate(_CAND_PER_KA):
        s_rows.append(v1[ka:ka + 1, :] + v2[:n_kb, :])
        e_rows.append(i1[ka:ka + 1, :] * float(N_KEYS) + i2[:n_kb, :])
    s_rows.append(jnp.full((_CAND_PAD, t), -jnp.inf, F32))
    e_rows.append(jnp.zeros((_CAND_PAD, t), F32))
    s = jnp.concatenate(s_rows, axis=0)
    e = jnp.concatenate(e_rows, axis=0)
    rows = s.shape[0]
    ridx = lax.broadcasted_iota(jnp.int32, (rows, t), 0).astype(F32)
    kidx = lax.broadcasted_iota(jnp.int32, (PEER_TOPK, t), 0)

    def body(k, carry):
        s, vals, picks = carry
        m = jnp.max(s, axis=0, keepdims=True)
        pos = jnp.min(jnp.where(s == m, ridx, float(rows)), axis=0, keepdims=True)
        onehot = ridx == pos
        pick = jnp.max(jnp.where(onehot, e, -1.0), axis=0, keepdims=True)
        vals, picks = _record(k, kidx, m, pick, vals, picks)
        return jnp.where(onehot, -jnp.inf, s), vals, picks

    zeros = jnp.zeros((PEER_TOPK, t), F32)
    _, vals, picks = lax.fori_loop(0, PEER_TOPK, body, (s, zeros, zeros), unroll=True)
    return vals, picks


def _gelu_exact(x, half=0.5):
    return half * x * (1.0 + lax.erf(x * (2.0 ** -0.5)))


def _key_scores(hb, wq, k1, k2):
    half = D_KEY // 2
    qh = jnp.dot(hb, wq, preferred_element_type=F32).astype(BF16)
    return _dot_nt(k1, qh[:, :half]), _dot_nt(k2, qh[:, half:])


def _peer_kernel(xr_ref, xp_ref, g_ref, wq_ref, k1_ref, k2_ref, ut_ref, v_ref, o_ref,
                 h_buf, sc_buf, gt_buf, et_buf, a_buf, b_buf, gate_buf, wtok):
    i = pl.program_id(0)
    c = pl.program_id(1)
    last_c = c == PEER_N_CHUNKS - 1
    route_slot = i % 2
    peer_slot = 1 - route_slot
    score_slot = c % 2
    h_peer_slot = (i + 2) % 3

    @pl.when(jnp.logical_and(i == 0, c == 0))
    def _():
        hb = _rms(xr_ref[...], g_ref[...]).astype(BF16)
        h_buf[0] = hb
        s1, s2 = _key_scores(hb, wq_ref[0], k1_ref[...], k2_ref[...])
        sc_buf[0, 0] = s1
        sc_buf[0, 1] = s2
        h_buf[2] = jnp.zeros((PEER_TOK, D_MODEL), BF16)
        gt_buf[1] = jnp.zeros((PEER_HEADS * PEER_TOPK, PEER_TOK), F32)
        et_buf[1] = jnp.zeros((PEER_HEADS * PEER_TOPK, PEER_TOK), F32)

    @pl.when(last_c)
    def _():
        h_buf[(i + 1) % 3] = _rms(xr_ref[...], g_ref[...]).astype(BF16)

    @pl.when(c == 0)
    def _():
        o_ref[...] = xp_ref[...]
        e = et_buf[peer_slot].T
        a = jnp.floor(e * (1.0 / N_KEYS))
        a_buf[...] = a
        b_buf[...] = e - a * float(N_KEYS)
        gate_buf[...] = gt_buf[peer_slot].T
        sub = lax.broadcasted_iota(jnp.int32, (N_KEYS, LANES), 0).astype(F32)

        def tok_group(grp, carry):
            t0 = pl.multiple_of(grp * TOK_GROUP, TOK_GROUP)
            slabs = []
            for j in range(TOK_GROUP):
                ar = a_buf[pl.ds(t0 + j, 1), :]
                br = b_buf[pl.ds(t0 + j, 1), :]
                gr = gate_buf[pl.ds(t0 + j, 1), :]
                pg = jnp.where(sub == ar, gr, 0.0).astype(BF16)
                qb = jnp.where(sub == br, 1.0, 0.0).astype(BF16)
                slabs.append(_dot_nt(pg, qb))
            wtok[:, pl.ds(t0, TOK_GROUP), :] = jnp.swapaxes(jnp.stack(slabs), 0, 1)
            return carry

        lax.fori_loop(0, PEER_TOK // TOK_GROUP, tok_group, 0, unroll=16)

    v1, i1 = _top_keys(sc_buf[score_slot, 0])
    v2, i2 = _top_keys(sc_buf[score_slot, 1])

    h_next_slot = jnp.where(last_c, i + 1, i) % 3
    next_head = jnp.where(last_c, 0, c + 1)
    s1, s2 = _key_scores(h_buf[h_next_slot], wq_ref[next_head], k1_ref[...], k2_ref[...])
    sc_buf[1 - score_slot, 0] = s1
    sc_buf[1 - score_slot, 1] = s2

    a_act = jnp.dot(h_buf[h_peer_slot], ut_ref[...], preferred_element_type=F32)
    key0 = c * PEER_CHUNK_KEYS
    w = jnp.concatenate([wtok[key0 + j] for j in range(PEER_CHUNK_KEYS)], axis=1)
    half = 0.5 + jnp.where(i1[PEER_TOPK - 1:, 0:1] < -1.0, 1.0, 0.0)
    wg = w * _gelu_exact(a_act, half)
    o_ref[...] += jnp.dot(wg.astype(BF16), v_ref[...], preferred_element_type=F32)

    top_s, top_e = _top_candidates(v1, i1, v2, i2)
    ex = jnp.exp(top_s - jnp.max(top_s, axis=0, keepdims=True))
    rows = pl.ds(pl.multiple_of(c * PEER_TOPK, PEER_TOPK), PEER_TOPK)
    gt_buf[route_slot, rows, :] = ex / jnp.sum(ex, axis=0, keepdims=True)
    et_buf[route_slot, rows, :] = top_e


def _peer(x, g, wq_heads, keys1, keys2, eut, ev, layer):
    n_blk = N_TOK // PEER_TOK
    hk = PEER_HEADS * PEER_TOPK
    route_blk = lambda i, c: (jnp.minimum(i + c // (PEER_N_CHUNKS - 1), n_blk - 1), 0)
    peer_blk = lambda i, c: (jnp.maximum(i - 1, 0), 0)
    const2 = lambda i, c: (0, 0)
    chunk = pl.BlockSpec((None, PEER_CHUNK, D_MODEL), lambda i, c: (layer, c, 0))
    chunk_t = pl.BlockSpec((None, D_MODEL, PEER_CHUNK), lambda i, c: (layer, 0, c))
    return pl.pallas_call(
        _peer_kernel,
        grid=(n_blk + 1, PEER_N_CHUNKS),
        in_specs=[pl.BlockSpec((PEER_TOK, D_MODEL), route_blk),
                  pl.BlockSpec((PEER_TOK, D_MODEL), peer_blk),
                  pl.BlockSpec((1, D_MODEL), const2),
                  pl.BlockSpec((PEER_HEADS, D_MODEL, D_KEY), lambda i, c: (0, 0, 0)),
                  pl.BlockSpec((N_KEYS, D_KEY // 2), const2),
                  pl.BlockSpec((N_KEYS, D_KEY // 2), const2),
                  chunk_t, chunk],
        out_specs=pl.BlockSpec((PEER_TOK, D_MODEL), peer_blk),
        out_shape=jax.ShapeDtypeStruct((N_TOK, D_MODEL), F32),
        scratch_shapes=[pltpu.VMEM((3, PEER_TOK, D_MODEL), BF16),
                        pltpu.VMEM((2, 2, N_KEYS, PEER_TOK), F32),
                        pltpu.VMEM((2, hk, PEER_TOK), F32),
                        pltpu.VMEM((2, hk, PEER_TOK), F32),
                        pltpu.VMEM((PEER_TOK, hk), F32),
                        pltpu.VMEM((PEER_TOK, hk), F32),
                        pltpu.VMEM((PEER_TOK, hk), F32),
                        pltpu.VMEM((N_KEYS, PEER_TOK, N_KEYS), F32)],
        compiler_params=pltpu.CompilerParams(dimension_semantics=("arbitrary", "arbitrary"),
                                             vmem_limit_bytes=PEER_VMEM_LIMIT),
        name="peer",
    )(x, x, g, wq_heads, keys1, keys2, eut, ev)


def _ple_kernel(x_ref, pa_ref, pb_ref, g_ref, wg_ref, wp_ref, gf_ref, *o_refs, final):
    x = x_ref[...]
    hp = _rms(x, g_ref[...]).astype(BF16)
    gate = jax.nn.sigmoid(jnp.dot(hp, wg_ref[...], preferred_element_type=F32))
    proj = jnp.dot(_pick_part(pa_ref, pb_ref).astype(BF16), wp_ref[...], preferred_element_type=F32)
    y = x + gate * proj
    if not final:
        o_refs[0][...] = y
        return
    y = _rms(y, gf_ref[...])
    is_prompt = pl.program_id(0) < PROMPT_BLOCKS

    @pl.when(is_prompt)
    def _():
        o_refs[0][...] = y

    @pl.when(jnp.logical_not(is_prompt))
    def _():
        o_refs[1][...] = y


def _ple(x, p_prompt, p_sample, layer, g, w_gate, w_proj, g_final, final):
    const2 = lambda i: (0, 0)
    tok = pl.BlockSpec((TOK_BLOCK, D_MODEL), lambda i: (i, 0))
    p_specs = (pl.BlockSpec((None, TOK_BLOCK, PLE_DIM), lambda i: (layer, jnp.minimum(i, PROMPT_BLOCKS - 1), 0)),
               pl.BlockSpec((None, TOK_BLOCK, PLE_DIM), lambda i: (layer, jnp.maximum(i - PROMPT_BLOCKS, 0), 0)))
    if final:
        out_specs = list(_split_specs(D_MODEL, joint=False))
        out_shape = [jax.ShapeDtypeStruct((SEQ, D_MODEL), F32), jax.ShapeDtypeStruct((N_TOK - SEQ, D_MODEL), F32)]
    else:
        out_specs, out_shape = tok, jax.ShapeDtypeStruct((N_TOK, D_MODEL), F32)
    return pl.pallas_call(
        functools.partial(_ple_kernel, final=final),
        grid=(N_BLOCKS,),
        in_specs=[tok, *p_specs,
                  pl.BlockSpec((1, D_MODEL), const2),
                  pl.BlockSpec((D_MODEL, D_MODEL), const2),
                  pl.BlockSpec((PLE_DIM, D_MODEL), const2),
                  pl.BlockSpec((1, D_MODEL), const2)],
        out_specs=out_specs,
        out_shape=out_shape,
        compiler_params=pltpu.CompilerParams(dimension_semantics=("arbitrary",),
                                             vmem_limit_bytes=VMEM_LIMIT),
        name="ple",
    )(x, p_prompt, p_sample, g, w_gate, w_proj, g_final)


def _bias_tables(rpb):
    cols = np.arange(GRID_W)
    col_start = np.clip(cols - WIN_COLS // 2, 0, GRID_W - WIN_COLS)
    kc = np.arange(GRID_W)
    in_win = (kc[None, :] >= col_start[:, None]) & (kc[None, :] < col_start[:, None] + WIN_COLS)
    col_off = kc[None, :] - cols[:, None] + (WIN_COLS - 1)
    onehot = (col_off[None] == np.arange(2 * WIN_COLS - 1)[:, None, None]) & in_win[None]
    delta = np.arange(WIN_ROWS)
    j = np.arange(WIN_ROWS)
    row_off = j[None, :] - delta[:, None] + (WIN_ROWS - 1)
    rows = rpb[:, :, row_off]
    t = jnp.einsum('lhdjo,ock->ldhcjk', rows, jnp.asarray(onehot, F32), precision=lax.Precision.HIGHEST)
    t = t + jnp.asarray(np.where(in_win, 0.0, NEG_BIG), F32)[None, None, None, :, None, :]
    return t.reshape(DEPTH, WIN_ROWS, ATTN_HEADS // 2, 2 * GRID_W, WIN_ROWS * GRID_W)


def kernel(x_prompt, x_sample, p_prompt, p_sample, g_mix, w_in, rpb, conv_w, g_attn_out, g_conv_out, w_out,
           g_ffn, w_query, sub_keys1, sub_keys2, expert_u, expert_v, g_ple, w_ple_gate, w_ple_proj, g_final):
    x = (x_prompt.reshape(SEQ, D_MODEL), x_sample.reshape(DEC_BATCH * DEC_SEQ, D_MODEL))
    pp = p_prompt.reshape(DEPTH, SEQ, PLE_DIM)
    ps = p_sample.reshape(DEPTH, DEC_BATCH * DEC_SEQ, PLE_DIM)
    group = np.arange(ATTN_W) // HEAD_DIM
    gmat = jnp.asarray(group[:, None] == group[None, :], dtype=BF16)
    row = lambda g: g.reshape(1, -1)
    bias = _bias_tables(rpb)
    eut, ev = jnp.swapaxes(expert_u.astype(BF16), 1, 2), expert_v.astype(BF16)
    for i in range(DEPTH):
        q, k, v, gb, gc, gu = _proj(x, row(g_mix[i]), w_in[i].astype(BF16))
        x = _mixer(x, q, k, v, gb, gc, gu, bias, i, conv_w[i], row(g_attn_out[i]),
                   row(g_conv_out[i]), gmat, w_out[i].astype(BF16))
        wq_heads = w_query[i].astype(BF16).reshape(D_MODEL, PEER_HEADS, D_KEY).swapaxes(0, 1)
        x = _peer(x, row(g_ffn[i]), wq_heads, sub_keys1[i].astype(BF16),
                  sub_keys2[i].astype(BF16), eut, ev, i)
        x = _ple(x, pp, ps, i, row(g_ple[i]), w_ple_gate[i].astype(BF16), w_ple_proj[i].astype(BF16),
                 row(g_final), final=(i == DEPTH - 1))
    y_prompt, y_sample = x
    return (y_prompt.reshape(1, SEQ, D_MODEL), y_sample.reshape(DEC_BATCH, DEC_SEQ, D_MODEL))
```

```python
import functools

import numpy as np
import jax
import jax.numpy as jnp
from jax import lax
from jax.experimental import pallas as pl
from jax.experimental.pallas import tpu as pltpu

F32 = jnp.float32
BF16 = jnp.bfloat16

D_MODEL = 1024
DEPTH = 2
SEQ = 16384
DEC_BATCH = 4
DEC_SEQ = 4096
N_TOK = SEQ + DEC_BATCH * DEC_SEQ

GRID_W = 64
WIN_ROWS = 8
WIN_COLS = 16
ATTN_HEADS = 8
HEAD_DIM = 64
ATTN_W = ATTN_HEADS * HEAD_DIM
CONV_W = D_MODEL - ATTN_W
N_KEYS = 128
N_EXPERTS = N_KEYS * N_KEYS
PEER_HEADS = 8
D_KEY = 256
PEER_TOPK = 16
PLE_DIM = 256
EPS = 1e-6
NEG_BIG = -1e30

LANES = 128
ROW_BLOCK = WIN_ROWS
TOK_BLOCK = ROW_BLOCK * GRID_W
N_BLOCKS = N_TOK // TOK_BLOCK
PROMPT_ROWS = SEQ // GRID_W
SAMPLE_ROWS = DEC_SEQ // GRID_W
PROMPT_BLOCKS = PROMPT_ROWS // ROW_BLOCK
SAMPLE_BLOCKS = SAMPLE_ROWS // ROW_BLOCK
HALO = 8

PEER_TOK = 256
TOK_GROUP = 8
PEER_CHUNK_KEYS = 16
PEER_CHUNK = PEER_CHUNK_KEYS * N_KEYS
PEER_N_CHUNKS = N_EXPERTS // PEER_CHUNK
assert PEER_N_CHUNKS == PEER_HEADS

VMEM_LIMIT = 48 * 1024 * 1024
PEER_VMEM_LIMIT = 56 * 1024 * 1024


def _rms(x, g):
    return x * lax.rsqrt(jnp.mean(x * x, axis=-1, keepdims=True) + EPS) * g


def _dot_nt(a, b):
    return lax.dot_general(a, b, (((1,), (1,)), ((), ())), preferred_element_type=F32)


def _split_specs(width, joint):
    off = PROMPT_BLOCKS if joint else 0
    return (pl.BlockSpec((TOK_BLOCK, width), lambda i: (jnp.minimum(i, PROMPT_BLOCKS - 1), 0)),
            pl.BlockSpec((TOK_BLOCK, width), lambda i: (jnp.maximum(i - PROMPT_BLOCKS, 0) + off, 0)))


def _pick_part(prompt_ref, sample_ref):
    return jnp.where(pl.program_id(0) < PROMPT_BLOCKS, prompt_ref[...], sample_ref[...])


def _parts(x):
    return (x[0], x[1], False) if isinstance(x, tuple) else (x, x, True)


def _proj_kernel(xa_ref, xb_ref, g_ref, w_ref, q_ref, k_ref, v_ref, gb_ref, gc_ref, gu_ref):
    hb = _rms(_pick_part(xa_ref, xb_ref), g_ref[...]).astype(BF16)
    outs = (q_ref, k_ref, v_ref, gb_ref, gc_ref, gu_ref)
    for j, o_ref in enumerate(outs):
        z = jnp.dot(hb, w_ref[:, j * ATTN_W:(j + 1) * ATTN_W], preferred_element_type=F32)
        if j == 0:
            z = z * (HEAD_DIM ** -0.5)
        o_ref[...] = z.astype(o_ref.dtype)


def _proj(x, g, w_in):
    xa, xb, joint = _parts(x)
    tok = pl.BlockSpec((TOK_BLOCK, ATTN_W), lambda i: (i, 0))
    return pl.pallas_call(
        _proj_kernel,
        grid=(N_BLOCKS,),
        in_specs=[*_split_specs(D_MODEL, joint),
                  pl.BlockSpec((1, D_MODEL), lambda i: (0, 0)),
                  pl.BlockSpec((D_MODEL, 6 * ATTN_W), lambda i: (0, 0))],
        out_specs=[tok] * 6,
        out_shape=[jax.ShapeDtypeStruct((N_TOK, ATTN_W), BF16)] * 3
                  + [jax.ShapeDtypeStruct((N_TOK, ATTN_W), F32)] * 3,
        compiler_params=pltpu.CompilerParams(dimension_semantics=("arbitrary",),
                                             vmem_limit_bytes=VMEM_LIMIT),
        name="proj",
    )(xa, xb, g, w_in)


def _group_norm(xv, gmat, g):
    sq = xv * xv
    hi = sq.astype(BF16)
    lo = (sq - hi.astype(F32)).astype(BF16)
    ms = (jnp.dot(hi, gmat, preferred_element_type=F32)
          + jnp.dot(lo, gmat, preferred_element_type=F32)) * (1.0 / HEAD_DIM)
    return xv * lax.rsqrt(ms + EPS) * g


def _mixer_kernel(xa_ref, xb_ref, q_ref, kp_ref, kc_ref, kn_ref, vp_ref, vc_ref, vn_ref,
                  gb_ref, gc_ref, gu_ref, gcp_ref, gup_ref, gcn_ref, gun_ref,
                  bias_ref, cw_ref, ga_ref, gcv_ref, gmat_ref, wo_ref,
                  o_ref, kbuf, vbuf, abuf):
    i = pl.program_id(0)
    is_prompt = i < PROMPT_BLOCKS
    sample = (i - PROMPT_BLOCKS) // SAMPLE_BLOCKS
    seq_r0 = jnp.where(is_prompt, 0, PROMPT_ROWS + SAMPLE_ROWS * sample)
    seq_r1 = jnp.where(is_prompt, PROMPT_ROWS, seq_r0 + SAMPLE_ROWS)
    blk_r0 = ROW_BLOCK * i

    kbuf[0:TOK_BLOCK, :] = kp_ref[...]
    kbuf[TOK_BLOCK:2 * TOK_BLOCK, :] = kc_ref[...]
    kbuf[2 * TOK_BLOCK:3 * TOK_BLOCK, :] = kn_ref[...]
    vbuf[0:TOK_BLOCK, :] = vp_ref[...]
    vbuf[TOK_BLOCK:2 * TOK_BLOCK, :] = vc_ref[...]
    vbuf[2 * TOK_BLOCK:3 * TOK_BLOCK, :] = vn_ref[...]

    lane = lax.broadcasted_iota(jnp.int32, (GRID_W, LANES), 1)
    low_half = lane < HEAD_DIM
    n_win = WIN_ROWS * GRID_W
    for jr in range(ROW_BLOCK):
        r = blk_r0 + jr
        rs = jnp.clip(r - WIN_ROWS // 2, seq_r0, seq_r1 - WIN_ROWS)
        delta = r - rs
        start = pl.multiple_of((rs - blk_r0 + ROW_BLOCK) * GRID_W, GRID_W)
        pairs = range(ATTN_HEADS // 2)
        col = [slice(p * LANES, (p + 1) * LANES) for p in pairs]
        scores = []
        for p in pairs:
            qp = q_ref[jr * GRID_W:(jr + 1) * GRID_W, col[p]]
            zero = jnp.zeros_like(qp)
            q2 = jnp.concatenate([jnp.where(low_half, qp, zero), jnp.where(low_half, zero, qp)], axis=0)
            scores.append(_dot_nt(q2, kbuf[pl.ds(start, n_win), col[p]]) + bias_ref[delta, p])
        probs, norms = [], []
        for s in scores:
            e = jnp.exp(s - jnp.max(s, axis=-1, keepdims=True))
            probs.append(e.astype(BF16))
            norms.append(jnp.sum(e, axis=-1, keepdims=True))
        for p in pairs:
            o2 = jnp.dot(probs[p], vbuf[pl.ds(start, n_win), col[p]], preferred_element_type=F32) / norms[p]
            abuf[jr * GRID_W:(jr + 1) * GRID_W, col[p]] = jnp.where(low_half, o2[:GRID_W], o2[GRID_W:])

    first = blk_r0 == seq_r0
    last = blk_r0 + ROW_BLOCK == seq_r1
    cu = gc_ref[...] * gu_ref[...]
    prev_row = jnp.where(first, 0.0, gcp_ref[HALO - 1:HALO, :] * gup_ref[HALO - 1:HALO, :])
    next_row = jnp.where(last, 0.0, gcn_ref[0:1, :] * gun_ref[0:1, :])
    row = lax.broadcasted_iota(jnp.int32, (TOK_BLOCK, CONV_W), 0)
    up_prev = jnp.where(row == 0, prev_row, pltpu.roll(cu, 1, axis=0))
    up_next = jnp.where(row == TOK_BLOCK - 1, next_row, pltpu.roll(cu, TOK_BLOCK - 1, axis=0))
    conv = gb_ref[...] * (up_prev * cw_ref[0:1, :] + cu * cw_ref[1:2, :] + up_next * cw_ref[2:3, :])

    gmat = gmat_ref[...]
    attn_n = _group_norm(abuf[...], gmat, ga_ref[...]).astype(BF16)
    conv_n = _group_norm(conv, gmat, gcv_ref[...]).astype(BF16)
    y = (jnp.dot(attn_n, wo_ref[0:ATTN_W, :], preferred_element_type=F32)
         + jnp.dot(conv_n, wo_ref[ATTN_W:D_MODEL, :], preferred_element_type=F32))
    o_ref[...] = _pick_part(xa_ref, xb_ref) + y


def _mixer(x, q, k, v, gb, gc, gu, bias, layer, conv_w, g_attn, g_conv, gmat, w_out):
    xa, xb, joint = _parts(x)
    cur = lambda i: (i, 0)
    prev = lambda i: (jnp.maximum(i - 1, 0), 0)
    nxt = lambda i: (jnp.minimum(i + 1, N_BLOCKS - 1), 0)
    halo_per_block = TOK_BLOCK // HALO
    hprev = lambda i: (jnp.maximum(i * halo_per_block - 1, 0), 0)
    hnext = lambda i: (jnp.minimum((i + 1) * halo_per_block, N_TOK // HALO - 1), 0)
    const2 = lambda i: (0, 0)
    blk = lambda m: pl.BlockSpec((TOK_BLOCK, ATTN_W), m)
    halo = lambda m: pl.BlockSpec((HALO, CONV_W), m)
    return pl.pallas_call(
        _mixer_kernel,
        grid=(N_BLOCKS,),
        in_specs=[*_split_specs(D_MODEL, joint),
                  blk(cur), blk(prev), blk(cur), blk(nxt), blk(prev), blk(cur), blk(nxt),
                  blk(cur), blk(cur), blk(cur), halo(hprev), halo(hprev), halo(hnext), halo(hnext),
                  pl.BlockSpec((None, WIN_ROWS, ATTN_HEADS // 2, 2 * GRID_W, WIN_ROWS * GRID_W),
                               lambda i: (layer, 0, 0, 0, 0)),
                  pl.BlockSpec((3, CONV_W), const2),
                  pl.BlockSpec((1, ATTN_W), const2),
                  pl.BlockSpec((1, CONV_W), const2),
                  pl.BlockSpec((ATTN_W, ATTN_W), const2),
                  pl.BlockSpec((D_MODEL, D_MODEL), const2)],
        out_specs=pl.BlockSpec((TOK_BLOCK, D_MODEL), cur),
        out_shape=jax.ShapeDtypeStruct((N_TOK, D_MODEL), F32),
        scratch_shapes=[pltpu.VMEM((3 * TOK_BLOCK, ATTN_W), BF16),
                        pltpu.VMEM((3 * TOK_BLOCK, ATTN_W), BF16),
                        pltpu.VMEM((TOK_BLOCK, ATTN_W), F32)],
        compiler_params=pltpu.CompilerParams(dimension_semantics=("arbitrary",),
                                             vmem_limit_bytes=VMEM_LIMIT),
        name="mixer",
    )(xa, xb, q, k, k, k, v, v, v, gb, gc, gu, gc, gu, gc, gu, bias, conv_w, g_attn, g_conv, gmat, w_out)


def _record(k, kidx, m, pick, vals, picks):
    sel = kidx == k
    return jnp.where(sel, m, vals), jnp.where(sel, pick, picks)


def _top_keys(s):
    n, t = s.shape
    half = n // 2
    r0 = lax.broadcasted_iota(jnp.int32, (half, t), 0).astype(F32)
    r1 = r0 + float(half)
    first = s[:half] >= s[half:]
    hi = jnp.where(first, s[:half], s[half:])
    lo = jnp.where(first, s[half:], s[:half])
    ihi = jnp.where(first, r0, r1)
    ilo = jnp.where(first, r1, r0)
    kidx = lax.broadcasted_iota(jnp.int32, (PEER_TOPK, t), 0)

    def body(k, carry):
        hi, lo, ihi, vals, picks = carry
        m = jnp.max(hi, axis=0, keepdims=True)
        pos = jnp.min(jnp.where(hi == m, ihi, float(n)), axis=0, keepdims=True)
        onehot = ihi == pos
        vals, picks = _record(k, kidx, m, pos, vals, picks)
        return (jnp.where(onehot, lo, hi), jnp.where(onehot, -jnp.inf, lo), jnp.where(onehot, ilo, ihi),
                vals, picks)

    zeros = jnp.zeros((PEER_TOPK, t), F32)
    out = lax.fori_loop(0, PEER_TOPK, body, (hi, lo, ihi, zeros, zeros), unroll=True)
    return out[3], out[4]


_CAND_PER_KA = [PEER_TOPK // (ka + 1) for ka in range(PEER_TOPK)]
_N_CAND = sum(_CAND_PER_KA)
_CAND_PAD = -_N_CAND % 8


def _top_candidates(v1, i1, v2, i2):
    t = v1.shape[1]
    s_rows, e_rows = [], []
    for ka, n_kb in enumerate(_CAND_PER_KA):
        s_rows.append(v1[ka:ka + 1, :] + v2[:n_kb, :])
        e_rows.append(i1[ka:ka + 1, :] * float(N_KEYS) + i2[:n_kb, :])
    s_rows.append(jnp.full((_CAND_PAD, t), -jnp.inf, F32))
    e_rows.append(jnp.zeros((_CAND_PAD, t), F32))
    s = jnp.concatenate(s_rows, axis=0)
    e = jnp.concatenate(e_rows, axis=0)
    rows = s.shape[0]
    ridx = lax.broadcasted_iota(jnp.int32, (rows, t), 0).astype(F32)
    kidx = lax.broadcasted_iota(jnp.int32, (PEER_TOPK, t), 0)

    def body(k, carry):
        s, vals, picks = carry
        m = jnp.max(s, axis=0, keepdims=True)
        pos = jnp.min(jnp.where(s == m, ridx, float(rows)), axis=0, keepdims=True)
        onehot = ridx == pos
        pick = jnp.max(jnp.where(onehot, e, -1.0), axis=0, keepdims=True)
        vals, picks = _record(k, kidx, m, pick, vals, picks)
        return jnp.where(onehot, -jnp.inf, s), vals, picks

    zeros = jnp.zeros((PEER_TOPK, t), F32)
    _, vals, picks = lax.fori_loop(0, PEER_TOPK, body, (s, zeros, zeros), unroll=True)
    return vals, picks


def _gelu_exact(x, half=0.5):
    return half * x * (1.0 + lax.erf(x * (2.0 ** -0.5)))


def _key_scores(hb, wq, k1, k2):
    half = D_KEY // 2
    qh = jnp.dot(hb, wq, preferred_element_type=F32).astype(BF16)
    return _dot_nt(k1, qh[:, :half]), _dot_nt(k2, qh[:, half:])


def _peer_kernel(xr_ref, xp_ref, g_ref, wq_ref, k1_ref, k2_ref, ut_ref, v_ref, o_ref,
                 h_buf, sc_buf, gt_buf, et_buf, a_buf, b_buf, gate_buf, wtok, acc):
    i = pl.program_id(0)
    c = pl.program_id(1)
    last_c = c == PEER_N_CHUNKS - 1
    route_slot = i % 2
    peer_slot = 1 - route_slot
    score_slot = c % 2
    h_peer_slot = (i + 2) % 3

    @pl.when(jnp.logical_and(i == 0, c == 0))
    def _():
        hb = _rms(xr_ref[...], g_ref[...]).astype(BF16)
        h_buf[0] = hb
        s1, s2 = _key_scores(hb, wq_ref[0], k1_ref[...], k2_ref[...])
        sc_buf[0, 0] = s1
        sc_buf[0, 1] = s2
        h_buf[2] = jnp.zeros((PEER_TOK, D_MODEL), BF16)
        gt_buf[1] = jnp.zeros((PEER_HEADS * PEER_TOPK, PEER_TOK), F32)
        et_buf[1] = jnp.zeros((PEER_HEADS * PEER_TOPK, PEER_TOK), F32)

    @pl.when(last_c)
    def _():
        h_buf[(i + 1) % 3] = _rms(xr_ref[...], g_ref[...]).astype(BF16)

    @pl.when(c == 0)
    def _():
        acc[...] = jnp.zeros_like(acc)
        e = et_buf[peer_slot].T
        a = jnp.floor(e * (1.0 / N_KEYS))
        a_buf[...] = a
        b_buf[...] = e - a * float(N_KEYS)
        gate_buf[...] = gt_buf[peer_slot].T
        sub = lax.broadcasted_iota(jnp.int32, (N_KEYS, LANES), 0).astype(F32)

        def tok_group(grp, carry):
            t0 = pl.multiple_of(grp * TOK_GROUP, TOK_GROUP)
            slabs = []
            for j in range(TOK_GROUP):
                ar = a_buf[pl.ds(t0 + j, 1), :]
                br = b_buf[pl.ds(t0 + j, 1), :]
                gr = gate_buf[pl.ds(t0 + j, 1), :]
                pg = jnp.where(sub == ar, gr, 0.0).astype(BF16)
                qb = jnp.where(sub == br, 1.0, 0.0).astype(BF16)
                slabs.append(_dot_nt(pg, qb))
            wtok[:, pl.ds(t0, TOK_GROUP), :] = jnp.swapaxes(jnp.stack(slabs), 0, 1)
            return carry

        lax.fori_loop(0, PEER_TOK // TOK_GROUP, tok_group, 0, unroll=16)

    v1, i1 = _top_keys(sc_buf[score_slot, 0])
    v2, i2 = _top_keys(sc_buf[score_slot, 1])

    h_next_slot = jnp.where(last_c, i + 1, i) % 3
    next_head = jnp.where(last_c, 0, c + 1)
    s1, s2 = _key_scores(h_buf[h_next_slot], wq_ref[next_head], k1_ref[...], k2_ref[...])
    sc_buf[1 - score_slot, 0] = s1
    sc_buf[1 - score_slot, 1] = s2

    a_act = jnp.dot(h_buf[h_peer_slot], ut_ref[...], preferred_element_type=F32)
    key0 = c * PEER_CHUNK_KEYS
    w = jnp.concatenate([wtok[key0 + j] for j in range(PEER_CHUNK_KEYS)], axis=1)
    half = 0.5 + jnp.where(i1[PEER_TOPK - 1:, 0:1] < -1.0, 1.0, 0.0)
    wg = w * _gelu_exact(a_act, half)
    acc[...] += jnp.dot(wg.astype(BF16), v_ref[...], preferred_element_type=F32)

    top_s, top_e = _top_candidates(v1, i1, v2, i2)
    ex = jnp.exp(top_s - jnp.max(top_s, axis=0, keepdims=True))
    rows = pl.ds(pl.multiple_of(c * PEER_TOPK, PEER_TOPK), PEER_TOPK)
    gt_buf[route_slot, rows, :] = ex / jnp.sum(ex, axis=0, keepdims=True)
    et_buf[route_slot, rows, :] = top_e

    @pl.when(last_c)
    def _():
        o_ref[...] = xp_ref[...] + acc[...]


def _peer(x, g, wq_heads, keys1, keys2, eut, ev, layer):
    n_blk = N_TOK // PEER_TOK
    hk = PEER_HEADS * PEER_TOPK
    route_blk = lambda i, c: (jnp.minimum(i + c // (PEER_N_CHUNKS - 1), n_blk - 1), 0)
    peer_blk = lambda i, c: (jnp.maximum(i - 1, 0), 0)
    const2 = lambda i, c: (0, 0)
    chunk = pl.BlockSpec((None, PEER_CHUNK, D_MODEL), lambda i, c: (layer, c, 0))
    chunk_t = pl.BlockSpec((None, D_MODEL, PEER_CHUNK), lambda i, c: (layer, 0, c))
    return pl.pallas_call(
        _peer_kernel,
        grid=(n_blk + 1, PEER_N_CHUNKS),
        in_specs=[pl.BlockSpec((PEER_TOK, D_MODEL), route_blk),
                  pl.BlockSpec((PEER_TOK, D_MODEL), peer_blk),
                  pl.BlockSpec((1, D_MODEL), const2),
                  pl.BlockSpec((PEER_HEADS, D_MODEL, D_KEY), lambda i, c: (0, 0, 0)),
                  pl.BlockSpec((N_KEYS, D_KEY // 2), const2),
                  pl.BlockSpec((N_KEYS, D_KEY // 2), const2),
                  chunk_t, chunk],
        out_specs=pl.BlockSpec((PEER_TOK, D_MODEL), peer_blk),
        out_shape=jax.ShapeDtypeStruct((N_TOK, D_MODEL), F32),
        scratch_shapes=[pltpu.VMEM((3, PEER_TOK, D_MODEL), BF16),
                        pltpu.VMEM((2, 2, N_KEYS, PEER_TOK), F32),
                        pltpu.VMEM((2, hk, PEER_TOK), F32),
                        pltpu.VMEM((2, hk, PEER_TOK), F32),
                        pltpu.VMEM((PEER_TOK, hk), F32),
                        pltpu.VMEM((PEER_TOK, hk), F32),
                        pltpu.VMEM((PEER_TOK, hk), F32),
                        pltpu.VMEM((N_KEYS, PEER_TOK, N_KEYS), F32),
                        pltpu.VMEM((PEER_TOK, D_MODEL), F32)],
        compiler_params=pltpu.CompilerParams(dimension_semantics=("arbitrary", "arbitrary"),
                                             vmem_limit_bytes=PEER_VMEM_LIMIT),
        name="peer",
    )(x, x, g, wq_heads, keys1, keys2, eut, ev)


def _ple_kernel(x_ref, pa_ref, pb_ref, g_ref, wg_ref, wp_ref, gf_ref, *o_refs, final):
    x = x_ref[...]
    hp = _rms(x, g_ref[...]).astype(BF16)
    gate = jax.nn.sigmoid(jnp.dot(hp, wg_ref[...], preferred_element_type=F32))
    proj = jnp.dot(_pick_part(pa_ref, pb_ref).astype(BF16), wp_ref[...], preferred_element_type=F32)
    y = x + gate * proj
    if not final:
        o_refs[0][...] = y
        return
    y = _rms(y, gf_ref[...])
    is_prompt = pl.program_id(0) < PROMPT_BLOCKS

    @pl.when(is_prompt)
    def _():
        o_refs[0][...] = y

    @pl.when(jnp.logical_not(is_prompt))
    def _():
        o_refs[1][...] = y


def _ple(x, p_prompt, p_sample, layer, g, w_gate, w_proj, g_final, final):
    const2 = lambda i: (0, 0)
    tok = pl.BlockSpec((TOK_BLOCK, D_MODEL), lambda i: (i, 0))
    p_specs = (pl.BlockSpec((None, TOK_BLOCK, PLE_DIM), lambda i: (layer, jnp.minimum(i, PROMPT_BLOCKS - 1), 0)),
               pl.BlockSpec((None, TOK_BLOCK, PLE_DIM), lambda i: (layer, jnp.maximum(i - PROMPT_BLOCKS, 0), 0)))
    if final:
        out_specs = list(_split_specs(D_MODEL, joint=False))
        out_shape = [jax.ShapeDtypeStruct((SEQ, D_MODEL), F32), jax.ShapeDtypeStruct((N_TOK - SEQ, D_MODEL), F32)]
    else:
        out_specs, out_shape = tok, jax.ShapeDtypeStruct((N_TOK, D_MODEL), F32)
    return pl.pallas_call(
        functools.partial(_ple_kernel, final=final),
        grid=(N_BLOCKS,),
        in_specs=[tok, *p_specs,
                  pl.BlockSpec((1, D_MODEL), const2),
                  pl.BlockSpec((D_MODEL, D_MODEL), const2),
                  pl.BlockSpec((PLE_DIM, D_MODEL), const2),
                  pl.BlockSpec((1, D_MODEL), const2)],
        out_specs=out_specs,
        out_shape=out_shape,
        compiler_params=pltpu.CompilerParams(dimension_semantics=("arbitrary",),
                                             vmem_limit_bytes=VMEM_LIMIT),
        name="ple",
    )(x, p_prompt, p_sample, g, w_gate, w_proj, g_final)


def _ple_proj_kernel(x_ref, pa_ref, pb_ref, g_ref, wg_ref, wp_ref, gm_ref, w_ref,
                     o_ref, q_ref, k_ref, v_ref, gb_ref, gc_ref, gu_ref):
    x = x_ref[...]
    hp = _rms(x, g_ref[...]).astype(BF16)
    gate = jax.nn.sigmoid(jnp.dot(hp, wg_ref[...], preferred_element_type=F32))
    proj = jnp.dot(_pick_part(pa_ref, pb_ref).astype(BF16), wp_ref[...], preferred_element_type=F32)
    y = x + gate * proj
    o_ref[...] = y
    hb = _rms(y, gm_ref[...]).astype(BF16)
    outs = (q_ref, k_ref, v_ref, gb_ref, gc_ref, gu_ref)
    for j, out in enumerate(outs):
        z = jnp.dot(hb, w_ref[:, j * ATTN_W:(j + 1) * ATTN_W], preferred_element_type=F32)
        if j == 0:
            z = z * (HEAD_DIM ** -0.5)
        out[...] = z.astype(out.dtype)


def _ple_proj(x, p_prompt, p_sample, layer, g, w_gate, w_proj, g_mix_next, w_in_next):
    const2 = lambda i: (0, 0)
    tok = pl.BlockSpec((TOK_BLOCK, D_MODEL), lambda i: (i, 0))
    part = pl.BlockSpec((TOK_BLOCK, ATTN_W), lambda i: (i, 0))
    p_specs = (pl.BlockSpec((None, TOK_BLOCK, PLE_DIM), lambda i: (layer, jnp.minimum(i, PROMPT_BLOCKS - 1), 0)),
               pl.BlockSpec((None, TOK_BLOCK, PLE_DIM), lambda i: (layer, jnp.maximum(i - PROMPT_BLOCKS, 0), 0)))
    return pl.pallas_call(
        _ple_proj_kernel,
        grid=(N_BLOCKS,),
        in_specs=[tok, *p_specs,
                  pl.BlockSpec((1, D_MODEL), const2),
                  pl.BlockSpec((D_MODEL, D_MODEL), const2),
                  pl.BlockSpec((PLE_DIM, D_MODEL), const2),
                  pl.BlockSpec((1, D_MODEL), const2),
                  pl.BlockSpec((D_MODEL, 6 * ATTN_W), const2)],
        out_specs=[tok] + [part] * 6,
        out_shape=[jax.ShapeDtypeStruct((N_TOK, D_MODEL), F32)]
                  + [jax.ShapeDtypeStruct((N_TOK, ATTN_W), BF16)] * 3
                  + [jax.ShapeDtypeStruct((N_TOK, ATTN_W), F32)] * 3,
        compiler_params=pltpu.CompilerParams(dimension_semantics=("arbitrary",),
                                             vmem_limit_bytes=VMEM_LIMIT),
        name="ple_proj",
    )(x, p_prompt, p_sample, g, w_gate, w_proj, g_mix_next, w_in_next)


def _bias_tables(rpb):
    cols = np.arange(GRID_W)
    col_start = np.clip(cols - WIN_COLS // 2, 0, GRID_W - WIN_COLS)
    kc = np.arange(GRID_W)
    in_win = (kc[None, :] >= col_start[:, None]) & (kc[None, :] < col_start[:, None] + WIN_COLS)
    col_off = kc[None, :] - cols[:, None] + (WIN_COLS - 1)
    onehot = (col_off[None] == np.arange(2 * WIN_COLS - 1)[:, None, None]) & in_win[None]
    delta = np.arange(WIN_ROWS)
    j = np.arange(WIN_ROWS)
    row_off = j[None, :] - delta[:, None] + (WIN_ROWS - 1)
    rows = rpb[:, :, row_off]
    t = jnp.einsum('lhdjo,ock->ldhcjk', rows, jnp.asarray(onehot, F32), precision=lax.Precision.HIGHEST)
    t = t + jnp.asarray(np.where(in_win, 0.0, NEG_BIG), F32)[None, None, None, :, None, :]
    return t.reshape(DEPTH, WIN_ROWS, ATTN_HEADS // 2, 2 * GRID_W, WIN_ROWS * GRID_W)


def kernel(x_prompt, x_sample, p_prompt, p_sample, g_mix, w_in, rpb, conv_w, g_attn_out, g_conv_out, w_out,
           g_ffn, w_query, sub_keys1, sub_keys2, expert_u, expert_v, g_ple, w_ple_gate, w_ple_proj, g_final):
    x = (x_prompt.reshape(SEQ, D_MODEL), x_sample.reshape(DEC_BATCH * DEC_SEQ, D_MODEL))
    pp = p_prompt.reshape(DEPTH, SEQ, PLE_DIM)
    ps = p_sample.reshape(DEPTH, DEC_BATCH * DEC_SEQ, PLE_DIM)
    group = np.arange(ATTN_W) // HEAD_DIM
    gmat = jnp.asarray(group[:, None] == group[None, :], dtype=BF16)
    row = lambda g: g.reshape(1, -1)
    bias = _bias_tables(rpb)
    eut, ev = jnp.swapaxes(expert_u.astype(BF16), 1, 2), expert_v.astype(BF16)
    q, k, v, gb, gc, gu = _proj(x, row(g_mix[0]), w_in[0].astype(BF16))
    for i in range(DEPTH):
        x = _mixer(x, q, k, v, gb, gc, gu, bias, i, conv_w[i], row(g_attn_out[i]),
                   row(g_conv_out[i]), gmat, w_out[i].astype(BF16))
        wq_heads = w_query[i].astype(BF16).reshape(D_MODEL, PEER_HEADS, D_KEY).swapaxes(0, 1)
        x = _peer(x, row(g_ffn[i]), wq_heads, sub_keys1[i].astype(BF16),
                  sub_keys2[i].astype(BF16), eut, ev, i)
        gate_w, proj_w = w_ple_gate[i].astype(BF16), w_ple_proj[i].astype(BF16)
        if i < DEPTH - 1:
            x, q, k, v, gb, gc, gu = _ple_proj(x, pp, ps, i, row(g_ple[i]), gate_w, proj_w,
                                               row(g_mix[i + 1]), w_in[i + 1].astype(BF16))
        else:
            x = _ple(x, pp, ps, i, row(g_ple[i]), gate_w, proj_w, row(g_final), final=True)
    y_prompt, y_sample = x
    return (y_prompt.reshape(1, SEQ, D_MODEL), y_sample.reshape(DEC_BATCH, DEC_SEQ, D_MODEL))
```
